```python
import math
import jax
import jax.numpy as jnp
from jax import lax
import numpy as np


D_MODEL = 1024
BATCH = 8
SEQ = 8192
DEPTH = 2

CHUNK = 64
Q_BLOCK = 128
MEM_LEN = 256
N_MIXERS = 2
ROPE_THETA = 500000.0
LN_EPS = 1e-5
DEEPNORM_ALPHA = (2.0 * DEPTH) ** 0.25
DEEPNORM_BETA = (8.0 * DEPTH) ** -0.25

DIFF_HEADS = 8
DIFF_HEAD_DIM = 64
DIFF_ROPE_DIM = DIFF_HEAD_DIM // 4

MLA_HEADS = 8
MLA_Q_RANK = 256
MLA_KV_RANK = 256
MLA_ROPE_DIM = 32
MLA_NOPE_DIM = 96
MLA_V_DIM = 128
IDX_HEADS = 16
IDX_DIM = 64
IDX_ROPE_DIM = IDX_DIM // 4
IDX_TOPK_MAX = 256

MEM_HEADS = 4
MEM_HEAD_DIM = D_MODEL // MEM_HEADS

N_GROUPS = 4
EXPERTS_PER_GROUP = 4
N_EXPERTS = N_GROUPS * EXPERTS_PER_GROUP
EXPERT_TOP_K = 2
EXPERT_FF = 512

N_DIFF_LAYERS = (DEPTH + 1) // 2
N_DSA_LAYERS = DEPTH // 2

kernel_name = "hybrid_diff_dsa_hmoe_deepnorm_encoder"


def layer_norm(x, g, b):
    xf = x.astype(jnp.float32)
    mu = jnp.mean(xf, axis=-1, keepdims=True)
    xc = xf - mu
    var = jnp.mean(xc * xc, axis=-1, keepdims=True)
    y = xc * lax.rsqrt(var + LN_EPS) * g.astype(jnp.float32) + b.astype(jnp.float32)
    return y.astype(x.dtype)


def rms_norm(x, g):
    xf = x.astype(jnp.float32)
    y = xf * lax.rsqrt(jnp.mean(xf * xf, axis=-1, keepdims=True) + LN_EPS) * g.astype(jnp.float32)
    return y.astype(x.dtype)


def rope_tables(positions, rot_dim):
    inv_freq = ROPE_THETA ** (-jnp.arange(0, rot_dim, 2, dtype=jnp.float32) / rot_dim)
    ang = positions.astype(jnp.float32)[..., None] * inv_freq
    return jnp.cos(ang), jnp.sin(ang)


def apply_partial_rope(x, cos, sin):
    half = cos.shape[-1]
    c = cos[:, :, None, :].astype(x.dtype)
    s = sin[:, :, None, :].astype(x.dtype)
    x1 = x[..., :half]
    x2 = x[..., half:2 * half]
    return jnp.concatenate([x1 * c - x2 * s, x2 * c + x1 * s, x[..., 2 * half:]], axis=-1)


def to_blocks(a, blk):
    b, s = a.shape[0], a.shape[1]
    return jnp.moveaxis(a.reshape((b, s // blk, blk) + a.shape[2:]), 1, 0)


def from_blocks(a):
    nb, b, blk = a.shape[0], a.shape[1], a.shape[2]
    return jnp.moveaxis(a, 0, 1).reshape((b, nb * blk) + a.shape[3:])


def diff_lambda_init(layer_idx):
    return 0.8 - 0.6 * math.exp(-0.3 * layer_idx)


def differential_attention(x, cos, sin, w_in, lam, subln_g, w_out, lambda_init):
    bsz, seq, _ = x.shape
    h, d = DIFF_HEADS, DIFF_HEAD_DIM
    q, k, v = jnp.split(x @ w_in, 3, axis=-1)
    q = apply_partial_rope(q.reshape(bsz, seq, 2 * h, d), cos, sin) * (d ** -0.5)
    k = apply_partial_rope(k.reshape(bsz, seq, 2 * h, d), cos, sin)
    q = q.reshape(bsz, seq, h, 2, d)
    k = k.reshape(bsz, seq, h, 2, d)
    v = v.reshape(bsz, seq, h, 2 * d)
    lamf = lam.astype(jnp.float32)
    lam_full = jnp.exp(jnp.sum(lamf[0] * lamf[1])) - jnp.exp(jnp.sum(lamf[2] * lamf[3])) + lambda_init
    key_chunk = jnp.arange(seq) // CHUNK

    def block(args):
        q_blk, blk_idx = args
        q_chunk = (blk_idx * Q_BLOCK + jnp.arange(Q_BLOCK)) // CHUNK
        allowed = key_chunk[None, :] <= q_chunk[:, None]
        s = jnp.einsum('bqhcd,bkhcd->bhcqk', q_blk, k).astype(jnp.float32)
        p = jax.nn.softmax(jnp.where(allowed, s, -jnp.inf), axis=-1)
        p_diff = p[:, :, 0] - lam_full * p[:, :, 1]
        return jnp.einsum('bhqk,bkhe->bqhe', p_diff.astype(v.dtype), v)

    o = from_blocks(lax.map(block, (to_blocks(q, Q_BLOCK), jnp.arange(seq // Q_BLOCK))))
    o = rms_norm(o, subln_g) * (1.0 - lambda_init)
    return o.reshape(bsz, seq, h * 2 * d) @ w_out


def dsa_sparse_mla(x, cos_r, sin_r, cos_i, sin_i, w_in, q_norm_g, kv_norm_g, w_uq, w_qidx, w_uk, w_uv, w_out):
    bsz, seq, _ = x.shape
    h = MLA_HEADS
    splits = [MLA_Q_RANK,
              MLA_Q_RANK + MLA_KV_RANK,
              MLA_Q_RANK + MLA_KV_RANK + MLA_ROPE_DIM,
              MLA_Q_RANK + MLA_KV_RANK + MLA_ROPE_DIM + IDX_DIM]
    c_q, c_kv, k_rope, k_idx, w_idx = jnp.split(x @ w_in, splits, axis=-1)
    c_q = rms_norm(c_q, q_norm_g)
    c_kv = rms_norm(c_kv, kv_norm_g)
    q = (c_q @ w_uq).reshape(bsz, seq, h, MLA_ROPE_DIM + MLA_NOPE_DIM)
    q_rope = apply_partial_rope(q[..., :MLA_ROPE_DIM], cos_r, sin_r)
    q_lat = jnp.einsum('bshn,hrn->bshr', q[..., MLA_ROPE_DIM:], w_uk)
    q_full = jnp.concatenate([q_lat, q_rope], axis=-1) * ((MLA_ROPE_DIM + MLA_NOPE_DIM) ** -0.5)
    k_rope = apply_partial_rope(k_rope[:, :, None, :], cos_r, sin_r)[:, :, 0]
    kv_lat = jnp.concatenate([c_kv, k_rope], axis=-1)
    q_idx = apply_partial_rope((c_q @ w_qidx).reshape(bsz, seq, IDX_HEADS, IDX_DIM), cos_i, sin_i)
    k_idx = apply_partial_rope(k_idx[:, :, None, :], cos_i, sin_i)[:, :, 0]
    w_idx = w_idx * ((IDX_HEADS * IDX_DIM) ** -0.5)
    top_k = min(IDX_TOPK_MAX, seq // 4)
    key_chunk = jnp.arange(seq) // CHUNK
    gather = jax.vmap(lambda table, idx: table[idx])

    def block(args):
        qf, qi, wi, chunk_idx = args
        logits = jnp.einsum('bqhd,bkd->bqhk', qi, k_idx)
        score = jnp.einsum('bqh,bqhk->bqk', wi, jax.nn.relu(logits)).astype(jnp.float32)
        score = jnp.where((key_chunk <= chunk_idx)[None, None, :], score, -jnp.inf)
        top_score, top_idx = lax.top_k(score, top_k)
        valid = jnp.isfinite(top_score)
        sel = gather(kv_lat, top_idx)
        s = jnp.einsum('bqhd,bqkd->bhqk', qf, sel).astype(jnp.float32)
        p = jax.nn.softmax(jnp.where(valid[:, None], s, -jnp.inf), axis=-1)
        return jnp.einsum('bhqk,bqkr->bqhr', p.astype(sel.dtype), sel[..., :MLA_KV_RANK])

    o_lat = from_blocks(lax.map(block, (to_blocks(q_full, CHUNK), to_blocks(q_idx, CHUNK),
                                        to_blocks(w_idx, CHUNK), jnp.arange(seq // CHUNK))))
    o = jnp.einsum('bshr,hrv->bshv', o_lat, w_uv).reshape(bsz, seq, h * MLA_V_DIM)
    return o @ w_out


def memory_cross_attention(x, mem_k, mem_v, w_q, w_out):
    bsz, seq, _ = x.shape
    q = (x @ w_q).reshape(bsz, seq, MEM_HEADS, MEM_HEAD_DIM) * (MEM_HEAD_DIM ** -0.5)
    s = jnp.einsum('bshd,bmhd->bhsm', q, mem_k).astype(jnp.float32)
    p = jax.nn.softmax(s, axis=-1)
    o = jnp.einsum('bhsm,bmhd->bshd', p.astype(mem_v.dtype), mem_v).reshape(bsz, seq, MEM_HEADS * MEM_HEAD_DIM)
    return o @ w_out


def hierarchical_moe(x, w_group, b_group, w_expert, b_expert, w_gate, w_up, w_down):
    bsz, seq, dm = x.shape
    n_tok = bsz * seq
    t = x.reshape(n_tok, dm)
    g_logits = (t @ w_group + b_group).astype(jnp.float32)
    g_sel = jnp.argmax(g_logits, axis=-1)
    g_gate = jnp.max(jax.nn.softmax(g_logits, axis=-1), axis=-1, keepdims=True)
    e_logits = (t @ w_expert + b_expert).astype(jnp.float32).reshape(n_tok, N_GROUPS, EXPERTS_PER_GROUP)
    e_logits = e_logits[jnp.arange(n_tok), g_sel]
    top_val, top_idx = lax.top_k(e_logits, EXPERT_TOP_K)
    top_w = jax.nn.softmax(top_val, axis=-1) * g_gate
    expert_id = g_sel[:, None] * EXPERTS_PER_GROUP + top_idx
    combine = jnp.sum(jax.nn.one_hot(expert_id, N_EXPERTS, dtype=jnp.float32) * top_w[..., None], axis=1).astype(t.dtype)
    y = jnp.zeros_like(t)
    for e in range(N_EXPERTS):
        hid = jax.nn.silu(t @ w_gate[e]) * (t @ w_up[e])
        y = y + combine[:, e:e + 1] * (hid @ w_down[e])
    return y.reshape(bsz, seq, dm)


def setup_inputs(seed: int = 0) -> dict:
    key = jax.random.key(seed)
    ks = iter(jax.random.split(key, 40))
    beta = DEEPNORM_BETA
    d = D_MODEL

    def nrm(shape, scale):
        return jax.random.normal(next(ks), shape, jnp.float32) * scale

    x = nrm((BATCH, SEQ, d), 1.0)
    mem = nrm((BATCH, MEM_LEN, d), 1.0)
    offsets = jax.random.randint(next(ks), (BATCH, 1), 0, 1024) * CHUNK
    positions = (offsets + jnp.arange(SEQ)[None, :]).astype(jnp.int32)

    diff_w = DIFF_HEADS * 2 * DIFF_HEAD_DIM
    a_w_in = jnp.concatenate([nrm((N_DIFF_LAYERS, d, 2 * diff_w), d ** -0.5),
                              nrm((N_DIFF_LAYERS, d, diff_w), beta * d ** -0.5)], axis=-1)
    a_lambda = nrm((N_DIFF_LAYERS, 4, DIFF_HEAD_DIM), 0.1)
    a_subln_g = 1.0 + nrm((N_DIFF_LAYERS, 2 * DIFF_HEAD_DIM), 0.02)
    a_w_out = nrm((N_DIFF_LAYERS, diff_w, d), beta * diff_w ** -0.5)

    b_in_w = MLA_Q_RANK + MLA_KV_RANK + MLA_ROPE_DIM + IDX_DIM + IDX_HEADS
    b_w_in = nrm((N_DSA_LAYERS, d, b_in_w), d ** -0.5)
    b_q_norm_g = 1.0 + nrm((N_DSA_LAYERS, MLA_Q_RANK), 0.02)
    b_kv_norm_g = 1.0 + nrm((N_DSA_LAYERS, MLA_KV_RANK), 0.02)
    b_w_uq = nrm((N_DSA_LAYERS, MLA_Q_RANK, MLA_HEADS * (MLA_ROPE_DIM + MLA_NOPE_DIM)), MLA_Q_RANK ** -0.5)
    b_w_qidx = nrm((N_DSA_LAYERS, MLA_Q_RANK, IDX_HEADS * IDX_DIM), MLA_Q_RANK ** -0.5)
    b_w_uk = nrm((N_DSA_LAYERS, MLA_HEADS, MLA_KV_RANK, MLA_NOPE_DIM), MLA_KV_RANK ** -0.5)
    b_w_uv = nrm((N_DSA_LAYERS, MLA_HEADS, MLA_KV_RANK, MLA_V_DIM), beta * MLA_KV_RANK ** -0.5)
    b_w_out = nrm((N_DSA_LAYERS, MLA_HEADS * MLA_V_DIM, d), beta * (MLA_HEADS * MLA_V_DIM) ** -0.5)

    mem_w = MEM_HEADS * MEM_HEAD_DIM
    mem_w_kv = jnp.concatenate([nrm((d, mem_w), d ** -0.5), nrm((d, mem_w), beta * d ** -0.5)], axis=-1)
    xa_w_q = nrm((DEPTH, d, mem_w), d ** -0.5)
    xa_w_out = nrm((DEPTH, mem_w, d), beta * mem_w ** -0.5)

    moe_w_group = nrm((DEPTH, d, N_GROUPS), d ** -0.5)
    moe_b_group = nrm((DEPTH, N_GROUPS), 0.01)
    moe_w_expert = nrm((DEPTH, d, N_EXPERTS), d ** -0.5)
    moe_b_expert = nrm((DEPTH, N_EXPERTS), 0.01)
    moe_w_gate = nrm((DEPTH, N_EXPERTS, d, EXPERT_FF), d ** -0.5)
    moe_w_up = nrm((DEPTH, N_EXPERTS, d, EXPERT_FF), beta * d ** -0.5)
    moe_w_down = nrm((DEPTH, N_EXPERTS, EXPERT_FF, d), beta * EXPERT_FF ** -0.5)

    ln_g = 1.0 + nrm((DEPTH, 3, d), 0.02)
    ln_b = nrm((DEPTH, 3, d), 0.02)

    return {"x": x, "mem": mem, "positions": positions,
            "a_w_in": a_w_in, "a_lambda": a_lambda, "a_subln_g": a_subln_g, "a_w_out": a_w_out,
            "b_w_in": b_w_in, "b_q_norm_g": b_q_norm_g, "b_kv_norm_g": b_kv_norm_g,
            "b_w_uq": b_w_uq, "b_w_qidx": b_w_qidx, "b_w_uk": b_w_uk, "b_w_uv": b_w_uv, "b_w_out": b_w_out,
            "mem_w_kv": mem_w_kv, "xa_w_q": xa_w_q, "xa_w_out": xa_w_out,
            "moe_w_group": moe_w_group, "moe_b_group": moe_b_group,
            "moe_w_expert": moe_w_expert, "moe_b_expert": moe_b_expert,
            "moe_w_gate": moe_w_gate, "moe_w_up": moe_w_up, "moe_w_down": moe_w_down,
            "ln_g": ln_g, "ln_b": ln_b}


def reference(x, mem, positions, a_w_in, a_lambda, a_subln_g, a_w_out,
              b_w_in, b_q_norm_g, b_kv_norm_g, b_w_uq, b_w_qidx, b_w_uk, b_w_uv, b_w_out,
              mem_w_kv, xa_w_q, xa_w_out,
              moe_w_group, moe_b_group, moe_w_expert, moe_b_expert, moe_w_gate, moe_w_up, moe_w_down,
              ln_g, ln_b):
    bsz = x.shape[0]
    n_mem = mem.shape[1]
    cos_d, sin_d = rope_tables(positions, DIFF_ROPE_DIM)
    cos_r, sin_r = rope_tables(positions, MLA_ROPE_DIM)
    cos_i, sin_i = rope_tables(positions, IDX_ROPE_DIM)
    mem_k, mem_v = jnp.split(mem @ mem_w_kv, 2, axis=-1)
    mem_k = mem_k.reshape(bsz, n_mem, MEM_HEADS, MEM_HEAD_DIM)
    mem_v = mem_v.reshape(bsz, n_mem, MEM_HEADS, MEM_HEAD_DIM)
    alpha = DEEPNORM_ALPHA
    h = x
    for i in range(DEPTH):
        j = i // N_MIXERS
        if i % N_MIXERS == 0:
            mix = differential_attention(h, cos_d, sin_d, a_w_in[j], a_lambda[j], a_subln_g[j], a_w_out[j],
                                         diff_lambda_init(i))
        else:
            mix = dsa_sparse_mla(h, cos_r, sin_r, cos_i, sin_i, b_w_in[j], b_q_norm_g[j], b_kv_norm_g[j],
                                 b_w_uq[j], b_w_qidx[j], b_w_uk[j], b_w_uv[j], b_w_out[j])
        h = layer_norm(alpha * h + mix, ln_g[i, 0], ln_b[i, 0])
        h = layer_norm(alpha * h + memory_cross_attention(h, mem_k, mem_v, xa_w_q[i], xa_w_out[i]),
                       ln_g[i, 1], ln_b[i, 1])
        h = layer_norm(alpha * h + hierarchical_moe(h, moe_w_group[i], moe_b_group[i], moe_w_expert[i],
                                                    moe_b_expert[i], moe_w_gate[i], moe_w_up[i], moe_w_down[i]),
                       ln_g[i, 2], ln_b[i, 2])
    return h
```

```python
import functools

import jax
import jax.numpy as jnp
from jax import lax
from jax.experimental import pallas as pl
from jax.experimental.pallas import tpu as pltpu

F32 = jnp.float32
BF16 = jnp.bfloat16
I32 = jnp.int32

CHUNK = 64
ROPE_THETA = 500000.0
LN_EPS = 1e-5
N_MIXERS = 2
DIFF_HEADS = 8
DIFF_HEAD_DIM = 64
DIFF_ROPE_DIM = DIFF_HEAD_DIM // 4
MLA_HEADS = 8
MLA_Q_RANK = 256
MLA_KV_RANK = 256
MLA_ROPE_DIM = 32
MLA_NOPE_DIM = 96
MLA_V_DIM = 128
IDX_HEADS = 16
IDX_DIM = 64
IDX_ROPE_DIM = IDX_DIM // 4
IDX_TOPK_MAX = 256
MEM_HEADS = 4
N_GROUPS = 4
EXPERTS_PER_GROUP = 4
N_EXPERTS = N_GROUPS * EXPERTS_PER_GROUP

LANES = 128
VMEM_LIMIT_BYTES = 56 * 1024 * 1024

ROW_BLOCK = 512
ATTN_BLOCK = 512
KEY_TILE = 512
MLA_FEAT = MLA_KV_RANK + LANES

NEG_INF = float("-inf")
NEG_INF_KEY = -2139095041

_NT = (((1,), (1,)), ((), ()))


def _params(n_axes):
    return pltpu.CompilerParams(dimension_semantics=("arbitrary",) * n_axes,
                                vmem_limit_bytes=VMEM_LIMIT_BYTES)


def _dot(a, b):
    return jnp.dot(a, b, preferred_element_type=F32)


def _dot_nt(a, b):
    return lax.dot_general(a, b, _NT, preferred_element_type=F32)


def _layer_norm(z, g, b):
    mu = jnp.mean(z, axis=-1, keepdims=True)
    zc = z - mu
    var = jnp.mean(zc * zc, axis=-1, keepdims=True)
    return zc * lax.rsqrt(var + LN_EPS) * g + b


def _rms_norm(x, g):
    return x * lax.rsqrt(jnp.mean(x * x, axis=-1, keepdims=True) + LN_EPS) * g


def _rope(y, c, s_up, s_dn, half):
    return (y * c + pltpu.roll(y, LANES - half, 1) * s_up + pltpu.roll(y, half, 1) * s_dn)


def _rope_lane_tables(positions, rot_dim, period, keep_rest):
    half = rot_dim // 2
    inv_freq = ROPE_THETA ** (-jnp.arange(0, rot_dim, 2, dtype=F32) / rot_dim)
    ang = positions.astype(F32)[..., None] * inv_freq
    cos, sin = jnp.cos(ang), jnp.sin(ang)
    lead = positions.shape
    rest = jnp.full(lead + (period - rot_dim,), 1.0 if keep_rest else 0.0, F32)
    zrest = jnp.zeros(lead + (period - rot_dim,), F32)
    zhalf = jnp.zeros(lead + (half,), F32)
    reps = LANES // period
    out = []
    for parts in ((cos, cos, rest), (-sin, zhalf, zrest), (zhalf, sin, zrest)):
        t = jnp.concatenate(parts, axis=-1)
        out.append(jnp.tile(t, (1,) * len(lead) + (reps,)).reshape(-1, LANES))
    return out


def _diff_qkv_kernel(h_ref, w_ref, c_ref, su_ref, sd_ref, o_ref, *, n_rope, n_q, q_scale, half):
    x = h_ref[...].astype(BF16)
    c, su, sd = c_ref[...], su_ref[...], sd_ref[...]
    n_slabs = o_ref.shape[1] // LANES
    for j2 in range(n_slabs // 2):
        y2 = _dot(x, w_ref[:, j2 * 2 * LANES:(j2 + 1) * 2 * LANES])
        for s in range(2):
            j = 2 * j2 + s
            y = y2[:, s * LANES:(s + 1) * LANES]
            if j < n_rope:
                y = _rope(y, c, su, sd, half)
            if j < n_q:
                y = y * q_scale
            o_ref[:, j * LANES:(j + 1) * LANES] = y.astype(BF16)


def _diff_qkv(h2, w_in, tabs):
    t, d = h2.shape
    n_out = w_in.shape[1]
    width = DIFF_HEADS * 2 * DIFF_HEAD_DIM
    kern = functools.partial(_diff_qkv_kernel, n_rope=2 * width // LANES, n_q=width // LANES,
                             q_scale=DIFF_HEAD_DIM ** -0.5, half=DIFF_ROPE_DIM // 2)
    row = lambda i: (i, 0)
    full = lambda i: (0, 0)
    return pl.pallas_call(
        kern, grid=(t // ROW_BLOCK,),
        in_specs=[pl.BlockSpec((ROW_BLOCK, d), row), pl.BlockSpec((d, n_out), full)]
        + [pl.BlockSpec((ROW_BLOCK, LANES), row)] * 3,
        out_specs=pl.BlockSpec((ROW_BLOCK, n_out), row),
        out_shape=jax.ShapeDtypeStruct((t, n_out), BF16),
        compiler_params=_params(1), name="diff_qkv",
    )(h2, w_in, *tabs)


def _diff_attn_kernel(lam_ref, g_ref, q_ref, k_ref, v_ref, o_ref, m_ref, l_ref, acc_ref, *,
                      blk, lambda_init):
    i = pl.program_id(2)
    lam = lam_ref[...]
    lam_full = (jnp.exp(jnp.sum(lam[0:1] * lam[1:2], axis=1, keepdims=True))
                - jnp.exp(jnp.sum(lam[2:3] * lam[3:4], axis=1, keepdims=True)) + lambda_init)
    q = q_ref[...]
    lane = lax.broadcasted_iota(I32, q.shape, 1)
    zero = jnp.zeros_like(q)
    q_maps = (jnp.where(lane < DIFF_HEAD_DIM, q, zero), jnp.where(lane >= DIFF_HEAD_DIM, q, zero))
    m_ref[...] = jnp.full(m_ref.shape, NEG_INF, F32)
    l_ref[...] = jnp.zeros(l_ref.shape, F32)
    acc_ref[...] = jnp.zeros(acc_ref.shape, F32)

    def step(j, masked):
        start = pl.multiple_of(j * blk, blk)
        k = k_ref[pl.ds(start, blk), :]
        v = v_ref[pl.ds(start, blk), :]
        for c in range(2):
            s = _dot_nt(q_maps[c], k)
            if masked:
                qc = lax.broadcasted_iota(I32, s.shape, 0) // CHUNK
                kc = lax.broadcasted_iota(I32, s.shape, 1) // CHUNK
                s = jnp.where(kc <= qc, s, NEG_INF)
            m_old = m_ref[c]
            m_new = jnp.maximum(m_old, jnp.max(s, axis=1, keepdims=True))
            p = jnp.exp(s - m_new)
            alpha = jnp.exp(m_old - m_new)
            l_ref[c] = alpha * l_ref[c] + jnp.sum(p, axis=1, keepdims=True)
            acc_ref[c] = alpha * acc_ref[c] + _dot(p.astype(BF16), v)
            m_ref[c] = m_new

    def full_step(j, carry):
        step(j, False)
        return carry

    lax.fori_loop(0, i, full_step, 0)
    step(i, True)

    o = acc_ref[0] * (1.0 / l_ref[0]) - lam_full * (acc_ref[1] * (1.0 / l_ref[1]))
    o = _rms_norm(o, g_ref[...]) * (1.0 - lambda_init)
    o_ref[...] = o.astype(BF16)


def _diff_attn(qkv, lam, subln_g, bsz, seq, lambda_init):
    blk = ATTN_BLOCK
    nq = seq // blk
    hd = 2 * DIFF_HEAD_DIM
    kern = functools.partial(_diff_attn_kernel, blk=blk, lambda_init=lambda_init)
    return pl.pallas_call(
        kern, grid=(bsz, DIFF_HEADS, nq),
        in_specs=[
            pl.BlockSpec(lam.shape, lambda b, h, i: (0, 0)),
            pl.BlockSpec((1, hd), lambda b, h, i: (0, 0)),
            pl.BlockSpec((blk, hd), lambda b, h, i: (b * nq + i, h)),
            pl.BlockSpec((seq, hd), lambda b, h, i: (b, DIFF_HEADS + h)),
            pl.BlockSpec((seq, hd), lambda b, h, i: (b, 2 * DIFF_HEADS + h)),
        ],
        out_specs=pl.BlockSpec((blk, hd), lambda b, h, i: (b * nq + i, h)),
        out_shape=jax.ShapeDtypeStruct((bsz * seq, DIFF_HEADS * hd), BF16),
        scratch_shapes=[pltpu.VMEM((2, blk, 1), F32), pltpu.VMEM((2, blk, 1), F32),
                        pltpu.VMEM((2, blk, hd), F32)],
        compiler_params=_params(3), name="diff_attn",
    )(lam, subln_g.reshape(1, hd), qkv, qkv, qkv)


def _proj_res_ln_kernel(a_ref, w_ref, h_ref, g_ref, b_ref, o_ref, *, alpha):
    z = alpha * h_ref[...] + _dot(a_ref[...], w_ref[...])
    o_ref[...] = _layer_norm(z, g_ref[...], b_ref[...])


def _proj_res_ln(a, w, h2, g, b, alpha):
    t, d = h2.shape
    k = a.shape[1]
    row = lambda i: (i, 0)
    full = lambda i: (0, 0)
    return pl.pallas_call(
        functools.partial(_proj_res_ln_kernel, alpha=alpha), grid=(t // ROW_BLOCK,),
        in_specs=[pl.BlockSpec((ROW_BLOCK, k), row), pl.BlockSpec((k, d), full),
                  pl.BlockSpec((ROW_BLOCK, d), row), pl.BlockSpec((1, d), full),
                  pl.BlockSpec((1, d), full)],
        out_specs=pl.BlockSpec((ROW_BLOCK, d), row),
        out_shape=jax.ShapeDtypeStruct((t, d), F32),
        compiler_params=_params(1), name="proj_res_ln",
    )(a, w, h2, g.reshape(1, d), b.reshape(1, d))


def _mem_kv_kernel(m_ref, w_ref, o_ref):
    o_ref[...] = _dot(m_ref[...].astype(BF16), w_ref[...]).astype(BF16)


def _mem_kv(mem2, w_kv):
    rows, d = mem2.shape
    n = w_kv.shape[1]
    blk = min(ROW_BLOCK, rows)
    return pl.pallas_call(
        _mem_kv_kernel, grid=(rows // blk,),
        in_specs=[pl.BlockSpec((blk, d), lambda i: (i, 0)), pl.BlockSpec((d, n), lambda i: (0, 0))],
        out_specs=pl.BlockSpec((blk, n), lambda i: (i, 0)),
        out_shape=jax.ShapeDtypeStruct((rows, n), BF16),
        compiler_params=_params(1), name="mem_kv",
    )(mem2, w_kv)


def _cross_attn_kernel(h_ref, wq_ref, kv_ref, wo_ref, g_ref, b_ref, o_ref, *, alpha, q_scale):
    h = h_ref[...]
    d = h.shape[1]
    hd = d // MEM_HEADS
    q = (_dot(h.astype(BF16), wq_ref[...]) * q_scale).astype(BF16)
    outs = []
    for hh in range(MEM_HEADS):
        s = _dot_nt(q[:, hh * hd:(hh + 1) * hd], kv_ref[:, hh * hd:(hh + 1) * hd])
        p = jnp.exp(s - jnp.max(s, axis=1, keepdims=True))
        p = p * (1.0 / jnp.sum(p, axis=1, keepdims=True))
        outs.append(_dot(p.astype(BF16), kv_ref[:, d + hh * hd:d + (hh + 1) * hd]).astype(BF16))
    o = jnp.concatenate(outs, axis=1)
    z = alpha * h + _dot(o, wo_ref[...])
    o_ref[...] = _layer_norm(z, g_ref[...], b_ref[...])


def _cross_attn(h2, w_q, memkv, w_out, g, b, alpha, seq, n_mem):
    t, d = h2.shape
    per_batch = seq // ROW_BLOCK
    row = lambda i: (i, 0)
    full = lambda i: (0, 0)
    kern = functools.partial(_cross_attn_kernel, alpha=alpha, q_scale=(d // MEM_HEADS) ** -0.5)
    return pl.pallas_call(
        kern, grid=(t // ROW_BLOCK,),
        in_specs=[pl.BlockSpec((ROW_BLOCK, d), row), pl.BlockSpec((d, d), full),
                  pl.BlockSpec((n_mem, 2 * d), lambda i: (i // per_batch, 0)),
                  pl.BlockSpec((d, d), full), pl.BlockSpec((1, d), full), pl.BlockSpec((1, d), full)],
        out_specs=pl.BlockSpec((ROW_BLOCK, d), row),
        out_shape=jax.ShapeDtypeStruct((t, d), F32),
        compiler_params=_params(1), name="cross_attn",
    )(h2, w_q, memkv, w_out, g.reshape(1, d), b.reshape(1, d))


def _route(h, wr_hi, wr_lo, br):
    h_hi = h.astype(BF16)
    h_lo = (h - h_hi.astype(F32)).astype(BF16)
    logits = _dot(h_hi, wr_hi) + _dot(h_hi, wr_lo) + _dot(h_lo, wr_hi) + br
    lane = lax.broadcasted_iota(I32, logits.shape, 1).astype(F32)
    gl = jnp.where(lane < N_GROUPS, logits, NEG_INF)
    gmax = jnp.max(gl, axis=1, keepdims=True)
    g_sel = jnp.min(jnp.where(gl == gmax, lane, float(LANES)), axis=1, keepdims=True)
    g_gate = 1.0 / jnp.sum(jnp.exp(gl - gmax), axis=1, keepdims=True)
    first = N_GROUPS + g_sel * EXPERTS_PER_GROUP
    el = jnp.where((lane >= first) & (lane < first + EXPERTS_PER_GROUP), logits, NEG_INF)
    v1 = jnp.max(el, axis=1, keepdims=True)
    i1 = jnp.min(jnp.where(el == v1, lane, float(LANES)), axis=1, keepdims=True)
    el2 = jnp.where(lane == i1, NEG_INF, el)
    v2 = jnp.max(el2, axis=1, keepdims=True)
    i2 = jnp.min(jnp.where(el2 == v2, lane, float(LANES)), axis=1, keepdims=True)
    r = jnp.exp(v2 - v1)
    inv = g_gate / (1.0 + r)
    return jnp.where(lane == i1, inv, 0.0) + jnp.where(lane == i2, inv * r, 0.0)


def _moe_kernel(h_ref, wrh_ref, wrl_ref, br_ref, wg_ref, wu_ref, wd_ref, g_ref, b_ref, o_ref,
                xb_ref, comb_ref, acc_ref, *, alpha):
    e = pl.program_id(1)

    @pl.when(e == 0)
    def _():
        h = h_ref[...]
        comb_ref[...] = _route(h, wrh_ref[...], wrl_ref[...], br_ref[...])
        xb_ref[...] = h.astype(BF16)
        acc_ref[...] = jnp.zeros(acc_ref.shape, F32)

    x = xb_ref[...]
    comb = comb_ref[...]
    lane = lax.broadcasted_iota(I32, comb.shape, 1)
    c = jnp.sum(jnp.where(lane == N_GROUPS + e, comb, 0.0), axis=1, keepdims=True)
    a = _dot(x, wg_ref[0])
    u = _dot(x, wu_ref[0])
    hid = a * (1.0 / (1.0 + jnp.exp(-a))) * u
    acc_ref[...] += _dot((c * hid).astype(BF16), wd_ref[0])

    @pl.when(e == pl.num_programs(1) - 1)
    def _():
        z = alpha * h_ref[...] + acc_ref[...]
        o_ref[...] = _layer_norm(z, g_ref[...], b_ref[...])


def _moe(h2, wr_hi, wr_lo, br, w_gate, w_up, w_down, g, b, alpha):
    t, d = h2.shape
    n_e, _, ff = w_gate.shape
    row = lambda i, e: (i, 0)
    full = lambda i, e: (0, 0)
    return pl.pallas_call(
        functools.partial(_moe_kernel, alpha=alpha), grid=(t // ROW_BLOCK, n_e),
        in_specs=[pl.BlockSpec((ROW_BLOCK, d), row), pl.BlockSpec((d, LANES), full),
                  pl.BlockSpec((d, LANES), full), pl.BlockSpec((1, LANES), full),
                  pl.BlockSpec((1, d, ff), lambda i, e: (e, 0, 0)),
                  pl.BlockSpec((1, d, ff), lambda i, e: (e, 0, 0)),
                  pl.BlockSpec((1, ff, d), lambda i, e: (e, 0, 0)),
                  pl.BlockSpec((1, d), full), pl.BlockSpec((1, d), full)],
        out_specs=pl.BlockSpec((ROW_BLOCK, d), row),
        out_shape=jax.ShapeDtypeStruct((t, d), F32),
        scratch_shapes=[pltpu.VMEM((ROW_BLOCK, d), BF16), pltpu.VMEM((ROW_BLOCK, LANES), F32),
                        pltpu.VMEM((ROW_BLOCK, d), F32)],
        compiler_params=_params(2), name="moe",
    )(h2, wr_hi, wr_lo, br, w_gate, w_up, w_down, g.reshape(1, d), b.reshape(1, d))


_COL_CQ, _COL_CKV, _COL_KROPE, _COL_KIDX, _COL_WIDX, _COL_END = 0, 256, 512, 640, 768, 896


def _dsa_proj_kernel(h_ref, win_ref, gq_ref, gkv_ref, wuq_ref, wqi_ref, wuk_ref,
                     cr_ref, sur_ref, sdr_ref, ci_ref, sui_ref, sdi_ref,
                     qf_ref, kv_ref, qi_ref, ki_ref, wi_ref, *, q_scale, w_scale):
    y = _dot(h_ref[...].astype(BF16), win_ref[...])
    cr, sur, sdr = cr_ref[...], sur_ref[...], sdr_ref[...]
    ci, sui, sdi = ci_ref[...], sui_ref[...], sdi_ref[...]
    half_r, half_i = MLA_ROPE_DIM // 2, IDX_ROPE_DIM // 2
    c_q = _rms_norm(y[:, _COL_CQ:_COL_CKV], gq_ref[...]).astype(BF16)
    c_kv = _rms_norm(y[:, _COL_CKV:_COL_KROPE], gkv_ref[...])
    kv_ref[:, 0:MLA_KV_RANK] = c_kv.astype(BF16)
    kv_ref[:, MLA_KV_RANK:MLA_FEAT] = _rope(y[:, _COL_KROPE:_COL_KIDX], cr, sur, sdr, half_r).astype(BF16)
    ki_ref[...] = _rope(y[:, _COL_KIDX:_COL_WIDX], ci, sui, sdi, half_i).astype(BF16)
    wi_ref[...] = y[:, _COL_WIDX:_COL_END] * w_scale
    q = _dot(c_q, wuq_ref[...])
    qb = q.astype(BF16)
    hd = MLA_ROPE_DIM + MLA_NOPE_DIM
    for hh in range(MLA_HEADS):
        lat = _dot(qb[:, hh * hd:(hh + 1) * hd], wuk_ref[hh])
        qf_ref[hh, :, 0:MLA_KV_RANK] = (lat * q_scale).astype(BF16)
        roped = _rope(q[:, hh * hd:(hh + 1) * hd], cr, sur, sdr, half_r)
        qf_ref[hh, :, MLA_KV_RANK:MLA_FEAT] = (roped * q_scale).astype(BF16)
    qi = _dot(c_q, wqi_ref[...])
    for p in range(IDX_HEADS // 2):
        qi_ref[p] = _rope(qi[:, p * LANES:(p + 1) * LANES], ci, sui, sdi, half_i).astype(BF16)


def _dsa_proj(h2, w_in_p, gq, gkv, w_uq, w_qidx, w_ukt, tabs_r, tabs_i):
    t, d = h2.shape
    row = lambda i: (i, 0)
    full = lambda i: (0, 0)
    full3 = lambda i: (0, 0, 0)
    head_row = lambda i: (0, i, 0)
    n_pairs = IDX_HEADS // 2
    kern = functools.partial(_dsa_proj_kernel, q_scale=(MLA_ROPE_DIM + MLA_NOPE_DIM) ** -0.5,
                             w_scale=(IDX_HEADS * IDX_DIM) ** -0.5)
    tab = pl.BlockSpec((ROW_BLOCK, LANES), row)
    return pl.pallas_call(
        kern, grid=(t // ROW_BLOCK,),
        in_specs=[pl.BlockSpec((ROW_BLOCK, d), row), pl.BlockSpec(w_in_p.shape, full),
                  pl.BlockSpec((1, MLA_Q_RANK), full), pl.BlockSpec((1, MLA_KV_RANK), full),
                  pl.BlockSpec(w_uq.shape, full), pl.BlockSpec(w_qidx.shape, full),
                  pl.BlockSpec(w_ukt.shape, full3)] + [tab] * 6,
        out_specs=[pl.BlockSpec((MLA_HEADS, ROW_BLOCK, MLA_FEAT), head_row),
                   pl.BlockSpec((ROW_BLOCK, MLA_FEAT), row),
                   pl.BlockSpec((n_pairs, ROW_BLOCK, LANES), head_row),
                   pl.BlockSpec((ROW_BLOCK, LANES), row),
                   pl.BlockSpec((ROW_BLOCK, LANES), row)],
        out_shape=[jax.ShapeDtypeStruct((MLA_HEADS, t, MLA_FEAT), BF16),
                   jax.ShapeDtypeStruct((t, MLA_FEAT), BF16),
                   jax.ShapeDtypeStruct((n_pairs, t, LANES), BF16),
                   jax.ShapeDtypeStruct((t, LANES), BF16),
                   jax.ShapeDtypeStruct((t, LANES), F32)],
        compiler_params=_params(1), name="dsa_proj",
    )(h2, w_in_p, gq.reshape(1, -1), gkv.reshape(1, -1), w_uq, w_qidx, w_ukt, *tabs_r, *tabs_i)


def _sortable_key(x):
    bits = lax.bitcast_convert_type(x, I32)
    return bits ^ ((bits >> 31) & 0x7FFFFFFF)


def _dsa_attn_kernel(qi_ref, wi_ref, qf_ref, ki_ref, kv_ref, wuv_ref, o_ref,
                     key_ref, thr_ref, m_ref, l_ref, acc_ref, *, kt, top_k):
    c = pl.program_id(1)
    n_allowed = (c + 1) * CHUNK
    n_tiles = (n_allowed + kt - 1) // kt
    n_pairs = IDX_HEADS // 2
    slabs = kt // LANES

    qi = qi_ref[...].reshape(n_pairs * CHUNK, LANES)
    lane = lax.broadcasted_iota(I32, qi.shape, 1)
    zero = jnp.zeros_like(qi)
    qi_rows = jnp.concatenate([jnp.where(lane < IDX_DIM, qi, zero),
                               jnp.where(lane >= IDX_DIM, qi, zero)], axis=0)
    w = wi_ref[...]
    head_of_slab = [2 * r for r in range(n_pairs)] + [2 * r + 1 for r in range(n_pairs)]
    w_rows = [jnp.broadcast_to(w[:, hd:hd + 1], (CHUNK, LANES)) for hd in head_of_slab]

    def score_tile(t, carry):
        lo, hi = carry
        start = pl.multiple_of(t * kt, kt)
        logits = _dot_nt(qi_rows, ki_ref[pl.ds(start, kt), :])
        cols = []
        for s in range(slabs):
            sc = jnp.zeros((CHUNK, LANES), F32)
            for r in range(IDX_HEADS):
                sc = sc + jnp.maximum(logits[r * CHUNK:(r + 1) * CHUNK, s * LANES:(s + 1) * LANES], 0.0) * w_rows[r]
            cols.append(sc)
        sc = jnp.concatenate(cols, axis=1)
        col = t * kt + lax.broadcasted_iota(I32, sc.shape, 1)
        valid = col < n_allowed
        key_ref[t] = _sortable_key(jnp.where(valid, sc, NEG_INF))
        lo = jnp.minimum(lo, jnp.min(jnp.where(valid, sc, float("inf")), axis=1, keepdims=True))
        hi = jnp.maximum(hi, jnp.max(jnp.where(valid, sc, NEG_INF), axis=1, keepdims=True))
        return lo, hi

    lo_f, hi_f = lax.fori_loop(0, n_tiles, score_tile,
                               (jnp.full((CHUNK, 1), float("inf"), F32), jnp.full((CHUNK, 1), NEG_INF, F32)))

    thr_ref[...] = jnp.full(thr_ref.shape, NEG_INF_KEY + 1, I32)

    @pl.when(n_allowed > top_k)
    def _():
        def count_ge(mid):
            def body(t, cnt):
                k = key_ref[t]
                for s in range(slabs):
                    cnt = cnt + jnp.where(k[:, s * LANES:(s + 1) * LANES] >= mid, 1, 0)
                return cnt
            cnt = lax.fori_loop(0, n_tiles, body, jnp.zeros((CHUNK, LANES), I32))
            return jnp.sum(cnt.astype(F32), axis=1, keepdims=True)

        def bisect(_, st):
            lo, hi = st
            mid = (lo >> 1) + (hi >> 1) + ((lo | hi) & 1)
            cnt = count_ge(mid)
            ge = cnt >= float(top_k)
            lo2 = jnp.where(ge, mid, lo)
            hi2 = jnp.where(cnt == float(top_k), mid, jnp.where(ge, hi, mid - 1))
            return lo2, hi2

        lo, _ = lax.fori_loop(0, 32, bisect, (_sortable_key(lo_f), _sortable_key(hi_f)))
        thr_ref[...] = lo

    thr = thr_ref[...]
    qf = qf_ref[...].reshape(MLA_HEADS * CHUNK, MLA_FEAT)
    m_ref[...] = jnp.full(m_ref.shape, NEG_INF, F32)
    l_ref[...] = jnp.zeros(l_ref.shape, F32)
    acc_ref[...] = jnp.zeros(acc_ref.shape, F32)

    def attn_tile(t, carry):
        start = pl.multiple_of(t * kt, kt)
        kvt = kv_ref[pl.ds(start, kt), :]
        s = _dot_nt(qf, kvt)
        bias = jnp.where(key_ref[t] >= thr, 0.0, NEG_INF)
        s = jnp.concatenate([s[hh * CHUNK:(hh + 1) * CHUNK] + bias for hh in range(MLA_HEADS)], axis=0)
        m_old = m_ref[...]
        m_new = jnp.maximum(m_old, jnp.max(s, axis=1, keepdims=True))
        m_safe = jnp.where(m_new == NEG_INF, 0.0, m_new)
        p = jnp.exp(s - m_safe)
        alpha = jnp.exp(m_old - m_safe)
        l_ref[...] = alpha * l_ref[...] + jnp.sum(p, axis=1, keepdims=True)
        acc_ref[...] = alpha * acc_ref[...] + _dot(p.astype(BF16), kvt[:, 0:MLA_KV_RANK])
        m_ref[...] = m_new
        return carry

    lax.fori_loop(0, n_tiles, attn_tile, 0)

    o_lat = (acc_ref[...] * (1.0 / l_ref[...])).astype(BF16)
    for hh in range(MLA_HEADS):
        o_ref[:, hh * MLA_V_DIM:(hh + 1) * MLA_V_DIM] = _dot(
            o_lat[hh * CHUNK:(hh + 1) * CHUNK], wuv_ref[hh]).astype(BF16)


def _dsa_attn(qf, kv, qi, ki, wi, w_uv, bsz, seq):
    kt = min(KEY_TILE, seq)
    nc = seq // CHUNK
    n_pairs = IDX_HEADS // 2
    top_k = min(IDX_TOPK_MAX, seq // 4)
    kern = functools.partial(_dsa_attn_kernel, kt=kt, top_k=top_k)
    chunk3 = lambda b, c: (0, b * nc + c, 0)
    chunk2 = lambda b, c: (b * nc + c, 0)
    batch2 = lambda b, c: (b, 0)
    rows = MLA_HEADS * CHUNK
    return pl.pallas_call(
        kern, grid=(bsz, nc),
        in_specs=[pl.BlockSpec((n_pairs, CHUNK, LANES), chunk3),
                  pl.BlockSpec((CHUNK, LANES), chunk2),
                  pl.BlockSpec((MLA_HEADS, CHUNK, MLA_FEAT), chunk3),
                  pl.BlockSpec((seq, LANES), batch2),
                  pl.BlockSpec((seq, MLA_FEAT), batch2),
                  pl.BlockSpec(w_uv.shape, lambda b, c: (0, 0, 0))],
        out_specs=pl.BlockSpec((CHUNK, MLA_HEADS * MLA_V_DIM), chunk2),
        out_shape=jax.ShapeDtypeStruct((bsz * seq, MLA_HEADS * MLA_V_DIM), BF16),
        scratch_shapes=[pltpu.VMEM((seq // kt, CHUNK, kt), I32), pltpu.VMEM((CHUNK, 1), I32),
                        pltpu.VMEM((rows, 1), F32), pltpu.VMEM((rows, 1), F32),
                        pltpu.VMEM((rows, MLA_KV_RANK), F32)],
        compiler_params=_params(2), name="dsa_attn",
    )(qi, wi, qf, ki, kv, w_uv)


def _diff_lambda_init(layer_idx):
    import math
    return 0.8 - 0.6 * math.exp(-0.3 * layer_idx)


def _pad_dsa_w_in(w_in):
    d = w_in.shape[0]
    o_cq, o_ckv = 0, MLA_Q_RANK
    o_kr = o_ckv + MLA_KV_RANK
    o_ki = o_kr + MLA_ROPE_DIM
    o_wi = o_ki + IDX_DIM
    out = jnp.zeros((d, _COL_END), w_in.dtype)
    out = out.at[:, _COL_CQ:_COL_CQ + MLA_Q_RANK].set(w_in[:, o_cq:o_ckv])
    out = out.at[:, _COL_CKV:_COL_CKV + MLA_KV_RANK].set(w_in[:, o_ckv:o_kr])
    out = out.at[:, _COL_KROPE:_COL_KROPE + MLA_ROPE_DIM].set(w_in[:, o_kr:o_ki])
    out = out.at[:, _COL_KIDX:_COL_KIDX + IDX_DIM].set(w_in[:, o_ki:o_wi])
    out = out.at[:, _COL_KIDX + IDX_DIM:_COL_KIDX + 2 * IDX_DIM].set(w_in[:, o_ki:o_wi])
    out = out.at[:, _COL_WIDX:_COL_WIDX + IDX_HEADS].set(w_in[:, o_wi:o_wi + IDX_HEADS])
    return out.astype(BF16)


def _router_weights(w_group, b_group, w_expert, b_expert):
    d = w_group.shape[0]
    w = jnp.zeros((d, LANES), F32)
    w = w.at[:, 0:N_GROUPS].set(w_group).at[:, N_GROUPS:N_GROUPS + N_EXPERTS].set(w_expert)
    br = jnp.zeros((1, LANES), F32)
    br = br.at[0, 0:N_GROUPS].set(b_group).at[0, N_GROUPS:N_GROUPS + N_EXPERTS].set(b_expert)
    w_hi = w.astype(BF16)
    w_lo = (w - w_hi.astype(F32)).astype(BF16)
    return w_hi, w_lo, br


def kernel(x, mem, positions, a_w_in, a_lambda, a_subln_g, a_w_out, b_w_in, b_q_norm_g, b_kv_norm_g,
           b_w_uq, b_w_qidx, b_w_uk, b_w_uv, b_w_out, mem_w_kv, xa_w_q, xa_w_out,
           moe_w_group, moe_b_group, moe_w_expert, moe_b_expert, moe_w_gate, moe_w_up, moe_w_down,
           ln_g, ln_b):
    bsz, seq, d = x.shape
    n_mem = mem.shape[1]
    depth = ln_g.shape[0]
    alpha = (2.0 * depth) ** 0.25
    assert seq % ROW_BLOCK == 0 and seq % ATTN_BLOCK == 0 and seq % KEY_TILE == 0

    tabs_d = _rope_lane_tables(positions, DIFF_ROPE_DIM, DIFF_HEAD_DIM, True)
    tabs_r = _rope_lane_tables(positions, MLA_ROPE_DIM, LANES, False)
    tabs_i = _rope_lane_tables(positions, IDX_ROPE_DIM, IDX_DIM, True)

    memkv = _mem_kv(mem.reshape(bsz * n_mem, d), mem_w_kv.astype(BF16))

    h = x.reshape(bsz * seq, d)
    for i in range(depth):
        j = i // N_MIXERS
        if i % N_MIXERS == 0:
            qkv = _diff_qkv(h, a_w_in[j].astype(BF16), tabs_d)
            mix = _diff_attn(qkv, a_lambda[j], a_subln_g[j], bsz, seq, _diff_lambda_init(i))
            w_mix_out = a_w_out[j]
        else:
            w_ukt = jnp.pad(jnp.swapaxes(b_w_uk[j], 1, 2), ((0, 0), (MLA_ROPE_DIM, 0), (0, 0))).astype(BF16)
            qf, kv, qi, ki, wi = _dsa_proj(h, _pad_dsa_w_in(b_w_in[j]), b_q_norm_g[j], b_kv_norm_g[j],
                                           b_w_uq[j].astype(BF16), b_w_qidx[j].astype(BF16), w_ukt,
                                           tabs_r, tabs_i)
            mix = _dsa_attn(qf, kv, qi, ki, wi, b_w_uv[j].astype(BF16), bsz, seq)
            w_mix_out = b_w_out[j]
        h = _proj_res_ln(mix, w_mix_out.astype(BF16), h, ln_g[i, 0], ln_b[i, 0], alpha)
        h = _cross_attn(h, xa_w_q[i].astype(BF16), memkv, xa_w_out[i].astype(BF16),
                        ln_g[i, 1], ln_b[i, 1], alpha, seq, n_mem)
        wr_hi, wr_lo, br = _router_weights(moe_w_group[i], moe_b_group[i], moe_w_expert[i], moe_b_expert[i])
        h = _moe(h, wr_hi, wr_lo, br, moe_w_gate[i].astype(BF16), moe_w_up[i].astype(BF16),
                 moe_w_down[i].astype(BF16), ln_g[i, 2], ln_b[i, 2], alpha)
    return h.reshape(bsz, seq, d)
```

```python
import functools

import jax
import jax.numpy as jnp
from jax import lax
from jax.experimental import pallas as pl
from jax.experimental.pallas import tpu as pltpu

F32 = jnp.float32
BF16 = jnp.bfloat16
I32 = jnp.int32

CHUNK = 64
ROPE_THETA = 500000.0
LN_EPS = 1e-5
N_MIXERS = 2
DIFF_HEADS = 8
DIFF_HEAD_DIM = 64
DIFF_ROPE_DIM = DIFF_HEAD_DIM // 4
MLA_HEADS = 8
MLA_Q_RANK = 256
MLA_KV_RANK = 256
MLA_ROPE_DIM = 32
MLA_NOPE_DIM = 96
MLA_V_DIM = 128
IDX_HEADS = 16
IDX_DIM = 64
IDX_ROPE_DIM = IDX_DIM // 4
IDX_TOPK_MAX = 256
MEM_HEADS = 4
N_GROUPS = 4
EXPERTS_PER_GROUP = 4
N_EXPERTS = N_GROUPS * EXPERTS_PER_GROUP

LANES = 128
VMEM_LIMIT_BYTES = 56 * 1024 * 1024

ROW_BLOCK = 512
ATTN_BLOCK = 512
KEY_TILE = 512
MLA_FEAT = MLA_KV_RANK + LANES

NEG_INF = float("-inf")
NEG_INF_KEY = -2139095041

_NT = (((1,), (1,)), ((), ()))


def _params(n_axes):
    return pltpu.CompilerParams(dimension_semantics=("arbitrary",) * n_axes,
                                vmem_limit_bytes=VMEM_LIMIT_BYTES)


def _dot(a, b):
    return jnp.dot(a, b, preferred_element_type=F32)


def _dot_nt(a, b):
    return lax.dot_general(a, b, _NT, preferred_element_type=F32)


def _layer_norm(z, g, b):
    mu = jnp.mean(z, axis=-1, keepdims=True)
    zc = z - mu
    var = jnp.mean(zc * zc, axis=-1, keepdims=True)
    return zc * lax.rsqrt(var + LN_EPS) * g + b


def _rms_norm(x, g):
    return x * lax.rsqrt(jnp.mean(x * x, axis=-1, keepdims=True) + LN_EPS) * g


def _rope(y, c, s_up, s_dn, half):
    return (y * c + pltpu.roll(y, LANES - half, 1) * s_up + pltpu.roll(y, half, 1) * s_dn)


def _rope_lane_tables(positions, rot_dim, period, keep_rest):
    half = rot_dim // 2
    inv_freq = ROPE_THETA ** (-jnp.arange(0, rot_dim, 2, dtype=F32) / rot_dim)
    ang = positions.astype(F32)[..., None] * inv_freq
    cos, sin = jnp.cos(ang), jnp.sin(ang)
    lead = positions.shape
    rest = jnp.full(lead + (period - rot_dim,), 1.0 if keep_rest else 0.0, F32)
    zrest = jnp.zeros(lead + (period - rot_dim,), F32)
    zhalf = jnp.zeros(lead + (half,), F32)
    reps = LANES // period
    out = []
    for parts in ((cos, cos, rest), (-sin, zhalf, zrest), (zhalf, sin, zrest)):
        t = jnp.concatenate(parts, axis=-1)
        out.append(jnp.tile(t, (1,) * len(lead) + (reps,)).reshape(-1, LANES))
    return out


LOG2E = 1.4426950408889634


def _diff_qkv_kernel(h_ref, w_ref, c_ref, su_ref, sd_ref, qt_ref, k_ref, vt_ref, *, q_scale, half):
    x = h_ref[...].astype(BF16)
    c, su, sd = c_ref[...], su_ref[...], sd_ref[...]
    n_heads = qt_ref.shape[0]
    for j2 in range(3 * n_heads // 2):
        y2 = _dot(x, w_ref[:, j2 * 2 * LANES:(j2 + 1) * 2 * LANES])
        for s in range(2):
            j = 2 * j2 + s
            y = y2[:, s * LANES:(s + 1) * LANES]
            if j < n_heads:
                qt_ref[j, 0] = (_rope(y, c, su, sd, half) * q_scale).T.astype(BF16)
            elif j < 2 * n_heads:
                k_ref[:, (j - n_heads) * LANES:(j - n_heads + 1) * LANES] = _rope(y, c, su, sd, half).astype(BF16)
            else:
                vt_ref[j - 2 * n_heads, 0] = y.T.astype(BF16)


def _diff_qkv(h2, w_in, tabs):
    t, d = h2.shape
    hd = 2 * DIFF_HEAD_DIM
    nb = t // ROW_BLOCK
    kern = functools.partial(_diff_qkv_kernel, q_scale=DIFF_HEAD_DIM ** -0.5 * LOG2E, half=DIFF_ROPE_DIM // 2)
    row = lambda i: (i, 0)
    full = lambda i: (0, 0)
    t_shape = jax.ShapeDtypeStruct((DIFF_HEADS, nb, hd, ROW_BLOCK), BF16)
    t_spec = pl.BlockSpec((DIFF_HEADS, 1, hd, ROW_BLOCK), lambda i: (0, i, 0, 0))
    return pl.pallas_call(
        kern, grid=(nb,),
        in_specs=[pl.BlockSpec((ROW_BLOCK, d), row), pl.BlockSpec(w_in.shape, full)]
        + [pl.BlockSpec((ROW_BLOCK, LANES), row)] * 3,
        out_specs=[t_spec, pl.BlockSpec((ROW_BLOCK, DIFF_HEADS * hd), row), t_spec],
        out_shape=[t_shape, jax.ShapeDtypeStruct((t, DIFF_HEADS * hd), BF16), t_shape],
        compiler_params=_params(1), name="diff_qkv",
    )(h2, w_in, *tabs)


def _diff_attn_kernel(lam_ref, g_ref, qt_ref, k_ref, vt_ref, o_ref, m_ref, l_ref, acc_ref, *,
                      blk, lambda_init):
    i = pl.program_id(2)
    lam = lam_ref[...]
    lam_full = (jnp.exp(jnp.sum(lam[0:1] * lam[1:2], axis=1, keepdims=True))
                - jnp.exp(jnp.sum(lam[2:3] * lam[3:4], axis=1, keepdims=True)) + lambda_init)
    qt = qt_ref[0, 0]
    feat = lax.broadcasted_iota(I32, qt.shape, 0)
    zero = jnp.zeros_like(qt)
    q_maps = (jnp.where(feat < DIFF_HEAD_DIM, qt, zero), jnp.where(feat >= DIFF_HEAD_DIM, qt, zero))
    m_ref[...] = jnp.full(m_ref.shape, NEG_INF, F32)
    l_ref[...] = jnp.zeros(l_ref.shape, F32)
    acc_ref[...] = jnp.zeros(acc_ref.shape, F32)

    def step(j, masked):
        k = k_ref[pl.ds(pl.multiple_of(j * blk, blk), blk), :]
        vt = vt_ref[0, j]
        for c in range(2):
            s = _dot(k, q_maps[c])
            if masked:
                kc = lax.broadcasted_iota(I32, s.shape, 0) // CHUNK
                qc = lax.broadcasted_iota(I32, s.shape, 1) // CHUNK
                s = jnp.where(kc <= qc, s, NEG_INF)
            m_old = m_ref[c]
            m_new = jnp.maximum(m_old, jnp.max(s, axis=0, keepdims=True))
            p = jnp.exp2(s - m_new)
            alpha = jnp.exp2(m_old - m_new)
            l_ref[c] = alpha * l_ref[c] + jnp.sum(p, axis=0, keepdims=True)
            acc_ref[c] = alpha * acc_ref[c] + _dot(vt, p.astype(BF16))
            m_ref[c] = m_new

    def full_step(j, carry):
        step(j, False)
        return carry

    lax.fori_loop(0, i, full_step, 0)
    step(i, True)

    ot = acc_ref[0] * (1.0 / l_ref[0]) - lam_full * (acc_ref[1] * (1.0 / l_ref[1]))
    ot = ot * lax.rsqrt(jnp.mean(ot * ot, axis=0, keepdims=True) + LN_EPS) * (1.0 - lambda_init)
    g = g_ref[...]
    for s in range(blk // LANES):
        o_ref[s * LANES:(s + 1) * LANES, :] = (ot[:, s * LANES:(s + 1) * LANES] * g).T.astype(BF16)


def _diff_attn(qt, k, vt, lam, subln_g, bsz, seq, lambda_init):
    blk = ATTN_BLOCK
    nq = seq // blk
    hd = 2 * DIFF_HEAD_DIM
    kern = functools.partial(_diff_attn_kernel, blk=blk, lambda_init=lambda_init)
    g = jnp.broadcast_to(subln_g.astype(F32)[:, None], (hd, LANES))
    return pl.pallas_call(
        kern, grid=(bsz, DIFF_HEADS, nq),
        in_specs=[
            pl.BlockSpec(lam.shape, lambda b, h, i: (0, 0)),
            pl.BlockSpec((hd, LANES), lambda b, h, i: (0, 0)),
            pl.BlockSpec((1, 1, hd, blk), lambda b, h, i: (h, b * nq + i, 0, 0)),
            pl.BlockSpec((seq, hd), lambda b, h, i: (b, h)),
            pl.BlockSpec((1, nq, hd, blk), lambda b, h, i: (h, b, 0, 0)),
        ],
        out_specs=pl.BlockSpec((blk, hd), lambda b, h, i: (b * nq + i, h)),
        out_shape=jax.ShapeDtypeStruct((bsz * seq, DIFF_HEADS * hd), BF16),
        scratch_shapes=[pltpu.VMEM((2, 1, blk), F32), pltpu.VMEM((2, 1, blk), F32),
                        pltpu.VMEM((2, hd, blk), F32)],
        compiler_params=_params(3), name="diff_attn",
    )(lam, g, qt, k, vt)


def _proj_res_ln_kernel(a_ref, w_ref, h_ref, g_ref, b_ref, o_ref, *, alpha):
    z = alpha * h_ref[...] + _dot(a_ref[...], w_ref[...])
    o_ref[...] = _layer_norm(z, g_ref[...], b_ref[...])


def _proj_res_ln(a, w, h2, g, b, alpha):
    t, d = h2.shape
    k = a.shape[1]
    row = lambda i: (i, 0)
    full = lambda i: (0, 0)
    return pl.pallas_call(
        functools.partial(_proj_res_ln_kernel, alpha=alpha), grid=(t // ROW_BLOCK,),
        in_specs=[pl.BlockSpec((ROW_BLOCK, k), row), pl.BlockSpec((k, d), full),
                  pl.BlockSpec((ROW_BLOCK, d), row), pl.BlockSpec((1, d), full),
                  pl.BlockSpec((1, d), full)],
        out_specs=pl.BlockSpec((ROW_BLOCK, d), row),
        out_shape=jax.ShapeDtypeStruct((t, d), F32),
        compiler_params=_params(1), name="proj_res_ln",
    )(a, w, h2, g.reshape(1, d), b.reshape(1, d))


def _mem_kv_kernel(m_ref, w_ref, o_ref):
    o_ref[...] = _dot(m_ref[...].astype(BF16), w_ref[...]).astype(BF16)


def _mem_kv(mem2, w_kv):
    rows, d = mem2.shape
    n = w_kv.shape[1]
    blk = min(ROW_BLOCK, rows)
    return pl.pallas_call(
        _mem_kv_kernel, grid=(rows // blk,),
        in_specs=[pl.BlockSpec((blk, d), lambda i: (i, 0)), pl.BlockSpec((d, n), lambda i: (0, 0))],
        out_specs=pl.BlockSpec((blk, n), lambda i: (i, 0)),
        out_shape=jax.ShapeDtypeStruct((rows, n), BF16),
        compiler_params=_params(1), name="mem_kv",
    )(mem2, w_kv)


def _cross_attn_kernel(h_ref, wq_ref, kv_ref, wo_ref, g_ref, b_ref, o_ref, *, alpha, q_scale):
    h = h_ref[...]
    d = h.shape[1]
    hd = d // MEM_HEADS
    q = (_dot(h.astype(BF16), wq_ref[...]) * q_scale).astype(BF16)
    outs = []
    for hh in range(MEM_HEADS):
        s = _dot_nt(q[:, hh * hd:(hh + 1) * hd], kv_ref[:, hh * hd:(hh + 1) * hd])
        p = jnp.exp(s - jnp.max(s, axis=1, keepdims=True))
        p = p * (1.0 / jnp.sum(p, axis=1, keepdims=True))
        outs.append(_dot(p.astype(BF16), kv_ref[:, d + hh * hd:d + (hh + 1) * hd]).astype(BF16))
    o = jnp.concatenate(outs, axis=1)
    z = alpha * h + _dot(o, wo_ref[...])
    o_ref[...] = _layer_norm(z, g_ref[...], b_ref[...])


def _cross_attn(h2, w_q, memkv, w_out, g, b, alpha, seq, n_mem):
    t, d = h2.shape
    per_batch = seq // ROW_BLOCK
    row = lambda i: (i, 0)
    full = lambda i: (0, 0)
    kern = functools.partial(_cross_attn_kernel, alpha=alpha, q_scale=(d // MEM_HEADS) ** -0.5)
    return pl.pallas_call(
        kern, grid=(t // ROW_BLOCK,),
        in_specs=[pl.BlockSpec((ROW_BLOCK, d), row), pl.BlockSpec((d, d), full),
                  pl.BlockSpec((n_mem, 2 * d), lambda i: (i // per_batch, 0)),
                  pl.BlockSpec((d, d), full), pl.BlockSpec((1, d), full), pl.BlockSpec((1, d), full)],
        out_specs=pl.BlockSpec((ROW_BLOCK, d), row),
        out_shape=jax.ShapeDtypeStruct((t, d), F32),
        compiler_params=_params(1), name="cross_attn",
    )(h2, w_q, memkv, w_out, g.reshape(1, d), b.reshape(1, d))


def _route(h, wr_hi, wr_lo, br):
    h_hi = h.astype(BF16)
    h_lo = (h - h_hi.astype(F32)).astype(BF16)
    logits = _dot(h_hi, wr_hi) + _dot(h_hi, wr_lo) + _dot(h_lo, wr_hi) + br
    lane = lax.broadcasted_iota(I32, logits.shape, 1).astype(F32)
    gl = jnp.where(lane < N_GROUPS, logits, NEG_INF)
    gmax = jnp.max(gl, axis=1, keepdims=True)
    g_sel = jnp.min(jnp.where(gl == gmax, lane, float(LANES)), axis=1, keepdims=True)
    g_gate = 1.0 / jnp.sum(jnp.exp(gl - gmax), axis=1, keepdims=True)
    first = N_GROUPS + g_sel * EXPERTS_PER_GROUP
    el = jnp.where((lane >= first) & (lane < first + EXPERTS_PER_GROUP), logits, NEG_INF)
    v1 = jnp.max(el, axis=1, keepdims=True)
    i1 = jnp.min(jnp.where(el == v1, lane, float(LANES)), axis=1, keepdims=True)
    el2 = jnp.where(lane == i1, NEG_INF, el)
    v2 = jnp.max(el2, axis=1, keepdims=True)
    i2 = jnp.min(jnp.where(el2 == v2, lane, float(LANES)), axis=1, keepdims=True)
    r = jnp.exp(v2 - v1)
    inv = g_gate / (1.0 + r)
    return jnp.where(lane == i1, inv, 0.0) + jnp.where(lane == i2, inv * r, 0.0)


def _moe_kernel(h_ref, wrh_ref, wrl_ref, br_ref, wg_ref, wu_ref, wd_ref, g_ref, b_ref, o_ref,
                xb_ref, comb_ref, acc_ref, *, alpha):
    e = pl.program_id(1)

    @pl.when(e == 0)
    def _():
        h = h_ref[...]
        comb_ref[...] = _route(h, wrh_ref[...], wrl_ref[...], br_ref[...])
        xb_ref[...] = h.astype(BF16)
        acc_ref[...] = jnp.zeros(acc_ref.shape, F32)

    x = xb_ref[...]
    comb = comb_ref[...]
    lane = lax.broadcasted_iota(I32, comb.shape, 1)
    c = jnp.sum(jnp.where(lane == N_GROUPS + e, comb, 0.0), axis=1, keepdims=True)
    a = _dot(x, wg_ref[0])
    u = _dot(x, wu_ref[0])
    hid = a * (1.0 / (1.0 + jnp.exp(-a))) * u
    acc_ref[...] += _dot((c * hid).astype(BF16), wd_ref[0])

    @pl.when(e == pl.num_programs(1) - 1)
    def _():
        z = alpha * h_ref[...] + acc_ref[...]
        o_ref[...] = _layer_norm(z, g_ref[...], b_ref[...])


def _moe(h2, wr_hi, wr_lo, br, w_gate, w_up, w_down, g, b, alpha):
    t, d = h2.shape
    n_e, _, ff = w_gate.shape
    row = lambda i, e: (i, 0)
    full = lambda i, e: (0, 0)
    return pl.pallas_call(
        functools.partial(_moe_kernel, alpha=alpha), grid=(t // ROW_BLOCK, n_e),
        in_specs=[pl.BlockSpec((ROW_BLOCK, d), row), pl.BlockSpec((d, LANES), full),
                  pl.BlockSpec((d, LANES), full), pl.BlockSpec((1, LANES), full),
                  pl.BlockSpec((1, d, ff), lambda i, e: (e, 0, 0)),
                  pl.BlockSpec((1, d, ff), lambda i, e: (e, 0, 0)),
                  pl.BlockSpec((1, ff, d), lambda i, e: (e, 0, 0)),
                  pl.BlockSpec((1, d), full), pl.BlockSpec((1, d), full)],
        out_specs=pl.BlockSpec((ROW_BLOCK, d), row),
        out_shape=jax.ShapeDtypeStruct((t, d), F32),
        scratch_shapes=[pltpu.VMEM((ROW_BLOCK, d), BF16), pltpu.VMEM((ROW_BLOCK, LANES), F32),
                        pltpu.VMEM((ROW_BLOCK, d), F32)],
        compiler_params=_params(2), name="moe",
    )(h2, wr_hi, wr_lo, br, w_gate, w_up, w_down, g.reshape(1, d), b.reshape(1, d))


_COL_CQ, _COL_CKV, _COL_KROPE, _COL_KIDX, _COL_WIDX, _COL_END = 0, 256, 512, 640, 768, 896


def _dsa_proj_kernel(h_ref, win_ref, gq_ref, gkv_ref, wuq_ref, wqi_ref, wuk_ref,
                     cr_ref, sur_ref, sdr_ref, ci_ref, sui_ref, sdi_ref,
                     qf_ref, kv_ref, qi_ref, ki_ref, wi_ref, *, q_scale, w_scale):
    y = _dot(h_ref[...].astype(BF16), win_ref[...])
    cr, sur, sdr = cr_ref[...], sur_ref[...], sdr_ref[...]
    ci, sui, sdi = ci_ref[...], sui_ref[...], sdi_ref[...]
    half_r, half_i = MLA_ROPE_DIM // 2, IDX_ROPE_DIM // 2
    c_q = _rms_norm(y[:, _COL_CQ:_COL_CKV], gq_ref[...]).astype(BF16)
    c_kv = _rms_norm(y[:, _COL_CKV:_COL_KROPE], gkv_ref[...])
    kv_ref[:, 0:MLA_KV_RANK] = c_kv.astype(BF16)
    kv_ref[:, MLA_KV_RANK:MLA_FEAT] = _rope(y[:, _COL_KROPE:_COL_KIDX], cr, sur, sdr, half_r).astype(BF16)
    ki_ref[...] = _rope(y[:, _COL_KIDX:_COL_WIDX], ci, sui, sdi, half_i).astype(BF16)
    wi_ref[...] = y[:, _COL_WIDX:_COL_END] * w_scale
    q = _dot(c_q, wuq_ref[...])
    qb = q.astype(BF16)
    hd = MLA_ROPE_DIM + MLA_NOPE_DIM
    for hh in range(MLA_HEADS):
        lat = _dot(qb[:, hh * hd:(hh + 1) * hd], wuk_ref[hh])
        qf_ref[hh, :, 0:MLA_KV_RANK] = (lat * q_scale).astype(BF16)
        roped = _rope(q[:, hh * hd:(hh + 1) * hd], cr, sur, sdr, half_r)
        qf_ref[hh, :, MLA_KV_RANK:MLA_FEAT] = (roped * q_scale).astype(BF16)
    qi = _dot(c_q, wqi_ref[...])
    for p in range(IDX_HEADS // 2):
        qi_ref[p] = _rope(qi[:, p * LANES:(p + 1) * LANES], ci, sui, sdi, half_i).astype(BF16)


def _dsa_proj(h2, w_in_p, gq, gkv, w_uq, w_qidx, w_ukt, tabs_r, tabs_i):
    t, d = h2.shape
    row = lambda i: (i, 0)
    full = lambda i: (0, 0)
    full3 = lambda i: (0, 0, 0)
    head_row = lambda i: (0, i, 0)
    n_pairs = IDX_HEADS // 2
    kern = functools.partial(_dsa_proj_kernel, q_scale=(MLA_ROPE_DIM + MLA_NOPE_DIM) ** -0.5,
                             w_scale=(IDX_HEADS * IDX_DIM) ** -0.5)
    tab = pl.BlockSpec((ROW_BLOCK, LANES), row)
    return pl.pallas_call(
        kern, grid=(t // ROW_BLOCK,),
        in_specs=[pl.BlockSpec((ROW_BLOCK, d), row), pl.BlockSpec(w_in_p.shape, full),
                  pl.BlockSpec((1, MLA_Q_RANK), full), pl.BlockSpec((1, MLA_KV_RANK), full),
                  pl.BlockSpec(w_uq.shape, full), pl.BlockSpec(w_qidx.shape, full),
                  pl.BlockSpec(w_ukt.shape, full3)] + [tab] * 6,
        out_specs=[pl.BlockSpec((MLA_HEADS, ROW_BLOCK, MLA_FEAT), head_row),
                   pl.BlockSpec((ROW_BLOCK, MLA_FEAT), row),
                   pl.BlockSpec((n_pairs, ROW_BLOCK, LANES), head_row),
                   pl.BlockSpec((ROW_BLOCK, LANES), row),
                   pl.BlockSpec((ROW_BLOCK, LANES), row)],
        out_shape=[jax.ShapeDtypeStruct((MLA_HEADS, t, MLA_FEAT), BF16),
                   jax.ShapeDtypeStruct((t, MLA_FEAT), BF16),
                   jax.ShapeDtypeStruct((n_pairs, t, LANES), BF16),
                   jax.ShapeDtypeStruct((t, LANES), BF16),
                   jax.ShapeDtypeStruct((t, LANES), F32)],
        compiler_params=_params(1), name="dsa_proj",
    )(h2, w_in_p, gq.reshape(1, -1), gkv.reshape(1, -1), w_uq, w_qidx, w_ukt, *tabs_r, *tabs_i)


def _sortable_key(x):
    bits = lax.bitcast_convert_type(x, I32)
    return bits ^ ((bits >> 31) & 0x7FFFFFFF)


def _dsa_attn_kernel(qi_ref, wi_ref, qf_ref, ki_ref, kv_ref, wuv_ref, o_ref,
                     key_ref, thr_ref, m_ref, l_ref, acc_ref, *, kt, top_k):
    c = pl.program_id(1)
    n_allowed = (c + 1) * CHUNK
    n_tiles = (n_allowed + kt - 1) // kt
    n_pairs = IDX_HEADS // 2
    slabs = kt // LANES

    qi = qi_ref[...].reshape(n_pairs * CHUNK, LANES)
    lane = lax.broadcasted_iota(I32, qi.shape, 1)
    zero = jnp.zeros_like(qi)
    qi_rows = jnp.concatenate([jnp.where(lane < IDX_DIM, qi, zero),
                               jnp.where(lane >= IDX_DIM, qi, zero)], axis=0)
    w = wi_ref[...]
    head_of_slab = [2 * r for r in range(n_pairs)] + [2 * r + 1 for r in range(n_pairs)]
    w_rows = [jnp.broadcast_to(w[:, hd:hd + 1], (CHUNK, LANES)) for hd in head_of_slab]

    def score_tile(t, carry):
        lo, hi = carry
        start = pl.multiple_of(t * kt, kt)
        logits = _dot_nt(qi_rows, ki_ref[pl.ds(start, kt), :])
        cols = []
        for s in range(slabs):
            sc = jnp.zeros((CHUNK, LANES), F32)
            for r in range(IDX_HEADS):
                sc = sc + jnp.maximum(logits[r * CHUNK:(r + 1) * CHUNK, s * LANES:(s + 1) * LANES], 0.0) * w_rows[r]
            cols.append(sc)
        sc = jnp.concatenate(cols, axis=1)
        col = t * kt + lax.broadcasted_iota(I32, sc.shape, 1)
        valid = col < n_allowed
        key_ref[t] = _sortable_key(jnp.where(valid, sc, NEG_INF))
        lo = jnp.minimum(lo, jnp.min(jnp.where(valid, sc, float("inf")), axis=1, keepdims=True))
        hi = jnp.maximum(hi, jnp.max(jnp.where(valid, sc, NEG_INF), axis=1, keepdims=True))
        return lo, hi

    lo_f, hi_f = lax.fori_loop(0, n_tiles, score_tile,
                               (jnp.full((CHUNK, 1), float("inf"), F32), jnp.full((CHUNK, 1), NEG_INF, F32)))

    thr_ref[...] = jnp.full(thr_ref.shape, NEG_INF_KEY + 1, I32)

    @pl.when(n_allowed > top_k)
    def _():
        def count_ge(mid):
            def body(t, cnt):
                k = key_ref[t]
                for s in range(slabs):
                    cnt = cnt + jnp.where(k[:, s * LANES:(s + 1) * LANES] >= mid, 1, 0)
                return cnt
            cnt = lax.fori_loop(0, n_tiles, body, jnp.zeros((CHUNK, LANES), I32))
            return jnp.sum(cnt.astype(F32), axis=1, keepdims=True)

        def bisect(_, st):
            lo, hi = st
            mid = (lo >> 1) + (hi >> 1) + ((lo | hi) & 1)
            cnt = count_ge(mid)
            ge = cnt >= float(top_k)
            lo2 = jnp.where(ge, mid, lo)
            hi2 = jnp.where(cnt == float(top_k), mid, jnp.where(ge, hi, mid - 1))
            return lo2, hi2

        lo, _ = lax.fori_loop(0, 32, bisect, (_sortable_key(lo_f), _sortable_key(hi_f)))
        thr_ref[...] = lo

    thr = thr_ref[...]
    qf = qf_ref[...].reshape(MLA_HEADS * CHUNK, MLA_FEAT)
    m_ref[...] = jnp.full(m_ref.shape, NEG_INF, F32)
    l_ref[...] = jnp.zeros(l_ref.shape, F32)
    acc_ref[...] = jnp.zeros(acc_ref.shape, F32)

    def attn_tile(t, carry):
        start = pl.multiple_of(t * kt, kt)
        kvt = kv_ref[pl.ds(start, kt), :]
        s = _dot_nt(qf, kvt)
        bias = jnp.where(key_ref[t] >= thr, 0.0, NEG_INF)
        s = jnp.concatenate([s[hh * CHUNK:(hh + 1) * CHUNK] + bias for hh in range(MLA_HEADS)], axis=0)
        m_old = m_ref[...]
        m_new = jnp.maximum(m_old, jnp.max(s, axis=1, keepdims=True))
        m_safe = jnp.where(m_new == NEG_INF, 0.0, m_new)
        p = jnp.exp(s - m_safe)
        alpha = jnp.exp(m_old - m_safe)
        l_ref[...] = alpha * l_ref[...] + jnp.sum(p, axis=1, keepdims=True)
        acc_ref[...] = alpha * acc_ref[...] + _dot(p.astype(BF16), kvt[:, 0:MLA_KV_RANK])
        m_ref[...] = m_new
        return carry

    lax.fori_loop(0, n_tiles, attn_tile, 0)

    o_lat = (acc_ref[...] * (1.0 / l_ref[...])).astype(BF16)
    for hh in range(MLA_HEADS):
        o_ref[:, hh * MLA_V_DIM:(hh + 1) * MLA_V_DIM] = _dot(
            o_lat[hh * CHUNK:(hh + 1) * CHUNK], wuv_ref[hh]).astype(BF16)


def _dsa_attn(qf, kv, qi, ki, wi, w_uv, bsz, seq):
    kt = min(KEY_TILE, seq)
    nc = seq // CHUNK
    n_pairs = IDX_HEADS // 2
    top_k = min(IDX_TOPK_MAX, seq // 4)
    kern = functools.partial(_dsa_attn_kernel, kt=kt, top_k=top_k)
    chunk3 = lambda b, c: (0, b * nc + c, 0)
    chunk2 = lambda b, c: (b * nc + c, 0)
    batch2 = lambda b, c: (b, 0)
    rows = MLA_HEADS * CHUNK
    return pl.pallas_call(
        kern, grid=(bsz, nc),
        in_specs=[pl.BlockSpec((n_pairs, CHUNK, LANES), chunk3),
                  pl.BlockSpec((CHUNK, LANES), chunk2),
                  pl.BlockSpec((MLA_HEADS, CHUNK, MLA_FEAT), chunk3),
                  pl.BlockSpec((seq, LANES), batch2),
                  pl.BlockSpec((seq, MLA_FEAT), batch2),
                  pl.BlockSpec(w_uv.shape, lambda b, c: (0, 0, 0))],
        out_specs=pl.BlockSpec((CHUNK, MLA_HEADS * MLA_V_DIM), chunk2),
        out_shape=jax.ShapeDtypeStruct((bsz * seq, MLA_HEADS * MLA_V_DIM), BF16),
        scratch_shapes=[pltpu.VMEM((seq // kt, CHUNK, kt), I32), pltpu.VMEM((CHUNK, 1), I32),
                        pltpu.VMEM((rows, 1), F32), pltpu.VMEM((rows, 1), F32),
                        pltpu.VMEM((rows, MLA_KV_RANK), F32)],
        compiler_params=_params(2), name="dsa_attn",
    )(qi, wi, qf, ki, kv, w_uv)


def _diff_lambda_init(layer_idx):
    import math
    return 0.8 - 0.6 * math.exp(-0.3 * layer_idx)


def _pad_dsa_w_in(w_in):
    d = w_in.shape[0]
    o_cq, o_ckv = 0, MLA_Q_RANK
    o_kr = o_ckv + MLA_KV_RANK
    o_ki = o_kr + MLA_ROPE_DIM
    o_wi = o_ki + IDX_DIM
    out = jnp.zeros((d, _COL_END), w_in.dtype)
    out = out.at[:, _COL_CQ:_COL_CQ + MLA_Q_RANK].set(w_in[:, o_cq:o_ckv])
    out = out.at[:, _COL_CKV:_COL_CKV + MLA_KV_RANK].set(w_in[:, o_ckv:o_kr])
    out = out.at[:, _COL_KROPE:_COL_KROPE + MLA_ROPE_DIM].set(w_in[:, o_kr:o_ki])
    out = out.at[:, _COL_KIDX:_COL_KIDX + IDX_DIM].set(w_in[:, o_ki:o_wi])
    out = out.at[:, _COL_KIDX + IDX_DIM:_COL_KIDX + 2 * IDX_DIM].set(w_in[:, o_ki:o_wi])
    out = out.at[:, _COL_WIDX:_COL_WIDX + IDX_HEADS].set(w_in[:, o_wi:o_wi + IDX_HEADS])
    return out.astype(BF16)


def _router_weights(w_group, b_group, w_expert, b_expert):
    d = w_group.shape[0]
    w = jnp.zeros((d, LANES), F32)
    w = w.at[:, 0:N_GROUPS].set(w_group).at[:, N_GROUPS:N_GROUPS + N_EXPERTS].set(w_expert)
    br = jnp.zeros((1, LANES), F32)
    br = br.at[0, 0:N_GROUPS].set(b_group).at[0, N_GROUPS:N_GROUPS + N_EXPERTS].set(b_expert)
    w_hi = w.astype(BF16)
    w_lo = (w - w_hi.astype(F32)).astype(BF16)
    return w_hi, w_lo, br


def kernel(x, mem, positions, a_w_in, a_lambda, a_subln_g, a_w_out, b_w_in, b_q_norm_g, b_kv_norm_g,
           b_w_uq, b_w_qidx, b_w_uk, b_w_uv, b_w_out, mem_w_kv, xa_w_q, xa_w_out,
           moe_w_group, moe_b_group, moe_w_expert, moe_b_expert, moe_w_gate, moe_w_up, moe_w_down,
           ln_g, ln_b):
    bsz, seq, d = x.shape
    n_mem = mem.shape[1]
    depth = ln_g.shape[0]
    alpha = (2.0 * depth) ** 0.25
    assert seq % ROW_BLOCK == 0 and seq % KEY_TILE == 0
    assert ROW_BLOCK == ATTN_BLOCK

    tabs_d = _rope_lane_tables(positions, DIFF_ROPE_DIM, DIFF_HEAD_DIM, True)
    tabs_r = _rope_lane_tables(positions, MLA_ROPE_DIM, LANES, False)
    tabs_i = _rope_lane_tables(positions, IDX_ROPE_DIM, IDX_DIM, True)

    memkv = _mem_kv(mem.reshape(bsz * n_mem, d), mem_w_kv.astype(BF16))

    h = x.reshape(bsz * seq, d)
    for i in range(depth):
        j = i // N_MIXERS
        if i % N_MIXERS == 0:
            qt, k, vt = _diff_qkv(h, a_w_in[j].astype(BF16), tabs_d)
            mix = _diff_attn(qt, k, vt, a_lambda[j], a_subln_g[j], bsz, seq, _diff_lambda_init(i))
            w_mix_out = a_w_out[j]
        else:
            w_ukt = jnp.pad(jnp.swapaxes(b_w_uk[j], 1, 2), ((0, 0), (MLA_ROPE_DIM, 0), (0, 0))).astype(BF16)
            qf, kv, qi, ki, wi = _dsa_proj(h, _pad_dsa_w_in(b_w_in[j]), b_q_norm_g[j], b_kv_norm_g[j],
                                           b_w_uq[j].astype(BF16), b_w_qidx[j].astype(BF16), w_ukt,
                                           tabs_r, tabs_i)
            mix = _dsa_attn(qf, kv, qi, ki, wi, b_w_uv[j].astype(BF16), bsz, seq)
            w_mix_out = b_w_out[j]
        h = _proj_res_ln(mix, w_mix_out.astype(BF16), h, ln_g[i, 0], ln_b[i, 0], alpha)
        h = _cross_attn(h, xa_w_q[i].astype(BF16), memkv, xa_w_out[i].astype(BF16),
                        ln_g[i, 1], ln_b[i, 1], alpha, seq, n_mem)
        wr_hi, wr_lo, br = _router_weights(moe_w_group[i], moe_b_group[i], moe_w_expert[i], moe_b_expert[i])
        h = _moe(h, wr_hi, wr_lo, br, moe_w_gate[i].astype(BF16), moe_w_up[i].astype(BF16),
                 moe_w_down[i].astype(BF16), ln_g[i, 2], ln_b[i, 2], alpha)
    return h.reshape(bsz, seq, d)
```

```python
import functools

import jax
import jax.numpy as jnp
from jax import lax
from jax.experimental import pallas as pl
from jax.experimental.pallas import tpu as pltpu

F32 = jnp.float32
BF16 = jnp.bfloat16
I32 = jnp.int32

CHUNK = 64
ROPE_THETA = 500000.0
LN_EPS = 1e-5
N_MIXERS = 2
DIFF_HEADS = 8
DIFF_HEAD_DIM = 64
DIFF_ROPE_DIM = DIFF_HEAD_DIM // 4
MLA_HEADS = 8
MLA_Q_RANK = 256
MLA_KV_RANK = 256
MLA_ROPE_DIM = 32
MLA_NOPE_DIM = 96
MLA_V_DIM = 128
IDX_HEADS = 16
IDX_DIM = 64
IDX_ROPE_DIM = IDX_DIM // 4
IDX_TOPK_MAX = 256
MEM_HEADS = 4
N_GROUPS = 4
EXPERTS_PER_GROUP = 4
N_EXPERTS = N_GROUPS * EXPERTS_PER_GROUP

LANES = 128
VMEM_LIMIT_BYTES = 56 * 1024 * 1024

ROW_BLOCK = 512
ATTN_BLOCK = 512
KEY_TILE = 512
QUERY_BLOCK = 128
HEADS_PER_GROUP = 4
BISECT_STEPS_PER_CHECK = 4
MLA_FEAT = MLA_KV_RANK + LANES

NEG_INF = float("-inf")
NEG_INF_KEY = -2139095041

_NT = (((1,), (1,)), ((), ()))


def _params(n_axes):
    return pltpu.CompilerParams(dimension_semantics=("arbitrary",) * n_axes,
                                vmem_limit_bytes=VMEM_LIMIT_BYTES)


def _dot(a, b):
    return jnp.dot(a, b, preferred_element_type=F32)


def _dot_nt(a, b):
    return lax.dot_general(a, b, _NT, preferred_element_type=F32)


def _layer_norm(z, g, b):
    mu = jnp.mean(z, axis=-1, keepdims=True)
    zc = z - mu
    var = jnp.mean(zc * zc, axis=-1, keepdims=True)
    return zc * lax.rsqrt(var + LN_EPS) * g + b


def _rms_norm(x, g):
    return x * lax.rsqrt(jnp.mean(x * x, axis=-1, keepdims=True) + LN_EPS) * g


def _rope(y, c, s_up, s_dn, half):
    return (y * c + pltpu.roll(y, LANES - half, 1) * s_up + pltpu.roll(y, half, 1) * s_dn)


def _rope_lane_tables(positions, rot_dim, period, keep_rest):
    half = rot_dim // 2
    inv_freq = ROPE_THETA ** (-jnp.arange(0, rot_dim, 2, dtype=F32) / rot_dim)
    ang = positions.astype(F32)[..., None] * inv_freq
    cos, sin = jnp.cos(ang), jnp.sin(ang)
    lead = positions.shape
    rest = jnp.full(lead + (period - rot_dim,), 1.0 if keep_rest else 0.0, F32)
    zrest = jnp.zeros(lead + (period - rot_dim,), F32)
    zhalf = jnp.zeros(lead + (half,), F32)
    reps = LANES // period
    out = []
    for parts in ((cos, cos, rest), (-sin, zhalf, zrest), (zhalf, sin, zrest)):
        t = jnp.concatenate(parts, axis=-1)
        out.append(jnp.tile(t, (1,) * len(lead) + (reps,)).reshape(-1, LANES))
    return out


LOG2E = 1.4426950408889634


def _diff_qkv_kernel(h_ref, w_ref, c_ref, su_ref, sd_ref, qt_ref, k_ref, vt_ref, *, q_scale, half):
    x = h_ref[...].astype(BF16)
    c, su, sd = c_ref[...], su_ref[...], sd_ref[...]
    n_heads = qt_ref.shape[0]
    for j2 in range(3 * n_heads // 2):
        y2 = _dot(x, w_ref[:, j2 * 2 * LANES:(j2 + 1) * 2 * LANES])
        for s in range(2):
            j = 2 * j2 + s
            y = y2[:, s * LANES:(s + 1) * LANES]
            if j < n_heads:
                qt_ref[j, 0] = (_rope(y, c, su, sd, half) * q_scale).T.astype(BF16)
            elif j < 2 * n_heads:
                k_ref[:, (j - n_heads) * LANES:(j - n_heads + 1) * LANES] = _rope(y, c, su, sd, half).astype(BF16)
            else:
                vt_ref[j - 2 * n_heads, 0] = y.T.astype(BF16)


def _diff_qkv(h2, w_in, tabs):
    t, d = h2.shape
    hd = 2 * DIFF_HEAD_DIM
    nb = t // ROW_BLOCK
    kern = functools.partial(_diff_qkv_kernel, q_scale=DIFF_HEAD_DIM ** -0.5 * LOG2E, half=DIFF_ROPE_DIM // 2)
    row = lambda i: (i, 0)
    full = lambda i: (0, 0)
    t_shape = jax.ShapeDtypeStruct((DIFF_HEADS, nb, hd, ROW_BLOCK), BF16)
    t_spec = pl.BlockSpec((DIFF_HEADS, 1, hd, ROW_BLOCK), lambda i: (0, i, 0, 0))
    return pl.pallas_call(
        kern, grid=(nb,),
        in_specs=[pl.BlockSpec((ROW_BLOCK, d), row), pl.BlockSpec(w_in.shape, full)]
        + [pl.BlockSpec((ROW_BLOCK, LANES), row)] * 3,
        out_specs=[t_spec, pl.BlockSpec((ROW_BLOCK, DIFF_HEADS * hd), row), t_spec],
        out_shape=[t_shape, jax.ShapeDtypeStruct((t, DIFF_HEADS * hd), BF16), t_shape],
        compiler_params=_params(1), name="diff_qkv",
    )(h2, w_in, *tabs)


def _diff_attn_kernel(lam_ref, g_ref, qt_ref, k_ref, vt_ref, o_ref, m_ref, l_ref, acc_ref, *,
                      blk, lambda_init):
    i = pl.program_id(2)
    lam = lam_ref[...]
    lam_full = (jnp.exp(jnp.sum(lam[0:1] * lam[1:2], axis=1, keepdims=True))
                - jnp.exp(jnp.sum(lam[2:3] * lam[3:4], axis=1, keepdims=True)) + lambda_init)
    qt = qt_ref[0, 0]
    feat = lax.broadcasted_iota(I32, qt.shape, 0)
    zero = jnp.zeros_like(qt)
    q_maps = (jnp.where(feat < DIFF_HEAD_DIM, qt, zero), jnp.where(feat >= DIFF_HEAD_DIM, qt, zero))
    m_ref[...] = jnp.full(m_ref.shape, NEG_INF, F32)
    l_ref[...] = jnp.zeros(l_ref.shape, F32)
    acc_ref[...] = jnp.zeros(acc_ref.shape, F32)

    def step(j, masked):
        k = k_ref[pl.ds(pl.multiple_of(j * blk, blk), blk), :]
        vt = vt_ref[0, j]
        for c in range(2):
            s = _dot(k, q_maps[c])
            if masked:
                kc = lax.broadcasted_iota(I32, s.shape, 0) // CHUNK
                qc = lax.broadcasted_iota(I32, s.shape, 1) // CHUNK
                s = jnp.where(kc <= qc, s, NEG_INF)
            m_old = m_ref[c]
            m_new = jnp.maximum(m_old, jnp.max(s, axis=0, keepdims=True))
            p = jnp.exp2(s - m_new)
            alpha = jnp.exp2(m_old - m_new)
            l_ref[c] = alpha * l_ref[c] + jnp.sum(p, axis=0, keepdims=True)
            acc_ref[c] = alpha * acc_ref[c] + _dot(vt, p.astype(BF16))
            m_ref[c] = m_new

    def full_step(j, carry):
        step(j, False)
        return carry

    lax.fori_loop(0, i, full_step, 0)
    step(i, True)

    ot = acc_ref[0] * (1.0 / l_ref[0]) - lam_full * (acc_ref[1] * (1.0 / l_ref[1]))
    ot = ot * lax.rsqrt(jnp.mean(ot * ot, axis=0, keepdims=True) + LN_EPS) * (1.0 - lambda_init)
    g = g_ref[...]
    for s in range(blk // LANES):
        o_ref[s * LANES:(s + 1) * LANES, :] = (ot[:, s * LANES:(s + 1) * LANES] * g).T.astype(BF16)


def _diff_attn(qt, k, vt, lam, subln_g, bsz, seq, lambda_init):
    blk = ATTN_BLOCK
    nq = seq // blk
    hd = 2 * DIFF_HEAD_DIM
    kern = functools.partial(_diff_attn_kernel, blk=blk, lambda_init=lambda_init)
    g = jnp.broadcast_to(subln_g.astype(F32)[:, None], (hd, LANES))
    return pl.pallas_call(
        kern, grid=(bsz, DIFF_HEADS, nq),
        in_specs=[
            pl.BlockSpec(lam.shape, lambda b, h, i: (0, 0)),
            pl.BlockSpec((hd, LANES), lambda b, h, i: (0, 0)),
            pl.BlockSpec((1, 1, hd, blk), lambda b, h, i: (h, b * nq + i, 0, 0)),
            pl.BlockSpec((seq, hd), lambda b, h, i: (b, h)),
            pl.BlockSpec((1, nq, hd, blk), lambda b, h, i: (h, b, 0, 0)),
        ],
        out_specs=pl.BlockSpec((blk, hd), lambda b, h, i: (b * nq + i, h)),
        out_shape=jax.ShapeDtypeStruct((bsz * seq, DIFF_HEADS * hd), BF16),
        scratch_shapes=[pltpu.VMEM((2, 1, blk), F32), pltpu.VMEM((2, 1, blk), F32),
                        pltpu.VMEM((2, hd, blk), F32)],
        compiler_params=_params(3), name="diff_attn",
    )(lam, g, qt, k, vt)


def _proj_res_ln_kernel(a_ref, w_ref, h_ref, g_ref, b_ref, o_ref, *, alpha):
    z = alpha * h_ref[...] + _dot(a_ref[...], w_ref[...])
    o_ref[...] = _layer_norm(z, g_ref[...], b_ref[...])


def _proj_res_ln(a, w, h2, g, b, alpha):
    t, d = h2.shape
    k = a.shape[1]
    row = lambda i: (i, 0)
    full = lambda i: (0, 0)
    return pl.pallas_call(
        functools.partial(_proj_res_ln_kernel, alpha=alpha), grid=(t // ROW_BLOCK,),
        in_specs=[pl.BlockSpec((ROW_BLOCK, k), row), pl.BlockSpec((k, d), full),
                  pl.BlockSpec((ROW_BLOCK, d), row), pl.BlockSpec((1, d), full),
                  pl.BlockSpec((1, d), full)],
        out_specs=pl.BlockSpec((ROW_BLOCK, d), row),
        out_shape=jax.ShapeDtypeStruct((t, d), F32),
        compiler_params=_params(1), name="proj_res_ln",
    )(a, w, h2, g.reshape(1, d), b.reshape(1, d))


def _mem_kv_kernel(m_ref, w_ref, o_ref):
    o_ref[...] = _dot(m_ref[...].astype(BF16), w_ref[...]).astype(BF16)


def _mem_kv(mem2, w_kv):
    rows, d = mem2.shape
    n = w_kv.shape[1]
    blk = min(ROW_BLOCK, rows)
    return pl.pallas_call(
        _mem_kv_kernel, grid=(rows // blk,),
        in_specs=[pl.BlockSpec((blk, d), lambda i: (i, 0)), pl.BlockSpec((d, n), lambda i: (0, 0))],
        out_specs=pl.BlockSpec((blk, n), lambda i: (i, 0)),
        out_shape=jax.ShapeDtypeStruct((rows, n), BF16),
        compiler_params=_params(1), name="mem_kv",
    )(mem2, w_kv)


def _cross_attn_kernel(h_ref, wq_ref, kv_ref, wo_ref, g_ref, b_ref, o_ref, *, alpha, q_scale):
    h = h_ref[...]
    d = h.shape[1]
    hd = d // MEM_HEADS
    q = (_dot(h.astype(BF16), wq_ref[...]) * q_scale).astype(BF16)
    outs = []
    for hh in range(MEM_HEADS):
        s = _dot_nt(q[:, hh * hd:(hh + 1) * hd], kv_ref[:, hh * hd:(hh + 1) * hd])
        p = jnp.exp(s - jnp.max(s, axis=1, keepdims=True))
        p = p * (1.0 / jnp.sum(p, axis=1, keepdims=True))
        outs.append(_dot(p.astype(BF16), kv_ref[:, d + hh * hd:d + (hh + 1) * hd]).astype(BF16))
    o = jnp.concatenate(outs, axis=1)
    z = alpha * h + _dot(o, wo_ref[...])
    o_ref[...] = _layer_norm(z, g_ref[...], b_ref[...])


def _cross_attn(h2, w_q, memkv, w_out, g, b, alpha, seq, n_mem):
    t, d = h2.shape
    per_batch = seq // ROW_BLOCK
    row = lambda i: (i, 0)
    full = lambda i: (0, 0)
    kern = functools.partial(_cross_attn_kernel, alpha=alpha, q_scale=(d // MEM_HEADS) ** -0.5)
    return pl.pallas_call(
        kern, grid=(t // ROW_BLOCK,),
        in_specs=[pl.BlockSpec((ROW_BLOCK, d), row), pl.BlockSpec((d, d), full),
                  pl.BlockSpec((n_mem, 2 * d), lambda i: (i // per_batch, 0)),
                  pl.BlockSpec((d, d), full), pl.BlockSpec((1, d), full), pl.BlockSpec((1, d), full)],
        out_specs=pl.BlockSpec((ROW_BLOCK, d), row),
        out_shape=jax.ShapeDtypeStruct((t, d), F32),
        compiler_params=_params(1), name="cross_attn",
    )(h2, w_q, memkv, w_out, g.reshape(1, d), b.reshape(1, d))


def _route(h, wr_hi, wr_lo, br):
    h_hi = h.astype(BF16)
    h_lo = (h - h_hi.astype(F32)).astype(BF16)
    logits = _dot(h_hi, wr_hi) + _dot(h_hi, wr_lo) + _dot(h_lo, wr_hi) + br
    lane = lax.broadcasted_iota(I32, logits.shape, 1).astype(F32)
    gl = jnp.where(lane < N_GROUPS, logits, NEG_INF)
    gmax = jnp.max(gl, axis=1, keepdims=True)
    g_sel = jnp.min(jnp.where(gl == gmax, lane, float(LANES)), axis=1, keepdims=True)
    g_gate = 1.0 / jnp.sum(jnp.exp(gl - gmax), axis=1, keepdims=True)
    first = N_GROUPS + g_sel * EXPERTS_PER_GROUP
    el = jnp.where((lane >= first) & (lane < first + EXPERTS_PER_GROUP), logits, NEG_INF)
    v1 = jnp.max(el, axis=1, keepdims=True)
    i1 = jnp.min(jnp.where(el == v1, lane, float(LANES)), axis=1, keepdims=True)
    el2 = jnp.where(lane == i1, NEG_INF, el)
    v2 = jnp.max(el2, axis=1, keepdims=True)
    i2 = jnp.min(jnp.where(el2 == v2, lane, float(LANES)), axis=1, keepdims=True)
    r = jnp.exp(v2 - v1)
    inv = g_gate / (1.0 + r)
    return jnp.where(lane == i1, inv, 0.0) + jnp.where(lane == i2, inv * r, 0.0)


def _moe_kernel(h_ref, wrh_ref, wrl_ref, br_ref, wg_ref, wu_ref, wd_ref, g_ref, b_ref, o_ref,
                xb_ref, comb_ref, acc_ref, *, alpha):
    e = pl.program_id(1)

    @pl.when(e == 0)
    def _():
        h = h_ref[...]
        comb_ref[...] = _route(h, wrh_ref[...], wrl_ref[...], br_ref[...])
        xb_ref[...] = h.astype(BF16)
        acc_ref[...] = jnp.zeros(acc_ref.shape, F32)

    x = xb_ref[...]
    comb = comb_ref[...]
    lane = lax.broadcasted_iota(I32, comb.shape, 1)
    c = jnp.sum(jnp.where(lane == N_GROUPS + e, comb, 0.0), axis=1, keepdims=True)
    a = _dot(x, wg_ref[0])
    u = _dot(x, wu_ref[0])
    hid = a * (1.0 / (1.0 + jnp.exp(-a))) * u
    acc_ref[...] += _dot((c * hid).astype(BF16), wd_ref[0])

    @pl.when(e == pl.num_programs(1) - 1)
    def _():
        z = alpha * h_ref[...] + acc_ref[...]
        o_ref[...] = _layer_norm(z, g_ref[...], b_ref[...])


def _moe(h2, wr_hi, wr_lo, br, w_gate, w_up, w_down, g, b, alpha):
    t, d = h2.shape
    n_e, _, ff = w_gate.shape
    row = lambda i, e: (i, 0)
    full = lambda i, e: (0, 0)
    return pl.pallas_call(
        functools.partial(_moe_kernel, alpha=alpha), grid=(t // ROW_BLOCK, n_e),
        in_specs=[pl.BlockSpec((ROW_BLOCK, d), row), pl.BlockSpec((d, LANES), full),
                  pl.BlockSpec((d, LANES), full), pl.BlockSpec((1, LANES), full),
                  pl.BlockSpec((1, d, ff), lambda i, e: (e, 0, 0)),
                  pl.BlockSpec((1, d, ff), lambda i, e: (e, 0, 0)),
                  pl.BlockSpec((1, ff, d), lambda i, e: (e, 0, 0)),
                  pl.BlockSpec((1, d), full), pl.BlockSpec((1, d), full)],
        out_specs=pl.BlockSpec((ROW_BLOCK, d), row),
        out_shape=jax.ShapeDtypeStruct((t, d), F32),
        scratch_shapes=[pltpu.VMEM((ROW_BLOCK, d), BF16), pltpu.VMEM((ROW_BLOCK, LANES), F32),
                        pltpu.VMEM((ROW_BLOCK, d), F32)],
        compiler_params=_params(2), name="moe",
    )(h2, wr_hi, wr_lo, br, w_gate, w_up, w_down, g.reshape(1, d), b.reshape(1, d))


_COL_CQ, _COL_CKV, _COL_KROPE, _COL_KIDX, _COL_WIDX, _COL_END = 0, 256, 512, 640, 768, 896


def _dsa_proj_kernel(h_ref, win_ref, gq_ref, gkv_ref, wuq_ref, wqi_ref, wuk_ref,
                     cr_ref, sur_ref, sdr_ref, ci_ref, sui_ref, sdi_ref,
                     qft_ref, kv_ref, kvt_ref, qit_ref, ki_ref, wit_ref, *, q_scale, w_scale):
    y = _dot(h_ref[...].astype(BF16), win_ref[...])
    cr, sur, sdr = cr_ref[...], sur_ref[...], sdr_ref[...]
    ci, sui, sdi = ci_ref[...], sui_ref[...], sdi_ref[...]
    half_r, half_i = MLA_ROPE_DIM // 2, IDX_ROPE_DIM // 2
    qb = QUERY_BLOCK
    n_qb = y.shape[0] // qb
    c_q = _rms_norm(y[:, _COL_CQ:_COL_CKV], gq_ref[...]).astype(BF16)
    c_kv = _rms_norm(y[:, _COL_CKV:_COL_KROPE], gkv_ref[...])
    kv_ref[:, 0:MLA_KV_RANK] = c_kv.astype(BF16)
    kvt_ref[0] = c_kv.T.astype(BF16)
    kv_ref[:, MLA_KV_RANK:MLA_FEAT] = _rope(y[:, _COL_KROPE:_COL_KIDX], cr, sur, sdr, half_r).astype(BF16)
    ki_ref[...] = _rope(y[:, _COL_KIDX:_COL_WIDX], ci, sui, sdi, half_i).astype(BF16)
    wit_ref[...] = (y[:, _COL_WIDX:_COL_END] * w_scale).T[0:IDX_HEADS, :]
    q = _dot(c_q, wuq_ref[...])
    q_bf = q.astype(BF16)
    hd = MLA_ROPE_DIM + MLA_NOPE_DIM
    for hh in range(MLA_HEADS):
        lat_t = (_dot(q_bf[:, hh * hd:(hh + 1) * hd], wuk_ref[hh]) * q_scale).T.astype(BF16)
        rope_t = (_rope(q[:, hh * hd:(hh + 1) * hd], cr, sur, sdr, half_r) * q_scale).T.astype(BF16)
        for bl in range(n_qb):
            qft_ref[bl, 0:MLA_KV_RANK, hh * qb:(hh + 1) * qb] = lat_t[:, bl * qb:(bl + 1) * qb]
            qft_ref[bl, MLA_KV_RANK:MLA_FEAT, hh * qb:(hh + 1) * qb] = rope_t[:, bl * qb:(bl + 1) * qb]
    qi = _dot(c_q, wqi_ref[...])
    for p in range(IDX_HEADS // 2):
        pair_t = _rope(qi[:, p * LANES:(p + 1) * LANES], ci, sui, sdi, half_i).T.astype(BF16)
        for bl in range(n_qb):
            qit_ref[bl, p] = pair_t[:, bl * qb:(bl + 1) * qb]


def _dsa_proj(h2, w_in_p, gq, gkv, w_uq, w_qidx, w_ukt, tabs_r, tabs_i):
    t, d = h2.shape
    row = lambda i: (i, 0)
    full = lambda i: (0, 0)
    full3 = lambda i: (0, 0, 0)
    n_pairs = IDX_HEADS // 2
    qb = QUERY_BLOCK
    n_qb = ROW_BLOCK // qb
    kern = functools.partial(_dsa_proj_kernel, q_scale=(MLA_ROPE_DIM + MLA_NOPE_DIM) ** -0.5 * LOG2E,
                             w_scale=(IDX_HEADS * IDX_DIM) ** -0.5)
    tab = pl.BlockSpec((ROW_BLOCK, LANES), row)
    return pl.pallas_call(
        kern, grid=(t // ROW_BLOCK,),
        in_specs=[pl.BlockSpec((ROW_BLOCK, d), row), pl.BlockSpec(w_in_p.shape, full),
                  pl.BlockSpec((1, MLA_Q_RANK), full), pl.BlockSpec((1, MLA_KV_RANK), full),
                  pl.BlockSpec(w_uq.shape, full), pl.BlockSpec(w_qidx.shape, full),
                  pl.BlockSpec(w_ukt.shape, full3)] + [tab] * 6,
        out_specs=[pl.BlockSpec((n_qb, MLA_FEAT, MLA_HEADS * qb), lambda i: (i, 0, 0)),
                   pl.BlockSpec((ROW_BLOCK, MLA_FEAT), row),
                   pl.BlockSpec((1, MLA_KV_RANK, ROW_BLOCK), lambda i: (i, 0, 0)),
                   pl.BlockSpec((n_qb, n_pairs, LANES, qb), lambda i: (i, 0, 0, 0)),
                   pl.BlockSpec((ROW_BLOCK, LANES), row),
                   pl.BlockSpec((IDX_HEADS, ROW_BLOCK), lambda i: (0, i))],
        out_shape=[jax.ShapeDtypeStruct((t // qb, MLA_FEAT, MLA_HEADS * qb), BF16),
                   jax.ShapeDtypeStruct((t, MLA_FEAT), BF16),
                   jax.ShapeDtypeStruct((t // ROW_BLOCK, MLA_KV_RANK, ROW_BLOCK), BF16),
                   jax.ShapeDtypeStruct((t // qb, n_pairs, LANES, qb), BF16),
                   jax.ShapeDtypeStruct((t, LANES), BF16),
                   jax.ShapeDtypeStruct((IDX_HEADS, t), F32)],
        compiler_params=_params(1), name="dsa_proj",
    )(h2, w_in_p, gq.reshape(1, -1), gkv.reshape(1, -1), w_uq, w_qidx, w_ukt, *tabs_r, *tabs_i)


def _sortable_key(x):
    bits = lax.bitcast_convert_type(x, I32)
    return bits ^ ((bits >> 31) & 0x7FFFFFFF)


def _dsa_attn_kernel(qit_ref, wit_ref, qft_ref, ki_ref, kv_ref, kvt_ref, wuvt_ref, o_ref,
                     key_ref, thr_ref, m_ref, l_ref, acc_ref, *, kt, top_k):
    g = pl.program_id(1)
    qb = QUERY_BLOCK
    n_tiles = ((g + 1) * qb + kt - 1) // kt
    n_pairs = IDX_HEADS // 2
    lane_q = lax.broadcasted_iota(I32, (1, qb), 1)
    n_allowed = (g * (qb // CHUNK) + 1 + lane_q // CHUNK) * CHUNK

    feat = lax.broadcasted_iota(I32, (LANES, qb), 0)
    pair_w = []
    for p in range(n_pairs):
        slab = qit_ref[0, p]
        zero = jnp.zeros_like(slab)
        pair_w.append(jnp.concatenate([jnp.where(feat < IDX_DIM, slab, zero),
                                       jnp.where(feat >= IDX_DIM, slab, zero)], axis=1))
    wt = wit_ref[...]

    def score_tile(t, carry):
        lo, hi = carry
        k = ki_ref[pl.ds(pl.multiple_of(t * kt, kt), kt), :]
        sc = jnp.zeros((kt, qb), F32)
        for p in range(n_pairs):
            lg = _dot(k, pair_w[p])
            sc = (sc + jnp.maximum(lg[:, 0:qb], 0.0) * wt[2 * p:2 * p + 1, :]
                  + jnp.maximum(lg[:, qb:2 * qb], 0.0) * wt[2 * p + 1:2 * p + 2, :])
        kk = t * kt + lax.broadcasted_iota(I32, sc.shape, 0)
        valid = kk < n_allowed
        key_ref[t] = _sortable_key(jnp.where(valid, sc, NEG_INF))
        lo = jnp.minimum(lo, jnp.min(jnp.where(valid, sc, float("inf")), axis=0, keepdims=True))
        hi = jnp.maximum(hi, jnp.max(jnp.where(valid, sc, NEG_INF), axis=0, keepdims=True))
        return lo, hi

    lo_f, hi_f = lax.fori_loop(0, n_tiles, score_tile,
                               (jnp.full((1, qb), float("inf"), F32), jnp.full((1, qb), NEG_INF, F32)))

    keep_all = jnp.full((1, qb), NEG_INF_KEY + 1, I32)
    thr_ref[...] = keep_all

    @pl.when((g + 1) * qb > top_k)
    def _():
        def count_ge(mid):
            def body(t, cnt):
                hit = jnp.where(key_ref[t] >= mid, 1, 0)
                return cnt + jnp.sum(hit.reshape(kt // 8, 8, qb), axis=0)
            cnt = lax.fori_loop(0, n_tiles, body, jnp.zeros((8, qb), I32))
            return jnp.sum(cnt.astype(F32), axis=0, keepdims=True)

        def bisect(st):
            lo, hi = st
            for _ in range(BISECT_STEPS_PER_CHECK):
                mid = (lo >> 1) + (hi >> 1) + ((lo | hi) & 1)
                cnt = count_ge(mid)
                ge = cnt >= float(top_k)
                lo_next = jnp.where(ge, mid, lo)
                hi = jnp.where(cnt == float(top_k), mid, jnp.where(ge, hi, mid - 1))
                lo = lo_next
            return lo, hi

        def unresolved(st):
            lo, hi = st
            return jnp.max(jnp.where(hi > lo, 1.0, 0.0)) > 0.0

        lo, _ = lax.while_loop(unresolved, bisect, (_sortable_key(lo_f), _sortable_key(hi_f)))
        thr_ref[...] = jnp.where(n_allowed > top_k, lo, keep_all)

    thr = thr_ref[...]
    m_ref[...] = jnp.full(m_ref.shape, NEG_INF, F32)
    l_ref[...] = jnp.zeros(l_ref.shape, F32)
    acc_ref[...] = jnp.zeros(acc_ref.shape, F32)
    hpg = HEADS_PER_GROUP
    group = hpg * qb

    def attn_tile(t, carry):
        kv_rows = kv_ref[pl.ds(pl.multiple_of(t * kt, kt), kt), :]
        ckv_t = kvt_ref[t]
        bias = jnp.where(key_ref[t] >= thr, 0.0, NEG_INF)
        bias_g = jnp.concatenate([bias] * hpg, axis=1)
        for gi in range(MLA_HEADS // hpg):
            s = _dot(kv_rows, qft_ref[0, :, gi * group:(gi + 1) * group]) + bias_g
            m_old = m_ref[gi]
            m_new = jnp.maximum(m_old, jnp.max(s, axis=0, keepdims=True))
            m_safe = jnp.where(m_new == NEG_INF, 0.0, m_new)
            p = jnp.exp2(s - m_safe)
            alpha = jnp.exp2(m_old - m_safe)
            l_ref[gi] = alpha * l_ref[gi] + jnp.sum(p, axis=0, keepdims=True)
            acc_ref[gi] = alpha * acc_ref[gi] + _dot(ckv_t, p.astype(BF16))
            m_ref[gi] = m_new
        return carry

    lax.fori_loop(0, n_tiles, attn_tile, 0)

    for gi in range(MLA_HEADS // hpg):
        o_lat_t = (acc_ref[gi] * (1.0 / l_ref[gi])).astype(BF16)
        for hl in range(hpg):
            hh = gi * hpg + hl
            o_t = _dot(wuvt_ref[hh], o_lat_t[:, hl * qb:(hl + 1) * qb])
            o_ref[:, hh * MLA_V_DIM:(hh + 1) * MLA_V_DIM] = o_t.T.astype(BF16)


def _dsa_attn(qft, kv, kvt, qit, ki, wit, w_uvt, bsz, seq):
    kt = KEY_TILE
    qb = QUERY_BLOCK
    ng = seq // qb
    n_pairs = IDX_HEADS // 2
    top_k = min(IDX_TOPK_MAX, seq // 4)
    kern = functools.partial(_dsa_attn_kernel, kt=kt, top_k=top_k)
    batch2 = lambda b, g: (b, 0)
    n_groups, group = MLA_HEADS // HEADS_PER_GROUP, HEADS_PER_GROUP * qb
    return pl.pallas_call(
        kern, grid=(bsz, ng),
        in_specs=[pl.BlockSpec((1, n_pairs, LANES, qb), lambda b, g: (b * ng + g, 0, 0, 0)),
                  pl.BlockSpec((IDX_HEADS, qb), lambda b, g: (0, b * ng + g)),
                  pl.BlockSpec((1, MLA_FEAT, MLA_HEADS * qb), lambda b, g: (b * ng + g, 0, 0)),
                  pl.BlockSpec((seq, LANES), batch2),
                  pl.BlockSpec((seq, MLA_FEAT), batch2),
                  pl.BlockSpec((seq // kt, MLA_KV_RANK, kt), lambda b, g: (b, 0, 0)),
                  pl.BlockSpec(w_uvt.shape, lambda b, g: (0, 0, 0))],
        out_specs=pl.BlockSpec((qb, MLA_HEADS * MLA_V_DIM), lambda b, g: (b * ng + g, 0)),
        out_shape=jax.ShapeDtypeStruct((bsz * seq, MLA_HEADS * MLA_V_DIM), BF16),
        scratch_shapes=[pltpu.VMEM((seq // kt, kt, qb), I32), pltpu.VMEM((1, qb), I32),
                        pltpu.VMEM((n_groups, 1, group), F32), pltpu.VMEM((n_groups, 1, group), F32),
                        pltpu.VMEM((n_groups, MLA_KV_RANK, group), F32)],
        compiler_params=_params(2), name="dsa_attn",
    )(qit, wit, qft, ki, kv, kvt, w_uvt)


def _diff_lambda_init(layer_idx):
    import math
    return 0.8 - 0.6 * math.exp(-0.3 * layer_idx)


def _pad_dsa_w_in(w_in):
    d = w_in.shape[0]
    o_cq, o_ckv = 0, MLA_Q_RANK
    o_kr = o_ckv + MLA_KV_RANK
    o_ki = o_kr + MLA_ROPE_DIM
    o_wi = o_ki + IDX_DIM
    out = jnp.zeros((d, _COL_END), w_in.dtype)
    out = out.at[:, _COL_CQ:_COL_CQ + MLA_Q_RANK].set(w_in[:, o_cq:o_ckv])
    out = out.at[:, _COL_CKV:_COL_CKV + MLA_KV_RANK].set(w_in[:, o_ckv:o_kr])
    out = out.at[:, _COL_KROPE:_COL_KROPE + MLA_ROPE_DIM].set(w_in[:, o_kr:o_ki])
    out = out.at[:, _COL_KIDX:_COL_KIDX + IDX_DIM].set(w_in[:, o_ki:o_wi])
    out = out.at[:, _COL_KIDX + IDX_DIM:_COL_KIDX + 2 * IDX_DIM].set(w_in[:, o_ki:o_wi])
    out = out.at[:, _COL_WIDX:_COL_WIDX + IDX_HEADS].set(w_in[:, o_wi:o_wi + IDX_HEADS])
    return out.astype(BF16)


def _router_weights(w_group, b_group, w_expert, b_expert):
    d = w_group.shape[0]
    w = jnp.zeros((d, LANES), F32)
    w = w.at[:, 0:N_GROUPS].set(w_group).at[:, N_GROUPS:N_GROUPS + N_EXPERTS].set(w_expert)
    br = jnp.zeros((1, LANES), F32)
    br = br.at[0, 0:N_GROUPS].set(b_group).at[0, N_GROUPS:N_GROUPS + N_EXPERTS].set(b_expert)
    w_hi = w.astype(BF16)
    w_lo = (w - w_hi.astype(F32)).astype(BF16)
    return w_hi, w_lo, br


def kernel(x, mem, positions, a_w_in, a_lambda, a_subln_g, a_w_out, b_w_in, b_q_norm_g, b_kv_norm_g,
           b_w_uq, b_w_qidx, b_w_uk, b_w_uv, b_w_out, mem_w_kv, xa_w_q, xa_w_out,
           moe_w_group, moe_b_group, moe_w_expert, moe_b_expert, moe_w_gate, moe_w_up, moe_w_down,
           ln_g, ln_b):
    bsz, seq, d = x.shape
    n_mem = mem.shape[1]
    depth = ln_g.shape[0]
    alpha = (2.0 * depth) ** 0.25
    assert seq % ROW_BLOCK == 0 and seq % KEY_TILE == 0
    assert ROW_BLOCK == ATTN_BLOCK and ROW_BLOCK == KEY_TILE and QUERY_BLOCK == 2 * CHUNK

    tabs_d = _rope_lane_tables(positions, DIFF_ROPE_DIM, DIFF_HEAD_DIM, True)
    tabs_r = _rope_lane_tables(positions, MLA_ROPE_DIM, LANES, False)
    tabs_i = _rope_lane_tables(positions, IDX_ROPE_DIM, IDX_DIM, True)

    memkv = _mem_kv(mem.reshape(bsz * n_mem, d), mem_w_kv.astype(BF16))

    h = x.reshape(bsz * seq, d)
    for i in range(depth):
        j = i // N_MIXERS
        if i % N_MIXERS == 0:
            qt, k, vt = _diff_qkv(h, a_w_in[j].astype(BF16), tabs_d)
            mix = _diff_attn(qt, k, vt, a_lambda[j], a_subln_g[j], bsz, seq, _diff_lambda_init(i))
            w_mix_out = a_w_out[j]
        else:
            w_ukt = jnp.pad(jnp.swapaxes(b_w_uk[j], 1, 2), ((0, 0), (MLA_ROPE_DIM, 0), (0, 0))).astype(BF16)
            qft, kv, kvt, qit, ki, wit = _dsa_proj(h, _pad_dsa_w_in(b_w_in[j]), b_q_norm_g[j], b_kv_norm_g[j],
                                                   b_w_uq[j].astype(BF16), b_w_qidx[j].astype(BF16), w_ukt,
                                                   tabs_r, tabs_i)
            w_uvt = jnp.swapaxes(b_w_uv[j], 1, 2).astype(BF16)
            mix = _dsa_attn(qft, kv, kvt, qit, ki, wit, w_uvt, bsz, seq)
            w_mix_out = b_w_out[j]
        h = _proj_res_ln(mix, w_mix_out.astype(BF16), h, ln_g[i, 0], ln_b[i, 0], alpha)
        h = _cross_attn(h, xa_w_q[i].astype(BF16), memkv, xa_w_out[i].astype(BF16),
                        ln_g[i, 1], ln_b[i, 1], alpha, seq, n_mem)
        wr_hi, wr_lo, br = _router_weights(moe_w_group[i], moe_b_group[i], moe_w_expert[i], moe_b_expert[i])
        h = _moe(h, wr_hi, wr_lo, br, moe_w_gate[i].astype(BF16), moe_w_up[i].astype(BF16),
                 moe_w_down[i].astype(BF16), ln_g[i, 2], ln_b[i, 2], alpha)
    return h.reshape(bsz, seq, d)
```

```python
import functools

import jax
import jax.numpy as jnp
from jax import lax
from jax.experimental import pallas as pl
from jax.experimental.pallas import tpu as pltpu

F32 = jnp.float32
BF16 = jnp.bfloat16
I32 = jnp.int32

CHUNK = 64
ROPE_THETA = 500000.0
LN_EPS = 1e-5
N_MIXERS = 2
DIFF_HEADS = 8
DIFF_HEAD_DIM = 64
DIFF_ROPE_DIM = DIFF_HEAD_DIM // 4
MLA_HEADS = 8
MLA_Q_RANK = 256
MLA_KV_RANK = 256
MLA_ROPE_DIM = 32
MLA_NOPE_DIM = 96
MLA_V_DIM = 128
IDX_HEADS = 16
IDX_DIM = 64
IDX_ROPE_DIM = IDX_DIM // 4
IDX_TOPK_MAX = 256
MEM_HEADS = 4
N_GROUPS = 4
EXPERTS_PER_GROUP = 4
N_EXPERTS = N_GROUPS * EXPERTS_PER_GROUP

LANES = 128
VMEM_LIMIT_BYTES = 56 * 1024 * 1024

ROW_BLOCK = 512
ATTN_BLOCK = 512
KEY_TILE = 512
QUERY_BLOCK = 128
HEADS_PER_GROUP = 4
BISECT_STEPS_PER_CHECK = 4
MLA_FEAT = MLA_KV_RANK + LANES

NEG_INF = float("-inf")
NEG_INF_KEY = -2139095041

_NT = (((1,), (1,)), ((), ()))


def _params(n_axes):
    return pltpu.CompilerParams(dimension_semantics=("arbitrary",) * n_axes,
                                vmem_limit_bytes=VMEM_LIMIT_BYTES)


def _dot(a, b):
    return jnp.dot(a, b, preferred_element_type=F32)


def _dot_nt(a, b):
    return lax.dot_general(a, b, _NT, preferred_element_type=F32)


def _layer_norm(z, g, b):
    mu = jnp.mean(z, axis=-1, keepdims=True)
    zc = z - mu
    var = jnp.mean(zc * zc, axis=-1, keepdims=True)
    return zc * lax.rsqrt(var + LN_EPS) * g + b


def _rms_norm(x, g):
    return x * lax.rsqrt(jnp.mean(x * x, axis=-1, keepdims=True) + LN_EPS) * g


def _rope(y, c, s_up, s_dn, half):
    return (y * c + pltpu.roll(y, LANES - half, 1) * s_up + pltpu.roll(y, half, 1) * s_dn)


def _rope_lane_tables(positions, rot_dim, period, keep_rest):
    half = rot_dim // 2
    inv_freq = ROPE_THETA ** (-jnp.arange(0, rot_dim, 2, dtype=F32) / rot_dim)
    ang = positions.astype(F32)[..., None] * inv_freq
    cos, sin = jnp.cos(ang), jnp.sin(ang)
    lead = positions.shape
    rest = jnp.full(lead + (period - rot_dim,), 1.0 if keep_rest else 0.0, F32)
    zrest = jnp.zeros(lead + (period - rot_dim,), F32)
    zhalf = jnp.zeros(lead + (half,), F32)
    reps = LANES // period
    out = []
    for parts in ((cos, cos, rest), (-sin, zhalf, zrest), (zhalf, sin, zrest)):
        t = jnp.concatenate(parts, axis=-1)
        out.append(jnp.tile(t, (1,) * len(lead) + (reps,)).reshape(-1, LANES))
    return out


LOG2E = 1.4426950408889634


def _diff_qkv_kernel(h_ref, w_ref, c_ref, su_ref, sd_ref, qt_ref, k_ref, vt_ref, *, q_scale, half):
    x = h_ref[...].astype(BF16)
    c, su, sd = c_ref[...], su_ref[...], sd_ref[...]
    n_heads = qt_ref.shape[0]
    for j2 in range(3 * n_heads // 2):
        y2 = _dot(x, w_ref[:, j2 * 2 * LANES:(j2 + 1) * 2 * LANES])
        for s in range(2):
            j = 2 * j2 + s
            y = y2[:, s * LANES:(s + 1) * LANES]
            if j < n_heads:
                qt_ref[j, 0] = (_rope(y, c, su, sd, half) * q_scale).T.astype(BF16)
            elif j < 2 * n_heads:
                k_ref[:, (j - n_heads) * LANES:(j - n_heads + 1) * LANES] = _rope(y, c, su, sd, half).astype(BF16)
            else:
                vt_ref[j - 2 * n_heads, 0] = y.T.astype(BF16)


def _diff_qkv(h2, w_in, tabs):
    t, d = h2.shape
    hd = 2 * DIFF_HEAD_DIM
    nb = t // ROW_BLOCK
    kern = functools.partial(_diff_qkv_kernel, q_scale=DIFF_HEAD_DIM ** -0.5 * LOG2E, half=DIFF_ROPE_DIM // 2)
    row = lambda i: (i, 0)
    full = lambda i: (0, 0)
    t_shape = jax.ShapeDtypeStruct((DIFF_HEADS, nb, hd, ROW_BLOCK), BF16)
    t_spec = pl.BlockSpec((DIFF_HEADS, 1, hd, ROW_BLOCK), lambda i: (0, i, 0, 0))
    return pl.pallas_call(
        kern, grid=(nb,),
        in_specs=[pl.BlockSpec((ROW_BLOCK, d), row), pl.BlockSpec(w_in.shape, full)]
        + [pl.BlockSpec((ROW_BLOCK, LANES), row)] * 3,
        out_specs=[t_spec, pl.BlockSpec((ROW_BLOCK, DIFF_HEADS * hd), row), t_spec],
        out_shape=[t_shape, jax.ShapeDtypeStruct((t, DIFF_HEADS * hd), BF16), t_shape],
        compiler_params=_params(1), name="diff_qkv",
    )(h2, w_in, *tabs)


def _diff_attn_kernel(lam_ref, g_ref, qt_ref, k_ref, vt_ref, o_ref, m_ref, l_ref, acc_ref, *,
                      blk, lambda_init):
    i = pl.program_id(2)
    lam = lam_ref[...]
    lam_full = (jnp.exp(jnp.sum(lam[0:1] * lam[1:2], axis=1, keepdims=True))
                - jnp.exp(jnp.sum(lam[2:3] * lam[3:4], axis=1, keepdims=True)) + lambda_init)
    qt = qt_ref[0, 0]
    feat = lax.broadcasted_iota(I32, qt.shape, 0)
    zero = jnp.zeros_like(qt)
    q_maps = (jnp.where(feat < DIFF_HEAD_DIM, qt, zero), jnp.where(feat >= DIFF_HEAD_DIM, qt, zero))
    m_ref[...] = jnp.full(m_ref.shape, NEG_INF, F32)
    l_ref[...] = jnp.zeros(l_ref.shape, F32)
    acc_ref[...] = jnp.zeros(acc_ref.shape, F32)

    def scores(j, c, masked):
        k = k_ref[pl.ds(pl.multiple_of(j * blk, blk), blk), :]
        s = _dot(k, q_maps[c])
        if masked:
            kc = lax.broadcasted_iota(I32, s.shape, 0) // CHUNK
            qc = lax.broadcasted_iota(I32, s.shape, 1) // CHUNK
            s = jnp.where(kc <= qc, s, NEG_INF)
        return s

    def softmax_pv(j, c, s):
        m_old = m_ref[c]
        m_new = jnp.maximum(m_old, jnp.max(s, axis=0, keepdims=True))
        p = jnp.exp2(s - m_new)
        alpha = jnp.exp2(m_old - m_new)
        l_ref[c] = alpha * l_ref[c] + jnp.sum(p, axis=0, keepdims=True)
        acc_ref[c] = alpha * acc_ref[c] + _dot(vt_ref[0, j], p.astype(BF16))
        m_ref[c] = m_new

    def run(blocks):
        items = [(j, c, masked) for (j, masked) in blocks for c in range(2)]
        ahead = 2
        pending = {n: scores(*items[n]) for n in range(min(ahead, len(items)))}
        for n, (j, c, _) in enumerate(items):
            if n + ahead < len(items):
                pending[n + ahead] = scores(*items[n + ahead])
            softmax_pv(j, c, pending.pop(n))

    def full_pair(jj, carry):
        run([(2 * jj, False), (2 * jj + 1, False)])
        return carry

    lax.fori_loop(0, i // 2, full_pair, 0)

    @pl.when(i % 2 == 1)
    def _():
        run([(i - 1, False), (i, True)])

    @pl.when(i % 2 == 0)
    def _():
        run([(i, True)])

    ot = acc_ref[0] * (1.0 / l_ref[0]) - lam_full * (acc_ref[1] * (1.0 / l_ref[1]))
    ot = ot * lax.rsqrt(jnp.mean(ot * ot, axis=0, keepdims=True) + LN_EPS) * (1.0 - lambda_init)
    g = g_ref[...]
    for s in range(blk // LANES):
        o_ref[s * LANES:(s + 1) * LANES, :] = (ot[:, s * LANES:(s + 1) * LANES] * g).T.astype(BF16)


def _diff_attn(qt, k, vt, lam, subln_g, bsz, seq, lambda_init):
    blk = ATTN_BLOCK
    nq = seq // blk
    hd = 2 * DIFF_HEAD_DIM
    kern = functools.partial(_diff_attn_kernel, blk=blk, lambda_init=lambda_init)
    g = jnp.broadcast_to(subln_g.astype(F32)[:, None], (hd, LANES))
    return pl.pallas_call(
        kern, grid=(bsz, DIFF_HEADS, nq),
        in_specs=[
            pl.BlockSpec(lam.shape, lambda b, h, i: (0, 0)),
            pl.BlockSpec((hd, LANES), lambda b, h, i: (0, 0)),
            pl.BlockSpec((1, 1, hd, blk), lambda b, h, i: (h, b * nq + i, 0, 0)),
            pl.BlockSpec((seq, hd), lambda b, h, i: (b, h)),
            pl.BlockSpec((1, nq, hd, blk), lambda b, h, i: (h, b, 0, 0)),
        ],
        out_specs=pl.BlockSpec((blk, hd), lambda b, h, i: (b * nq + i, h)),
        out_shape=jax.ShapeDtypeStruct((bsz * seq, DIFF_HEADS * hd), BF16),
        scratch_shapes=[pltpu.VMEM((2, 1, blk), F32), pltpu.VMEM((2, 1, blk), F32),
                        pltpu.VMEM((2, hd, blk), F32)],
        compiler_params=_params(3), name="diff_attn",
    )(lam, g, qt, k, vt)


def _proj_res_ln_kernel(a_ref, w_ref, h_ref, g_ref, b_ref, o_ref, *, alpha):
    z = alpha * h_ref[...] + _dot(a_ref[...], w_ref[...])
    o_ref[...] = _layer_norm(z, g_ref[...], b_ref[...])


def _proj_res_ln(a, w, h2, g, b, alpha):
    t, d = h2.shape
    k = a.shape[1]
    row = lambda i: (i, 0)
    full = lambda i: (0, 0)
    return pl.pallas_call(
        functools.partial(_proj_res_ln_kernel, alpha=alpha), grid=(t // ROW_BLOCK,),
        in_specs=[pl.BlockSpec((ROW_BLOCK, k), row), pl.BlockSpec((k, d), full),
                  pl.BlockSpec((ROW_BLOCK, d), row), pl.BlockSpec((1, d), full),
                  pl.BlockSpec((1, d), full)],
        out_specs=pl.BlockSpec((ROW_BLOCK, d), row),
        out_shape=jax.ShapeDtypeStruct((t, d), F32),
        compiler_params=_params(1), name="proj_res_ln",
    )(a, w, h2, g.reshape(1, d), b.reshape(1, d))


def _mem_kv_kernel(m_ref, w_ref, o_ref):
    o_ref[...] = _dot(m_ref[...].astype(BF16), w_ref[...]).astype(BF16)


def _mem_kv(mem2, w_kv):
    rows, d = mem2.shape
    n = w_kv.shape[1]
    blk = min(ROW_BLOCK, rows)
    return pl.pallas_call(
        _mem_kv_kernel, grid=(rows // blk,),
        in_specs=[pl.BlockSpec((blk, d), lambda i: (i, 0)), pl.BlockSpec((d, n), lambda i: (0, 0))],
        out_specs=pl.BlockSpec((blk, n), lambda i: (i, 0)),
        out_shape=jax.ShapeDtypeStruct((rows, n), BF16),
        compiler_params=_params(1), name="mem_kv",
    )(mem2, w_kv)


def _cross_attn_kernel(h_ref, wq_ref, kv_ref, wo_ref, g_ref, b_ref, o_ref, *, alpha, q_scale):
    h = h_ref[...]
    d = h.shape[1]
    hd = d // MEM_HEADS
    q = (_dot(h.astype(BF16), wq_ref[...]) * q_scale).astype(BF16)
    outs = []
    for hh in range(MEM_HEADS):
        s = _dot_nt(q[:, hh * hd:(hh + 1) * hd], kv_ref[:, hh * hd:(hh + 1) * hd])
        p = jnp.exp(s - jnp.max(s, axis=1, keepdims=True))
        p = p * (1.0 / jnp.sum(p, axis=1, keepdims=True))
        outs.append(_dot(p.astype(BF16), kv_ref[:, d + hh * hd:d + (hh + 1) * hd]).astype(BF16))
    o = jnp.concatenate(outs, axis=1)
    z = alpha * h + _dot(o, wo_ref[...])
    o_ref[...] = _layer_norm(z, g_ref[...], b_ref[...])


def _cross_attn(h2, w_q, memkv, w_out, g, b, alpha, seq, n_mem):
    t, d = h2.shape
    per_batch = seq // ROW_BLOCK
    row = lambda i: (i, 0)
    full = lambda i: (0, 0)
    kern = functools.partial(_cross_attn_kernel, alpha=alpha, q_scale=(d // MEM_HEADS) ** -0.5)
    return pl.pallas_call(
        kern, grid=(t // ROW_BLOCK,),
        in_specs=[pl.BlockSpec((ROW_BLOCK, d), row), pl.BlockSpec((d, d), full),
                  pl.BlockSpec((n_mem, 2 * d), lambda i: (i // per_batch, 0)),
                  pl.BlockSpec((d, d), full), pl.BlockSpec((1, d), full), pl.BlockSpec((1, d), full)],
        out_specs=pl.BlockSpec((ROW_BLOCK, d), row),
        out_shape=jax.ShapeDtypeStruct((t, d), F32),
        compiler_params=_params(1), name="cross_attn",
    )(h2, w_q, memkv, w_out, g.reshape(1, d), b.reshape(1, d))


def _route(h, wr_hi, wr_lo, br):
    h_hi = h.astype(BF16)
    h_lo = (h - h_hi.astype(F32)).astype(BF16)
    logits = _dot(h_hi, wr_hi) + _dot(h_hi, wr_lo) + _dot(h_lo, wr_hi) + br
    lane = lax.broadcasted_iota(I32, logits.shape, 1).astype(F32)
    gl = jnp.where(lane < N_GROUPS, logits, NEG_INF)
    gmax = jnp.max(gl, axis=1, keepdims=True)
    g_sel = jnp.min(jnp.where(gl == gmax, lane, float(LANES)), axis=1, keepdims=True)
    g_gate = 1.0 / jnp.sum(jnp.exp(gl - gmax), axis=1, keepdims=True)
    first = N_GROUPS + g_sel * EXPERTS_PER_GROUP
    el = jnp.where((lane >= first) & (lane < first + EXPERTS_PER_GROUP), logits, NEG_INF)
    v1 = jnp.max(el, axis=1, keepdims=True)
    i1 = jnp.min(jnp.where(el == v1, lane, float(LANES)), axis=1, keepdims=True)
    el2 = jnp.where(lane == i1, NEG_INF, el)
    v2 = jnp.max(el2, axis=1, keepdims=True)
    i2 = jnp.min(jnp.where(el2 == v2, lane, float(LANES)), axis=1, keepdims=True)
    r = jnp.exp(v2 - v1)
    inv = g_gate / (1.0 + r)
    return jnp.where(lane == i1, inv, 0.0) + jnp.where(lane == i2, inv * r, 0.0)


def _moe_kernel(h_ref, wrh_ref, wrl_ref, br_ref, wg_ref, wu_ref, wd_ref, g_ref, b_ref, o_ref,
                xb_ref, comb_ref, acc_ref, *, alpha):
    e = pl.program_id(1)

    @pl.when(e == 0)
    def _():
        h = h_ref[...]
        comb_ref[...] = _route(h, wrh_ref[...], wrl_ref[...], br_ref[...])
        xb_ref[...] = h.astype(BF16)
        acc_ref[...] = jnp.zeros(acc_ref.shape, F32)

    x = xb_ref[...]
    comb = comb_ref[...]
    lane = lax.broadcasted_iota(I32, comb.shape, 1)
    c = jnp.sum(jnp.where(lane == N_GROUPS + e, comb, 0.0), axis=1, keepdims=True)
    a = _dot(x, wg_ref[0])
    u = _dot(x, wu_ref[0])
    hid = a * (1.0 / (1.0 + jnp.exp(-a))) * u
    acc_ref[...] += _dot((c * hid).astype(BF16), wd_ref[0])

    @pl.when(e == pl.num_programs(1) - 1)
    def _():
        z = alpha * h_ref[...] + acc_ref[...]
        o_ref[...] = _layer_norm(z, g_ref[...], b_ref[...])


def _moe(h2, wr_hi, wr_lo, br, w_gate, w_up, w_down, g, b, alpha):
    t, d = h2.shape
    n_e, _, ff = w_gate.shape
    row = lambda i, e: (i, 0)
    full = lambda i, e: (0, 0)
    return pl.pallas_call(
        functools.partial(_moe_kernel, alpha=alpha), grid=(t // ROW_BLOCK, n_e),
        in_specs=[pl.BlockSpec((ROW_BLOCK, d), row), pl.BlockSpec((d, LANES), full),
                  pl.BlockSpec((d, LANES), full), pl.BlockSpec((1, LANES), full),
                  pl.BlockSpec((1, d, ff), lambda i, e: (e, 0, 0)),
                  pl.BlockSpec((1, d, ff), lambda i, e: (e, 0, 0)),
                  pl.BlockSpec((1, ff, d), lambda i, e: (e, 0, 0)),
                  pl.BlockSpec((1, d), full), pl.BlockSpec((1, d), full)],
        out_specs=pl.BlockSpec((ROW_BLOCK, d), row),
        out_shape=jax.ShapeDtypeStruct((t, d), F32),
        scratch_shapes=[pltpu.VMEM((ROW_BLOCK, d), BF16), pltpu.VMEM((ROW_BLOCK, LANES), F32),
                        pltpu.VMEM((ROW_BLOCK, d), F32)],
        compiler_params=_params(2), name="moe",
    )(h2, wr_hi, wr_lo, br, w_gate, w_up, w_down, g.reshape(1, d), b.reshape(1, d))


_COL_CQ, _COL_CKV, _COL_KROPE, _COL_KIDX, _COL_WIDX, _COL_END = 0, 256, 512, 640, 768, 896


def _dsa_proj_kernel(h_ref, win_ref, gq_ref, gkv_ref, wuq_ref, wqi_ref, wuk_ref,
                     cr_ref, sur_ref, sdr_ref, ci_ref, sui_ref, sdi_ref,
                     qft_ref, kv_ref, kvt_ref, qit_ref, ki_ref, wit_ref, *, q_scale, w_scale):
    y = _dot(h_ref[...].astype(BF16), win_ref[...])
    cr, sur, sdr = cr_ref[...], sur_ref[...], sdr_ref[...]
    ci, sui, sdi = ci_ref[...], sui_ref[...], sdi_ref[...]
    half_r, half_i = MLA_ROPE_DIM // 2, IDX_ROPE_DIM // 2
    qb = QUERY_BLOCK
    n_qb = y.shape[0] // qb
    c_q = _rms_norm(y[:, _COL_CQ:_COL_CKV], gq_ref[...]).astype(BF16)
    c_kv = _rms_norm(y[:, _COL_CKV:_COL_KROPE], gkv_ref[...])
    kv_ref[:, 0:MLA_KV_RANK] = c_kv.astype(BF16)
    kvt_ref[0] = c_kv.T.astype(BF16)
    kv_ref[:, MLA_KV_RANK:MLA_FEAT] = _rope(y[:, _COL_KROPE:_COL_KIDX], cr, sur, sdr, half_r).astype(BF16)
    ki_ref[...] = _rope(y[:, _COL_KIDX:_COL_WIDX], ci, sui, sdi, half_i).astype(BF16)
    wit_ref[...] = (y[:, _COL_WIDX:_COL_END] * w_scale).T[0:IDX_HEADS, :]
    q = _dot(c_q, wuq_ref[...])
    q_bf = q.astype(BF16)
    hd = MLA_ROPE_DIM + MLA_NOPE_DIM
    for hh in range(MLA_HEADS):
        lat_t = (_dot(q_bf[:, hh * hd:(hh + 1) * hd], wuk_ref[hh]) * q_scale).T.astype(BF16)
        rope_t = (_rope(q[:, hh * hd:(hh + 1) * hd], cr, sur, sdr, half_r) * q_scale).T.astype(BF16)
        for bl in range(n_qb):
            qft_ref[bl, 0:MLA_KV_RANK, hh * qb:(hh + 1) * qb] = lat_t[:, bl * qb:(bl + 1) * qb]
            qft_ref[bl, MLA_KV_RANK:MLA_FEAT, hh * qb:(hh + 1) * qb] = rope_t[:, bl * qb:(bl + 1) * qb]
    qi = _dot(c_q, wqi_ref[...])
    for p in range(IDX_HEADS // 2):
        pair_t = _rope(qi[:, p * LANES:(p + 1) * LANES], ci, sui, sdi, half_i).T.astype(BF16)
        for bl in range(n_qb):
            qit_ref[bl, p] = pair_t[:, bl * qb:(bl + 1) * qb]


def _dsa_proj(h2, w_in_p, gq, gkv, w_uq, w_qidx, w_ukt, tabs_r, tabs_i):
    t, d = h2.shape
    row = lambda i: (i, 0)
    full = lambda i: (0, 0)
    full3 = lambda i: (0, 0, 0)
    n_pairs = IDX_HEADS // 2
    qb = QUERY_BLOCK
    n_qb = ROW_BLOCK // qb
    kern = functools.partial(_dsa_proj_kernel, q_scale=(MLA_ROPE_DIM + MLA_NOPE_DIM) ** -0.5 * LOG2E,
                             w_scale=(IDX_HEADS * IDX_DIM) ** -0.5)
    tab = pl.BlockSpec((ROW_BLOCK, LANES), row)
    return pl.pallas_call(
        kern, grid=(t // ROW_BLOCK,),
        in_specs=[pl.BlockSpec((ROW_BLOCK, d), row), pl.BlockSpec(w_in_p.shape, full),
                  pl.BlockSpec((1, MLA_Q_RANK), full), pl.BlockSpec((1, MLA_KV_RANK), full),
                  pl.BlockSpec(w_uq.shape, full), pl.BlockSpec(w_qidx.shape, full),
                  pl.BlockSpec(w_ukt.shape, full3)] + [tab] * 6,
        out_specs=[pl.BlockSpec((n_qb, MLA_FEAT, MLA_HEADS * qb), lambda i: (i, 0, 0)),
                   pl.BlockSpec((ROW_BLOCK, MLA_FEAT), row),
                   pl.BlockSpec((1, MLA_KV_RANK, ROW_BLOCK), lambda i: (i, 0, 0)),
                   pl.BlockSpec((n_qb, n_pairs, LANES, qb), lambda i: (i, 0, 0, 0)),
                   pl.BlockSpec((ROW_BLOCK, LANES), row),
                   pl.BlockSpec((IDX_HEADS, ROW_BLOCK), lambda i: (0, i))],
        out_shape=[jax.ShapeDtypeStruct((t // qb, MLA_FEAT, MLA_HEADS * qb), BF16),
                   jax.ShapeDtypeStruct((t, MLA_FEAT), BF16),
                   jax.ShapeDtypeStruct((t // ROW_BLOCK, MLA_KV_RANK, ROW_BLOCK), BF16),
                   jax.ShapeDtypeStruct((t // qb, n_pairs, LANES, qb), BF16),
                   jax.ShapeDtypeStruct((t, LANES), BF16),
                   jax.ShapeDtypeStruct((IDX_HEADS, t), F32)],
        compiler_params=_params(1), name="dsa_proj",
    )(h2, w_in_p, gq.reshape(1, -1), gkv.reshape(1, -1), w_uq, w_qidx, w_ukt, *tabs_r, *tabs_i)


def _sortable_key(x):
    bits = lax.bitcast_convert_type(x, I32)
    return bits ^ ((bits >> 31) & 0x7FFFFFFF)


def _dsa_attn_kernel(qit_ref, wit_ref, qft_ref, ki_ref, kv_ref, kvt_ref, wuvt_ref, o_ref,
                     key_ref, thr_ref, m_ref, l_ref, acc_ref, *, kt, top_k):
    g = pl.program_id(1)
    qb = QUERY_BLOCK
    n_tiles = ((g + 1) * qb + kt - 1) // kt
    n_pairs = IDX_HEADS // 2
    lane_q = lax.broadcasted_iota(I32, (1, qb), 1)
    n_allowed = (g * (qb // CHUNK) + 1 + lane_q // CHUNK) * CHUNK

    feat = lax.broadcasted_iota(I32, (LANES, qb), 0)
    pair_w = []
    for p in range(n_pairs):
        slab = qit_ref[0, p]
        zero = jnp.zeros_like(slab)
        pair_w.append(jnp.concatenate([jnp.where(feat < IDX_DIM, slab, zero),
                                       jnp.where(feat >= IDX_DIM, slab, zero)], axis=1))
    wt = wit_ref[...]

    def score_tile(t, carry):
        lo, hi = carry
        k = ki_ref[pl.ds(pl.multiple_of(t * kt, kt), kt), :]
        sc = jnp.zeros((kt, qb), F32)
        for p in range(n_pairs):
            lg = _dot(k, pair_w[p])
            sc = (sc + jnp.maximum(lg[:, 0:qb], 0.0) * wt[2 * p:2 * p + 1, :]
                  + jnp.maximum(lg[:, qb:2 * qb], 0.0) * wt[2 * p + 1:2 * p + 2, :])
        kk = t * kt + lax.broadcasted_iota(I32, sc.shape, 0)
        valid = kk < n_allowed
        key_ref[t] = _sortable_key(jnp.where(valid, sc, NEG_INF))
        lo = jnp.minimum(lo, jnp.min(jnp.where(valid, sc, float("inf")), axis=0, keepdims=True))
        hi = jnp.maximum(hi, jnp.max(jnp.where(valid, sc, NEG_INF), axis=0, keepdims=True))
        return lo, hi

    lo_f, hi_f = lax.fori_loop(0, n_tiles, score_tile,
                               (jnp.full((1, qb), float("inf"), F32), jnp.full((1, qb), NEG_INF, F32)))

    keep_all = jnp.full((1, qb), NEG_INF_KEY + 1, I32)
    thr_ref[...] = keep_all

    @pl.when((g + 1) * qb > top_k)
    def _():
        def count_ge(mid):
            def body(t, cnt):
                hit = jnp.where(key_ref[t] >= mid, 1, 0)
                return cnt + jnp.sum(hit.reshape(kt // 8, 8, qb), axis=0)
            cnt = lax.fori_loop(0, n_tiles, body, jnp.zeros((8, qb), I32))
            return jnp.sum(cnt.astype(F32), axis=0, keepdims=True)

        def bisect(st):
            lo, hi = st
            for _ in range(BISECT_STEPS_PER_CHECK):
                mid = (lo >> 1) + (hi >> 1) + ((lo | hi) & 1)
                cnt = count_ge(mid)
                ge = cnt >= float(top_k)
                lo_next = jnp.where(ge, mid, lo)
                hi = jnp.where(cnt == float(top_k), mid, jnp.where(ge, hi, mid - 1))
                lo = lo_next
            return lo, hi

        def unresolved(st):
            lo, hi = st
            return jnp.max(jnp.where(hi > lo, 1.0, 0.0)) > 0.0

        lo, _ = lax.while_loop(unresolved, bisect, (_sortable_key(lo_f), _sortable_key(hi_f)))
        thr_ref[...] = jnp.where(n_allowed > top_k, lo, keep_all)

    thr = thr_ref[...]
    m_ref[...] = jnp.full(m_ref.shape, NEG_INF, F32)
    l_ref[...] = jnp.zeros(l_ref.shape, F32)
    acc_ref[...] = jnp.zeros(acc_ref.shape, F32)
    hpg = HEADS_PER_GROUP
    group = hpg * qb

    def scores(t, gi):
        kv_rows = kv_ref[pl.ds(pl.multiple_of(t * kt, kt), kt), :]
        bias = jnp.where(key_ref[t] >= thr, 0.0, NEG_INF)
        bias_g = jnp.concatenate([bias] * hpg, axis=1)
        return _dot(kv_rows, qft_ref[0, :, gi * group:(gi + 1) * group]) + bias_g

    def softmax_pv(t, gi, s):
        m_old = m_ref[gi]
        m_new = jnp.maximum(m_old, jnp.max(s, axis=0, keepdims=True))
        m_safe = jnp.where(m_new == NEG_INF, 0.0, m_new)
        p = jnp.exp2(s - m_safe)
        alpha = jnp.exp2(m_old - m_safe)
        l_ref[gi] = alpha * l_ref[gi] + jnp.sum(p, axis=0, keepdims=True)
        acc_ref[gi] = alpha * acc_ref[gi] + _dot(kvt_ref[t], p.astype(BF16))
        m_ref[gi] = m_new

    def run(tiles):
        items = [(t, gi) for t in tiles for gi in range(MLA_HEADS // hpg)]
        ahead = 2
        pending = {n: scores(*items[n]) for n in range(min(ahead, len(items)))}
        for n, item in enumerate(items):
            if n + ahead < len(items):
                pending[n + ahead] = scores(*items[n + ahead])
            softmax_pv(*item, pending.pop(n))

    def tile_pair(tt, carry):
        run([2 * tt, 2 * tt + 1])
        return carry

    lax.fori_loop(0, n_tiles // 2, tile_pair, 0)

    @pl.when(n_tiles % 2 == 1)
    def _():
        run([n_tiles - 1])

    for gi in range(MLA_HEADS // hpg):
        o_lat_t = (acc_ref[gi] * (1.0 / l_ref[gi])).astype(BF16)
        for hl in range(hpg):
            hh = gi * hpg + hl
            o_t = _dot(wuvt_ref[hh], o_lat_t[:, hl * qb:(hl + 1) * qb])
            o_ref[:, hh * MLA_V_DIM:(hh + 1) * MLA_V_DIM] = o_t.T.astype(BF16)


def _dsa_attn(qft, kv, kvt, qit, ki, wit, w_uvt, bsz, seq):
    kt = KEY_TILE
    qb = QUERY_BLOCK
    ng = seq // qb
    n_pairs = IDX_HEADS // 2
    top_k = min(IDX_TOPK_MAX, seq // 4)
    kern = functools.partial(_dsa_attn_kernel, kt=kt, top_k=top_k)
    batch2 = lambda b, g: (b, 0)
    n_groups, group = MLA_HEADS // HEADS_PER_GROUP, HEADS_PER_GROUP * qb
    return pl.pallas_call(
        kern, grid=(bsz, ng),
        in_specs=[pl.BlockSpec((1, n_pairs, LANES, qb), lambda b, g: (b * ng + g, 0, 0, 0)),
                  pl.BlockSpec((IDX_HEADS, qb), lambda b, g: (0, b * ng + g)),
                  pl.BlockSpec((1, MLA_FEAT, MLA_HEADS * qb), lambda b, g: (b * ng + g, 0, 0)),
                  pl.BlockSpec((seq, LANES), batch2),
                  pl.BlockSpec((seq, MLA_FEAT), batch2),
                  pl.BlockSpec((seq // kt, MLA_KV_RANK, kt), lambda b, g: (b, 0, 0)),
                  pl.BlockSpec(w_uvt.shape, lambda b, g: (0, 0, 0))],
        out_specs=pl.BlockSpec((qb, MLA_HEADS * MLA_V_DIM), lambda b, g: (b * ng + g, 0)),
        out_shape=jax.ShapeDtypeStruct((bsz * seq, MLA_HEADS * MLA_V_DIM), BF16),
        scratch_shapes=[pltpu.VMEM((seq // kt, kt, qb), I32), pltpu.VMEM((1, qb), I32),
                        pltpu.VMEM((n_groups, 1, group), F32), pltpu.VMEM((n_groups, 1, group), F32),
                        pltpu.VMEM((n_groups, MLA_KV_RANK, group), F32)],
        compiler_params=_params(2), name="dsa_attn",
    )(qit, wit, qft, ki, kv, kvt, w_uvt)


def _diff_lambda_init(layer_idx):
    import math
    return 0.8 - 0.6 * math.exp(-0.3 * layer_idx)


def _pad_dsa_w_in(w_in):
    d = w_in.shape[0]
    o_cq, o_ckv = 0, MLA_Q_RANK
    o_kr = o_ckv + MLA_KV_RANK
    o_ki = o_kr + MLA_ROPE_DIM
    o_wi = o_ki + IDX_DIM
    out = jnp.zeros((d, _COL_END), w_in.dtype)
    out = out.at[:, _COL_CQ:_COL_CQ + MLA_Q_RANK].set(w_in[:, o_cq:o_ckv])
    out = out.at[:, _COL_CKV:_COL_CKV + MLA_KV_RANK].set(w_in[:, o_ckv:o_kr])
    out = out.at[:, _COL_KROPE:_COL_KROPE + MLA_ROPE_DIM].set(w_in[:, o_kr:o_ki])
    out = out.at[:, _COL_KIDX:_COL_KIDX + IDX_DIM].set(w_in[:, o_ki:o_wi])
    out = out.at[:, _COL_KIDX + IDX_DIM:_COL_KIDX + 2 * IDX_DIM].set(w_in[:, o_ki:o_wi])
    out = out.at[:, _COL_WIDX:_COL_WIDX + IDX_HEADS].set(w_in[:, o_wi:o_wi + IDX_HEADS])
    return out.astype(BF16)


def _router_weights(w_group, b_group, w_expert, b_expert):
    d = w_group.shape[0]
    w = jnp.zeros((d, LANES), F32)
    w = w.at[:, 0:N_GROUPS].set(w_group).at[:, N_GROUPS:N_GROUPS + N_EXPERTS].set(w_expert)
    br = jnp.zeros((1, LANES), F32)
    br = br.at[0, 0:N_GROUPS].set(b_group).at[0, N_GROUPS:N_GROUPS + N_EXPERTS].set(b_expert)
    w_hi = w.astype(BF16)
    w_lo = (w - w_hi.astype(F32)).astype(BF16)
    return w_hi, w_lo, br


def kernel(x, mem, positions, a_w_in, a_lambda, a_subln_g, a_w_out, b_w_in, b_q_norm_g, b_kv_norm_g,
           b_w_uq, b_w_qidx, b_w_uk, b_w_uv, b_w_out, mem_w_kv, xa_w_q, xa_w_out,
           moe_w_group, moe_b_group, moe_w_expert, moe_b_expert, moe_w_gate, moe_w_up, moe_w_down,
           ln_g, ln_b):
    bsz, seq, d = x.shape
    n_mem = mem.shape[1]
    depth = ln_g.shape[0]
    alpha = (2.0 * depth) ** 0.25
    assert seq % ROW_BLOCK == 0 and seq % KEY_TILE == 0
    assert ROW_BLOCK == ATTN_BLOCK and ROW_BLOCK == KEY_TILE and QUERY_BLOCK == 2 * CHUNK

    tabs_d = _rope_lane_tables(positions, DIFF_ROPE_DIM, DIFF_HEAD_DIM, True)
    tabs_r = _rope_lane_tables(positions, MLA_ROPE_DIM, LANES, False)
    tabs_i = _rope_lane_tables(positions, IDX_ROPE_DIM, IDX_DIM, True)

    memkv = _mem_kv(mem.reshape(bsz * n_mem, d), mem_w_kv.astype(BF16))

    h = x.reshape(bsz * seq, d)
    for i in range(depth):
        j = i // N_MIXERS
        if i % N_MIXERS == 0:
            qt, k, vt = _diff_qkv(h, a_w_in[j].astype(BF16), tabs_d)
            mix = _diff_attn(qt, k, vt, a_lambda[j], a_subln_g[j], bsz, seq, _diff_lambda_init(i))
            w_mix_out = a_w_out[j]
        else:
            w_ukt = jnp.pad(jnp.swapaxes(b_w_uk[j], 1, 2), ((0, 0), (MLA_ROPE_DIM, 0), (0, 0))).astype(BF16)
            qft, kv, kvt, qit, ki, wit = _dsa_proj(h, _pad_dsa_w_in(b_w_in[j]), b_q_norm_g[j], b_kv_norm_g[j],
                                                   b_w_uq[j].astype(BF16), b_w_qidx[j].astype(BF16), w_ukt,
                                                   tabs_r, tabs_i)
            w_uvt = jnp.swapaxes(b_w_uv[j], 1, 2).astype(BF16)
            mix = _dsa_attn(qft, kv, kvt, qit, ki, wit, w_uvt, bsz, seq)
            w_mix_out = b_w_out[j]
        h = _proj_res_ln(mix, w_mix_out.astype(BF16), h, ln_g[i, 0], ln_b[i, 0], alpha)
        h = _cross_attn(h, xa_w_q[i].astype(BF16), memkv, xa_w_out[i].astype(BF16),
                        ln_g[i, 1], ln_b[i, 1], alpha, seq, n_mem)
        wr_hi, wr_lo, br = _router_weights(moe_w_group[i], moe_b_group[i], moe_w_expert[i], moe_b_expert[i])
        h = _moe(h, wr_hi, wr_lo, br, moe_w_gate[i].astype(BF16), moe_w_up[i].astype(BF16),
                 moe_w_down[i].astype(BF16), ln_g[i, 2], ln_b[i, 2], alpha)
    return h.reshape(bsz, seq, d)
```

```python
import functools

import jax
import jax.numpy as jnp
from jax import lax
from jax.experimental import pallas as pl
from jax.experimental.pallas import tpu as pltpu

F32 = jnp.float32
BF16 = jnp.bfloat16
I32 = jnp.int32

CHUNK = 64
ROPE_THETA = 500000.0
LN_EPS = 1e-5
N_MIXERS = 2
DIFF_HEADS = 8
DIFF_HEAD_DIM = 64
DIFF_ROPE_DIM = DIFF_HEAD_DIM // 4
MLA_HEADS = 8
MLA_Q_RANK = 256
MLA_KV_RANK = 256
MLA_ROPE_DIM = 32
MLA_NOPE_DIM = 96
MLA_V_DIM = 128
IDX_HEADS = 16
IDX_DIM = 64
IDX_ROPE_DIM = IDX_DIM // 4
IDX_TOPK_MAX = 256
MEM_HEADS = 4
N_GROUPS = 4
EXPERTS_PER_GROUP = 4
N_EXPERTS = N_GROUPS * EXPERTS_PER_GROUP

LANES = 128
VMEM_LIMIT_BYTES = 56 * 1024 * 1024

ROW_BLOCK = 512
ATTN_BLOCK = 512
KEY_TILE = 512
QUERY_BLOCK = 128
HEADS_PER_GROUP = 4
BISECT_STEPS_PER_CHECK = 4
MLA_FEAT = MLA_KV_RANK + LANES

NEG_INF = float("-inf")
NEG_INF_KEY = -2139095041

_NT = (((1,), (1,)), ((), ()))


def _params(n_axes):
    return pltpu.CompilerParams(dimension_semantics=("arbitrary",) * n_axes,
                                vmem_limit_bytes=VMEM_LIMIT_BYTES)


def _dot(a, b):
    return jnp.dot(a, b, preferred_element_type=F32)


def _dot_nt(a, b):
    return lax.dot_general(a, b, _NT, preferred_element_type=F32)


def _layer_norm(z, g, b):
    mu = jnp.mean(z, axis=-1, keepdims=True)
    zc = z - mu
    var = jnp.mean(zc * zc, axis=-1, keepdims=True)
    return zc * lax.rsqrt(var + LN_EPS) * g + b


def _rms_norm(x, g):
    return x * lax.rsqrt(jnp.mean(x * x, axis=-1, keepdims=True) + LN_EPS) * g


def _rope(y, c, s_up, s_dn, half):
    return (y * c + pltpu.roll(y, LANES - half, 1) * s_up + pltpu.roll(y, half, 1) * s_dn)


def _rope_lane_tables(positions, rot_dim, period, keep_rest):
    half = rot_dim // 2
    inv_freq = ROPE_THETA ** (-jnp.arange(0, rot_dim, 2, dtype=F32) / rot_dim)
    ang = positions.astype(F32)[..., None] * inv_freq
    cos, sin = jnp.cos(ang), jnp.sin(ang)
    lead = positions.shape
    rest = jnp.full(lead + (period - rot_dim,), 1.0 if keep_rest else 0.0, F32)
    zrest = jnp.zeros(lead + (period - rot_dim,), F32)
    zhalf = jnp.zeros(lead + (half,), F32)
    reps = LANES // period
    out = []
    for parts in ((cos, cos, rest), (-sin, zhalf, zrest), (zhalf, sin, zrest)):
        t = jnp.concatenate(parts, axis=-1)
        out.append(jnp.tile(t, (1,) * len(lead) + (reps,)).reshape(-1, LANES))
    return out


LOG2E = 1.4426950408889634


def _diff_qkv_kernel(h_ref, w_ref, c_ref, su_ref, sd_ref, qt_ref, k_ref, vt_ref, *, q_scale, half):
    x = h_ref[...].astype(BF16)
    c, su, sd = c_ref[...], su_ref[...], sd_ref[...]
    n_heads = qt_ref.shape[0]
    for j2 in range(3 * n_heads // 2):
        y2 = _dot(x, w_ref[:, j2 * 2 * LANES:(j2 + 1) * 2 * LANES])
        for s in range(2):
            j = 2 * j2 + s
            y = y2[:, s * LANES:(s + 1) * LANES]
            if j < n_heads:
                qt_ref[j, 0] = (_rope(y, c, su, sd, half) * q_scale).T.astype(BF16)
            elif j < 2 * n_heads:
                k_ref[:, (j - n_heads) * LANES:(j - n_heads + 1) * LANES] = _rope(y, c, su, sd, half).astype(BF16)
            else:
                vt_ref[j - 2 * n_heads, 0] = y.T.astype(BF16)


def _diff_qkv(h2, w_in, tabs):
    t, d = h2.shape
    hd = 2 * DIFF_HEAD_DIM
    nb = t // ROW_BLOCK
    kern = functools.partial(_diff_qkv_kernel, q_scale=DIFF_HEAD_DIM ** -0.5 * LOG2E, half=DIFF_ROPE_DIM // 2)
    row = lambda i: (i, 0)
    full = lambda i: (0, 0)
    t_shape = jax.ShapeDtypeStruct((DIFF_HEADS, nb, hd, ROW_BLOCK), BF16)
    t_spec = pl.BlockSpec((DIFF_HEADS, 1, hd, ROW_BLOCK), lambda i: (0, i, 0, 0))
    return pl.pallas_call(
        kern, grid=(nb,),
        in_specs=[pl.BlockSpec((ROW_BLOCK, d), row), pl.BlockSpec(w_in.shape, full)]
        + [pl.BlockSpec((ROW_BLOCK, LANES), row)] * 3,
        out_specs=[t_spec, pl.BlockSpec((ROW_BLOCK, DIFF_HEADS * hd), row), t_spec],
        out_shape=[t_shape, jax.ShapeDtypeStruct((t, DIFF_HEADS * hd), BF16), t_shape],
        compiler_params=_params(1), name="diff_qkv",
    )(h2, w_in, *tabs)


def _diff_attn_kernel(lam_ref, g_ref, qt_ref, k_ref, vt_ref, o_ref, m_ref, l_ref, acc_ref, *,
                      blk, lambda_init):
    i = pl.program_id(2)
    lam = lam_ref[...]
    lam_full = (jnp.exp(jnp.sum(lam[0:1] * lam[1:2], axis=1, keepdims=True))
                - jnp.exp(jnp.sum(lam[2:3] * lam[3:4], axis=1, keepdims=True)) + lambda_init)
    qt = qt_ref[0, 0]
    feat = lax.broadcasted_iota(I32, qt.shape, 0)
    zero = jnp.zeros_like(qt)
    q_maps = (jnp.where(feat < DIFF_HEAD_DIM, qt, zero), jnp.where(feat >= DIFF_HEAD_DIM, qt, zero))
    m_ref[...] = jnp.full(m_ref.shape, NEG_INF, F32)
    l_ref[...] = jnp.zeros(l_ref.shape, F32)
    acc_ref[...] = jnp.zeros(acc_ref.shape, F32)

    def scores(j, c, masked):
        k = k_ref[pl.ds(pl.multiple_of(j * blk, blk), blk), :]
        s = _dot(k, q_maps[c])
        if masked:
            kc = lax.broadcasted_iota(I32, s.shape, 0) // CHUNK
            qc = lax.broadcasted_iota(I32, s.shape, 1) // CHUNK
            s = jnp.where(kc <= qc, s, NEG_INF)
        return s

    def softmax_pv(j, c, s):
        m_old = m_ref[c]
        m_new = jnp.maximum(m_old, jnp.max(s, axis=0, keepdims=True))
        p = jnp.exp2(s - m_new)
        alpha = jnp.exp2(m_old - m_new)
        l_ref[c] = alpha * l_ref[c] + jnp.sum(p, axis=0, keepdims=True)
        acc_ref[c] = alpha * acc_ref[c] + _dot(vt_ref[0, j], p.astype(BF16))
        m_ref[c] = m_new

    def run(blocks):
        items = [(j, c, masked) for (j, masked) in blocks for c in range(2)]
        ahead = 2
        pending = {n: scores(*items[n]) for n in range(min(ahead, len(items)))}
        for n, (j, c, _) in enumerate(items):
            if n + ahead < len(items):
                pending[n + ahead] = scores(*items[n + ahead])
            softmax_pv(j, c, pending.pop(n))

    def full_pair(jj, carry):
        run([(2 * jj, False), (2 * jj + 1, False)])
        return carry

    lax.fori_loop(0, i // 2, full_pair, 0)

    @pl.when(i % 2 == 1)
    def _():
        run([(i - 1, False), (i, True)])

    @pl.when(i % 2 == 0)
    def _():
        run([(i, True)])

    ot = acc_ref[0] * (1.0 / l_ref[0]) - lam_full * (acc_ref[1] * (1.0 / l_ref[1]))
    ot = ot * lax.rsqrt(jnp.mean(ot * ot, axis=0, keepdims=True) + LN_EPS) * (1.0 - lambda_init)
    g = g_ref[...]
    for s in range(blk // LANES):
        o_ref[s * LANES:(s + 1) * LANES, :] = (ot[:, s * LANES:(s + 1) * LANES] * g).T.astype(BF16)


def _diff_attn(qt, k, vt, lam, subln_g, bsz, seq, lambda_init):
    blk = ATTN_BLOCK
    nq = seq // blk
    hd = 2 * DIFF_HEAD_DIM
    kern = functools.partial(_diff_attn_kernel, blk=blk, lambda_init=lambda_init)
    g = jnp.broadcast_to(subln_g.astype(F32)[:, None], (hd, LANES))
    return pl.pallas_call(
        kern, grid=(bsz, DIFF_HEADS, nq),
        in_specs=[
            pl.BlockSpec(lam.shape, lambda b, h, i: (0, 0)),
            pl.BlockSpec((hd, LANES), lambda b, h, i: (0, 0)),
            pl.BlockSpec((1, 1, hd, blk), lambda b, h, i: (h, b * nq + i, 0, 0)),
            pl.BlockSpec((seq, hd), lambda b, h, i: (b, h)),
            pl.BlockSpec((1, nq, hd, blk), lambda b, h, i: (h, b, 0, 0)),
        ],
        out_specs=pl.BlockSpec((blk, hd), lambda b, h, i: (b * nq + i, h)),
        out_shape=jax.ShapeDtypeStruct((bsz * seq, DIFF_HEADS * hd), BF16),
        scratch_shapes=[pltpu.VMEM((2, 1, blk), F32), pltpu.VMEM((2, 1, blk), F32),
                        pltpu.VMEM((2, hd, blk), F32)],
        compiler_params=_params(3), name="diff_attn",
    )(lam, g, qt, k, vt)


def _proj_res_ln_kernel(a_ref, w_ref, h_ref, g_ref, b_ref, o_ref, *, alpha):
    z = alpha * h_ref[...] + _dot(a_ref[...], w_ref[...])
    o_ref[...] = _layer_norm(z, g_ref[...], b_ref[...])


def _proj_res_ln(a, w, h2, g, b, alpha):
    t, d = h2.shape
    k = a.shape[1]
    row = lambda i: (i, 0)
    full = lambda i: (0, 0)
    return pl.pallas_call(
        functools.partial(_proj_res_ln_kernel, alpha=alpha), grid=(t // ROW_BLOCK,),
        in_specs=[pl.BlockSpec((ROW_BLOCK, k), row), pl.BlockSpec((k, d), full),
                  pl.BlockSpec((ROW_BLOCK, d), row), pl.BlockSpec((1, d), full),
                  pl.BlockSpec((1, d), full)],
        out_specs=pl.BlockSpec((ROW_BLOCK, d), row),
        out_shape=jax.ShapeDtypeStruct((t, d), F32),
        compiler_params=_params(1), name="proj_res_ln",
    )(a, w, h2, g.reshape(1, d), b.reshape(1, d))


def _mem_kv_kernel(m_ref, w_ref, o_ref):
    o_ref[...] = _dot(m_ref[...].astype(BF16), w_ref[...]).astype(BF16)


def _mem_kv(mem2, w_kv):
    rows, d = mem2.shape
    n = w_kv.shape[1]
    blk = min(ROW_BLOCK, rows)
    return pl.pallas_call(
        _mem_kv_kernel, grid=(rows // blk,),
        in_specs=[pl.BlockSpec((blk, d), lambda i: (i, 0)), pl.BlockSpec((d, n), lambda i: (0, 0))],
        out_specs=pl.BlockSpec((blk, n), lambda i: (i, 0)),
        out_shape=jax.ShapeDtypeStruct((rows, n), BF16),
        compiler_params=_params(1), name="mem_kv",
    )(mem2, w_kv)


def _cross_attn_kernel(h_ref, wq_ref, kv_ref, wo_ref, g_ref, b_ref, o_ref, *, alpha, q_scale):
    h = h_ref[...]
    d = h.shape[1]
    hd = d // MEM_HEADS
    q = (_dot(h.astype(BF16), wq_ref[...]) * q_scale).astype(BF16)
    outs = []
    for hh in range(MEM_HEADS):
        s = _dot_nt(q[:, hh * hd:(hh + 1) * hd], kv_ref[:, hh * hd:(hh + 1) * hd])
        p = jnp.exp(s - jnp.max(s, axis=1, keepdims=True))
        p = p * (1.0 / jnp.sum(p, axis=1, keepdims=True))
        outs.append(_dot(p.astype(BF16), kv_ref[:, d + hh * hd:d + (hh + 1) * hd]).astype(BF16))
    o = jnp.concatenate(outs, axis=1)
    z = alpha * h + _dot(o, wo_ref[...])
    o_ref[...] = _layer_norm(z, g_ref[...], b_ref[...])


def _cross_attn(h2, w_q, memkv, w_out, g, b, alpha, seq, n_mem):
    t, d = h2.shape
    per_batch = seq // ROW_BLOCK
    row = lambda i: (i, 0)
    full = lambda i: (0, 0)
    kern = functools.partial(_cross_attn_kernel, alpha=alpha, q_scale=(d // MEM_HEADS) ** -0.5)
    return pl.pallas_call(
        kern, grid=(t // ROW_BLOCK,),
        in_specs=[pl.BlockSpec((ROW_BLOCK, d), row), pl.BlockSpec((d, d), full),
                  pl.BlockSpec((n_mem, 2 * d), lambda i: (i // per_batch, 0)),
                  pl.BlockSpec((d, d), full), pl.BlockSpec((1, d), full), pl.BlockSpec((1, d), full)],
        out_specs=pl.BlockSpec((ROW_BLOCK, d), row),
        out_shape=jax.ShapeDtypeStruct((t, d), F32),
        compiler_params=_params(1), name="cross_attn",
    )(h2, w_q, memkv, w_out, g.reshape(1, d), b.reshape(1, d))


def _route(h, wr_hi, wr_lo, br):
    h_hi = h.astype(BF16)
    h_lo = (h - h_hi.astype(F32)).astype(BF16)
    logits = _dot(h_hi, wr_hi) + _dot(h_hi, wr_lo) + _dot(h_lo, wr_hi) + br
    lane = lax.broadcasted_iota(I32, logits.shape, 1).astype(F32)
    gl = jnp.where(lane < N_GROUPS, logits, NEG_INF)
    gmax = jnp.max(gl, axis=1, keepdims=True)
    g_sel = jnp.min(jnp.where(gl == gmax, lane, float(LANES)), axis=1, keepdims=True)
    g_gate = 1.0 / jnp.sum(jnp.exp(gl - gmax), axis=1, keepdims=True)
    first = N_GROUPS + g_sel * EXPERTS_PER_GROUP
    el = jnp.where((lane >= first) & (lane < first + EXPERTS_PER_GROUP), logits, NEG_INF)
    v1 = jnp.max(el, axis=1, keepdims=True)
    i1 = jnp.min(jnp.where(el == v1, lane, float(LANES)), axis=1, keepdims=True)
    el2 = jnp.where(lane == i1, NEG_INF, el)
    v2 = jnp.max(el2, axis=1, keepdims=True)
    i2 = jnp.min(jnp.where(el2 == v2, lane, float(LANES)), axis=1, keepdims=True)
    r = jnp.exp(v2 - v1)
    w1 = g_gate / (1.0 + r)
    w2 = w1 * r
    first_is_low = i1 < i2
    e_lo = jnp.minimum(i1, i2) - first
    e_hi = jnp.maximum(i1, i2) - first
    return g_sel, e_lo, e_hi, jnp.where(first_is_low, w1, w2), jnp.where(first_is_low, w2, w1)


PAIRS_PER_GROUP = EXPERTS_PER_GROUP * (EXPERTS_PER_GROUP - 1) // 2
N_BINS = N_GROUPS * PAIRS_PER_GROUP
MOE_TILE = 256
W_LO_LANE, W_HI_LANE = 0, 64
META_ROWS = 8
DMA_LOOP_UNROLL = 8


def _bin_pairs():
    return [(lo, hi) for lo in range(EXPERTS_PER_GROUP) for hi in range(lo + 1, EXPERTS_PER_GROUP)]


def _moe_route_kernel(h_ref, wrh_ref, wrl_ref, br_ref, aug_ref, meta_ref, cnt_ref, run_ref):
    i = pl.program_id(0)

    @pl.when(i == 0)
    def _():
        run_ref[...] = jnp.zeros(run_ref.shape, F32)

    h = h_ref[...]
    rows, d = h.shape
    g_sel, e_lo, e_hi, w_lo, w_hi = _route(h, wrh_ref[...], wrl_ref[...], br_ref[...])
    pid = e_lo * (7.0 - e_lo) * 0.5 + e_hi - e_lo - 1.0
    bin_id = g_sel * PAIRS_PER_GROUP + pid
    lane = lax.broadcasted_iota(I32, (rows, LANES), 1).astype(F32)
    onehot = jnp.where(lane == bin_id, 1.0, 0.0)
    r_i = lax.broadcasted_iota(I32, (rows, rows), 0)
    c_i = lax.broadcasted_iota(I32, (rows, rows), 1)
    tri = jnp.where(c_i < r_i, 1.0, 0.0).astype(BF16)
    before = _dot(tri, onehot.astype(BF16)) + run_ref[...]
    rank = jnp.sum(before * onehot, axis=1, keepdims=True)
    run_ref[...] += jnp.sum(onehot, axis=0, keepdims=True)
    cnt_ref[...] = jnp.broadcast_to(run_ref[...], cnt_ref.shape)

    aug_ref[:, 0:d] = h
    aug_ref[:, d:d + LANES] = jnp.where(lane < W_HI_LANE, w_lo, w_hi)
    meta = jnp.where(lane == 0.0, bin_id, jnp.where(lane == 1.0, rank, 0.0))
    meta_ref[0] = meta.T[0:META_ROWS, :].astype(I32)


def _moe_route(h2, wr_hi, wr_lo, br):
    t, d = h2.shape
    nb = t // ROW_BLOCK
    row = lambda i: (i, 0)
    full = lambda i: (0, 0)
    return pl.pallas_call(
        _moe_route_kernel, grid=(nb,),
        in_specs=[pl.BlockSpec((ROW_BLOCK, d), row), pl.BlockSpec((d, LANES), full),
                  pl.BlockSpec((d, LANES), full), pl.BlockSpec((1, LANES), full)],
        out_specs=[pl.BlockSpec((ROW_BLOCK, d + LANES), row),
                   pl.BlockSpec((1, META_ROWS, ROW_BLOCK), lambda i: (i, 0, 0)),
                   pl.BlockSpec((8, LANES), full)],
        out_shape=[jax.ShapeDtypeStruct((t, d + LANES), F32),
                   jax.ShapeDtypeStruct((nb, META_ROWS, ROW_BLOCK), I32),
                   jax.ShapeDtypeStruct((8, LANES), F32)],
        scratch_shapes=[pltpu.VMEM((1, LANES), F32)],
        compiler_params=_params(1), name="moe_route",
    )(h2, wr_hi, wr_lo, br)


def _row_copy(src_ref, dst_ref, src_row, dst_row, sem):
    return pltpu.make_async_copy(src_ref.at[pl.ds(src_row, 1), :], dst_ref.at[pl.ds(dst_row, 1), :], sem)


def _moe_dispatch_kernel(off_ref, meta_ref, src_ref, init_ref, dst_ref, sem):
    del init_ref
    base = pl.program_id(0) * ROW_BLOCK

    def copy(r):
        slot = off_ref[meta_ref[0, 0, r]] + meta_ref[0, 1, r]
        return _row_copy(src_ref, dst_ref, base + r, slot, sem)

    def issue(r, carry):
        copy(r).start()
        return carry

    def drain(r, carry):
        copy(r).wait()
        return carry

    lax.fori_loop(0, ROW_BLOCK, issue, 0, unroll=DMA_LOOP_UNROLL)
    lax.fori_loop(0, ROW_BLOCK, drain, 0, unroll=DMA_LOOP_UNROLL)


def _moe_combine_kernel(off_ref, meta_ref, src_ref, dst_ref, sem):
    base = pl.program_id(0) * ROW_BLOCK

    def copy(r):
        slot = off_ref[meta_ref[0, 0, r]] + meta_ref[0, 1, r]
        return _row_copy(src_ref, dst_ref, slot, base + r, sem)

    def issue(r, carry):
        copy(r).start()
        return carry

    def drain(r, carry):
        copy(r).wait()
        return carry

    lax.fori_loop(0, ROW_BLOCK, issue, 0, unroll=DMA_LOOP_UNROLL)
    lax.fori_loop(0, ROW_BLOCK, drain, 0, unroll=DMA_LOOP_UNROLL)


def _moe_permute(kern, off, meta, src, out_rows, init=None, name=None):
    nb = meta.shape[0]
    width = src.shape[1]
    any_spec = pl.BlockSpec(memory_space=pl.ANY)
    operands = [off, meta, src] + ([init] if init is not None else [])
    grid_spec = pltpu.PrefetchScalarGridSpec(
        num_scalar_prefetch=1, grid=(nb,),
        in_specs=[pl.BlockSpec((1, META_ROWS, ROW_BLOCK), lambda i, off: (i, 0, 0), memory_space=pltpu.SMEM),
                  any_spec] + ([any_spec] if init is not None else []),
        out_specs=any_spec,
        scratch_shapes=[pltpu.SemaphoreType.DMA(())])
    return pl.pallas_call(
        kern, grid_spec=grid_spec,
        out_shape=jax.ShapeDtypeStruct((out_rows, width), src.dtype),
        input_output_aliases=({3: 0} if init is not None else {}),
        compiler_params=_params(1), name=name,
    )(*operands)


def _moe_expert_kernel(e_lo_ref, e_hi_ref, nv_ref, s_ref, wg1_ref, wu1_ref, wd1_ref, wg2_ref, wu2_ref, wd2_ref,
                       g_ref, b_ref, o_ref, *, alpha):
    del e_lo_ref, e_hi_ref
    occupied = pl.program_id(0) < nv_ref[0]

    @pl.when(jnp.logical_not(occupied))
    def _():
        o_ref[...] = jnp.zeros(o_ref.shape, F32)

    @pl.when(occupied)
    def _():
        d = o_ref.shape[1]
        x = s_ref[:, 0:d]
        xb = x.astype(BF16)
        y = jnp.zeros(x.shape, F32)
        for lane0, wg_ref, wu_ref, wd_ref in ((W_LO_LANE, wg1_ref, wu1_ref, wd1_ref),
                                              (W_HI_LANE, wg2_ref, wu2_ref, wd2_ref)):
            c = s_ref[:, d + lane0:d + lane0 + 1]
            a = _dot(xb, wg_ref[0])
            u = _dot(xb, wu_ref[0])
            hid = a * (1.0 / (1.0 + jnp.exp(-a))) * u
            y = y + _dot((c * hid).astype(BF16), wd_ref[0])
        o_ref[...] = _layer_norm(alpha * x + y, g_ref[...], b_ref[...])


def _moe_experts(sorted_rows, tile_e_lo, tile_e_hi, n_valid, w_gate, w_up, w_down, g, b, alpha):
    rows, width = sorted_rows.shape
    d = width - LANES
    _, _, ff = w_gate.shape
    n_tiles = rows // MOE_TILE
    tile = lambda i, lo, hi, nv: (jnp.minimum(i, nv[0] - 1), 0)
    out_tile = lambda i, lo, hi, nv: (i, 0)
    w_lo = lambda i, lo, hi, nv: (lo[jnp.minimum(i, nv[0] - 1)], 0, 0)
    w_hi = lambda i, lo, hi, nv: (hi[jnp.minimum(i, nv[0] - 1)], 0, 0)
    full = lambda i, lo, hi, nv: (0, 0)
    grid_spec = pltpu.PrefetchScalarGridSpec(
        num_scalar_prefetch=3, grid=(n_tiles,),
        in_specs=[pl.BlockSpec((MOE_TILE, width), tile),
                  pl.BlockSpec((1, d, ff), w_lo), pl.BlockSpec((1, d, ff), w_lo), pl.BlockSpec((1, ff, d), w_lo),
                  pl.BlockSpec((1, d, ff), w_hi), pl.BlockSpec((1, d, ff), w_hi), pl.BlockSpec((1, ff, d), w_hi),
                  pl.BlockSpec((1, d), full), pl.BlockSpec((1, d), full)],
        out_specs=pl.BlockSpec((MOE_TILE, d), out_tile))
    return pl.pallas_call(
        functools.partial(_moe_expert_kernel, alpha=alpha), grid_spec=grid_spec,
        out_shape=jax.ShapeDtypeStruct((rows, d), F32),
        compiler_params=_params(1), name="moe_experts",
    )(tile_e_lo, tile_e_hi, n_valid, sorted_rows, w_gate, w_up, w_down, w_gate, w_up, w_down,
      g.reshape(1, d), b.reshape(1, d))


def _moe(h2, wr_hi, wr_lo, br, w_gate, w_up, w_down, g, b, alpha):
    t, d = h2.shape
    aug, meta, counts = _moe_route(h2, wr_hi, wr_lo, br)
    cnt = counts[0, 0:N_BINS].astype(I32)
    padded = (cnt + MOE_TILE - 1) // MOE_TILE * MOE_TILE
    ends = jnp.cumsum(padded)
    off = (ends - padded).astype(I32)
    n_tiles = t // MOE_TILE + N_BINS
    tile_bin = jnp.minimum(jnp.searchsorted(ends, jnp.arange(n_tiles, dtype=I32) * MOE_TILE, side="right"),
                           N_BINS - 1).astype(I32)
    pairs = jnp.asarray(_bin_pairs(), I32)
    group = tile_bin // PAIRS_PER_GROUP
    tile_e_lo = group * EXPERTS_PER_GROUP + pairs[tile_bin % PAIRS_PER_GROUP, 0]
    tile_e_hi = group * EXPERTS_PER_GROUP + pairs[tile_bin % PAIRS_PER_GROUP, 1]
    n_valid = (ends[-1:] // MOE_TILE).astype(I32)
    sorted_rows = _moe_permute(_moe_dispatch_kernel, off, meta, aug, n_tiles * MOE_TILE,
                               init=jnp.zeros((n_tiles * MOE_TILE, d + LANES), F32), name="moe_dispatch")
    out_sorted = _moe_experts(sorted_rows, tile_e_lo, tile_e_hi, n_valid, w_gate, w_up, w_down, g, b, alpha)
    return _moe_permute(_moe_combine_kernel, off, meta, out_sorted, t, name="moe_combine")


_COL_CQ, _COL_CKV, _COL_KROPE, _COL_KIDX, _COL_WIDX, _COL_END = 0, 256, 512, 640, 768, 896


def _dsa_proj_kernel(h_ref, win_ref, gq_ref, gkv_ref, wuq_ref, wqi_ref, wuk_ref,
                     cr_ref, sur_ref, sdr_ref, ci_ref, sui_ref, sdi_ref,
                     qft_ref, kv_ref, kvt_ref, qit_ref, ki_ref, wit_ref, *, q_scale, w_scale):
    y = _dot(h_ref[...].astype(BF16), win_ref[...])
    cr, sur, sdr = cr_ref[...], sur_ref[...], sdr_ref[...]
    ci, sui, sdi = ci_ref[...], sui_ref[...], sdi_ref[...]
    half_r, half_i = MLA_ROPE_DIM // 2, IDX_ROPE_DIM // 2
    qb = QUERY_BLOCK
    n_qb = y.shape[0] // qb
    c_q = _rms_norm(y[:, _COL_CQ:_COL_CKV], gq_ref[...]).astype(BF16)
    c_kv = _rms_norm(y[:, _COL_CKV:_COL_KROPE], gkv_ref[...])
    kv_ref[:, 0:MLA_KV_RANK] = c_kv.astype(BF16)
    kvt_ref[0] = c_kv.T.astype(BF16)
    kv_ref[:, MLA_KV_RANK:MLA_FEAT] = _rope(y[:, _COL_KROPE:_COL_KIDX], cr, sur, sdr, half_r).astype(BF16)
    ki_ref[...] = _rope(y[:, _COL_KIDX:_COL_WIDX], ci, sui, sdi, half_i).astype(BF16)
    wit_ref[...] = (y[:, _COL_WIDX:_COL_END] * w_scale).T[0:IDX_HEADS, :]
    q = _dot(c_q, wuq_ref[...])
    q_bf = q.astype(BF16)
    hd = MLA_ROPE_DIM + MLA_NOPE_DIM
    for hh in range(MLA_HEADS):
        lat_t = (_dot(q_bf[:, hh * hd:(hh + 1) * hd], wuk_ref[hh]) * q_scale).T.astype(BF16)
        rope_t = (_rope(q[:, hh * hd:(hh + 1) * hd], cr, sur, sdr, half_r) * q_scale).T.astype(BF16)
        for bl in range(n_qb):
            qft_ref[bl, 0:MLA_KV_RANK, hh * qb:(hh + 1) * qb] = lat_t[:, bl * qb:(bl + 1) * qb]
            qft_ref[bl, MLA_KV_RANK:MLA_FEAT, hh * qb:(hh + 1) * qb] = rope_t[:, bl * qb:(bl + 1) * qb]
    qi = _dot(c_q, wqi_ref[...])
    for p in range(IDX_HEADS // 2):
        pair_t = _rope(qi[:, p * LANES:(p + 1) * LANES], ci, sui, sdi, half_i).T.astype(BF16)
        for bl in range(n_qb):
            qit_ref[bl, p] = pair_t[:, bl * qb:(bl + 1) * qb]


def _dsa_proj(h2, w_in_p, gq, gkv, w_uq, w_qidx, w_ukt, tabs_r, tabs_i):
    t, d = h2.shape
    row = lambda i: (i, 0)
    full = lambda i: (0, 0)
    full3 = lambda i: (0, 0, 0)
    n_pairs = IDX_HEADS // 2
    qb = QUERY_BLOCK
    n_qb = ROW_BLOCK // qb
    kern = functools.partial(_dsa_proj_kernel, q_scale=(MLA_ROPE_DIM + MLA_NOPE_DIM) ** -0.5 * LOG2E,
                             w_scale=(IDX_HEADS * IDX_DIM) ** -0.5)
    tab = pl.BlockSpec((ROW_BLOCK, LANES), row)
    return pl.pallas_call(
        kern, grid=(t // ROW_BLOCK,),
        in_specs=[pl.BlockSpec((ROW_BLOCK, d), row), pl.BlockSpec(w_in_p.shape, full),
                  pl.BlockSpec((1, MLA_Q_RANK), full), pl.BlockSpec((1, MLA_KV_RANK), full),
                  pl.BlockSpec(w_uq.shape, full), pl.BlockSpec(w_qidx.shape, full),
                  pl.BlockSpec(w_ukt.shape, full3)] + [tab] * 6,
        out_specs=[pl.BlockSpec((n_qb, MLA_FEAT, MLA_HEADS * qb), lambda i: (i, 0, 0)),
                   pl.BlockSpec((ROW_BLOCK, MLA_FEAT), row),
                   pl.BlockSpec((1, MLA_KV_RANK, ROW_BLOCK), lambda i: (i, 0, 0)),
                   pl.BlockSpec((n_qb, n_pairs, LANES, qb), lambda i: (i, 0, 0, 0)),
                   pl.BlockSpec((ROW_BLOCK, LANES), row),
                   pl.BlockSpec((IDX_HEADS, ROW_BLOCK), lambda i: (0, i))],
        out_shape=[jax.ShapeDtypeStruct((t // qb, MLA_FEAT, MLA_HEADS * qb), BF16),
                   jax.ShapeDtypeStruct((t, MLA_FEAT), BF16),
                   jax.ShapeDtypeStruct((t // ROW_BLOCK, MLA_KV_RANK, ROW_BLOCK), BF16),
                   jax.ShapeDtypeStruct((t // qb, n_pairs, LANES, qb), BF16),
                   jax.ShapeDtypeStruct((t, LANES), BF16),
                   jax.ShapeDtypeStruct((IDX_HEADS, t), F32)],
        compiler_params=_params(1), name="dsa_proj",
    )(h2, w_in_p, gq.reshape(1, -1), gkv.reshape(1, -1), w_uq, w_qidx, w_ukt, *tabs_r, *tabs_i)


def _sortable_key(x):
    bits = lax.bitcast_convert_type(x, I32)
    return bits ^ ((bits >> 31) & 0x7FFFFFFF)


def _dsa_attn_kernel(qit_ref, wit_ref, qft_ref, ki_ref, kv_ref, kvt_ref, wuvt_ref, o_ref,
                     key_ref, thr_ref, m_ref, l_ref, acc_ref, *, kt, top_k):
    g = pl.program_id(1)
    qb = QUERY_BLOCK
    n_tiles = ((g + 1) * qb + kt - 1) // kt
    n_pairs = IDX_HEADS // 2
    lane_q = lax.broadcasted_iota(I32, (1, qb), 1)
    n_allowed = (g * (qb // CHUNK) + 1 + lane_q // CHUNK) * CHUNK

    feat = lax.broadcasted_iota(I32, (LANES, qb), 0)
    pair_w = []
    for p in range(n_pairs):
        slab = qit_ref[0, p]
        zero = jnp.zeros_like(slab)
        pair_w.append(jnp.concatenate([jnp.where(feat < IDX_DIM, slab, zero),
                                       jnp.where(feat >= IDX_DIM, slab, zero)], axis=1))
    wt = wit_ref[...]

    def score_tile(t, carry):
        lo, hi = carry
        k = ki_ref[pl.ds(pl.multiple_of(t * kt, kt), kt), :]
        sc = jnp.zeros((kt, qb), F32)
        for p in range(n_pairs):
            lg = _dot(k, pair_w[p])
            sc = (sc + jnp.maximum(lg[:, 0:qb], 0.0) * wt[2 * p:2 * p + 1, :]
                  + jnp.maximum(lg[:, qb:2 * qb], 0.0) * wt[2 * p + 1:2 * p + 2, :])
        kk = t * kt + lax.broadcasted_iota(I32, sc.shape, 0)
        valid = kk < n_allowed
        key_ref[t] = _sortable_key(jnp.where(valid, sc, NEG_INF))
        lo = jnp.minimum(lo, jnp.min(jnp.where(valid, sc, float("inf")), axis=0, keepdims=True))
        hi = jnp.maximum(hi, jnp.max(jnp.where(valid, sc, NEG_INF), axis=0, keepdims=True))
        return lo, hi

    lo_f, hi_f = lax.fori_loop(0, n_tiles, score_tile,
                               (jnp.full((1, qb), float("inf"), F32), jnp.full((1, qb), NEG_INF, F32)))

    keep_all = jnp.full((1, qb), NEG_INF_KEY + 1, I32)
    thr_ref[...] = keep_all

    @pl.when((g + 1) * qb > top_k)
    def _():
        def count_ge(mid):
            def body(t, cnt):
                hit = jnp.where(key_ref[t] >= mid, 1, 0)
                return cnt + jnp.sum(hit.reshape(kt // 8, 8, qb), axis=0)
            cnt = lax.fori_loop(0, n_tiles, body, jnp.zeros((8, qb), I32))
            return jnp.sum(cnt.astype(F32), axis=0, keepdims=True)

        def bisect(st):
            lo, hi = st
            for _ in range(BISECT_STEPS_PER_CHECK):
                mid = (lo >> 1) + (hi >> 1) + ((lo | hi) & 1)
                cnt = count_ge(mid)
                ge = cnt >= float(top_k)
                lo_next = jnp.where(ge, mid, lo)
                hi = jnp.where(cnt == float(top_k), mid, jnp.where(ge, hi, mid - 1))
                lo = lo_next
            return lo, hi

        def unresolved(st):
            lo, hi = st
            return jnp.max(jnp.where(hi > lo, 1.0, 0.0)) > 0.0

        lo, _ = lax.while_loop(unresolved, bisect, (_sortable_key(lo_f), _sortable_key(hi_f)))
        thr_ref[...] = jnp.where(n_allowed > top_k, lo, keep_all)

    thr = thr_ref[...]
    m_ref[...] = jnp.full(m_ref.shape, NEG_INF, F32)
    l_ref[...] = jnp.zeros(l_ref.shape, F32)
    acc_ref[...] = jnp.zeros(acc_ref.shape, F32)
    hpg = HEADS_PER_GROUP
    group = hpg * qb

    def scores(t, gi):
        kv_rows = kv_ref[pl.ds(pl.multiple_of(t * kt, kt), kt), :]
        bias = jnp.where(key_ref[t] >= thr, 0.0, NEG_INF)
        bias_g = jnp.concatenate([bias] * hpg, axis=1)
        return _dot(kv_rows, qft_ref[0, :, gi * group:(gi + 1) * group]) + bias_g

    def softmax_pv(t, gi, s):
        m_old = m_ref[gi]
        m_new = jnp.maximum(m_old, jnp.max(s, axis=0, keepdims=True))
        m_safe = jnp.where(m_new == NEG_INF, 0.0, m_new)
        p = jnp.exp2(s - m_safe)
        alpha = jnp.exp2(m_old - m_safe)
        l_ref[gi] = alpha * l_ref[gi] + jnp.sum(p, axis=0, keepdims=True)
        acc_ref[gi] = alpha * acc_ref[gi] + _dot(kvt_ref[t], p.astype(BF16))
        m_ref[gi] = m_new

    def run(tiles):
        items = [(t, gi) for t in tiles for gi in range(MLA_HEADS // hpg)]
        ahead = 2
        pending = {n: scores(*items[n]) for n in range(min(ahead, len(items)))}
        for n, item in enumerate(items):
            if n + ahead < len(items):
                pending[n + ahead] = scores(*items[n + ahead])
            softmax_pv(*item, pending.pop(n))

    def tile_pair(tt, carry):
        run([2 * tt, 2 * tt + 1])
        return carry

    lax.fori_loop(0, n_tiles // 2, tile_pair, 0)

    @pl.when(n_tiles % 2 == 1)
    def _():
        run([n_tiles - 1])

    for gi in range(MLA_HEADS // hpg):
        o_lat_t = (acc_ref[gi] * (1.0 / l_ref[gi])).astype(BF16)
        for hl in range(hpg):
            hh = gi * hpg + hl
            o_t = _dot(wuvt_ref[hh], o_lat_t[:, hl * qb:(hl + 1) * qb])
            o_ref[:, hh * MLA_V_DIM:(hh + 1) * MLA_V_DIM] = o_t.T.astype(BF16)


def _dsa_attn(qft, kv, kvt, qit, ki, wit, w_uvt, bsz, seq):
    kt = KEY_TILE
    qb = QUERY_BLOCK
    ng = seq // qb
    n_pairs = IDX_HEADS // 2
    top_k = min(IDX_TOPK_MAX, seq // 4)
    kern = functools.partial(_dsa_attn_kernel, kt=kt, top_k=top_k)
    batch2 = lambda b, g: (b, 0)
    n_groups, group = MLA_HEADS // HEADS_PER_GROUP, HEADS_PER_GROUP * qb
    return pl.pallas_call(
        kern, grid=(bsz, ng),
        in_specs=[pl.BlockSpec((1, n_pairs, LANES, qb), lambda b, g: (b * ng + g, 0, 0, 0)),
                  pl.BlockSpec((IDX_HEADS, qb), lambda b, g: (0, b * ng + g)),
                  pl.BlockSpec((1, MLA_FEAT, MLA_HEADS * qb), lambda b, g: (b * ng + g, 0, 0)),
                  pl.BlockSpec((seq, LANES), batch2),
                  pl.BlockSpec((seq, MLA_FEAT), batch2),
                  pl.BlockSpec((seq // kt, MLA_KV_RANK, kt), lambda b, g: (b, 0, 0)),
                  pl.BlockSpec(w_uvt.shape, lambda b, g: (0, 0, 0))],
        out_specs=pl.BlockSpec((qb, MLA_HEADS * MLA_V_DIM), lambda b, g: (b * ng + g, 0)),
        out_shape=jax.ShapeDtypeStruct((bsz * seq, MLA_HEADS * MLA_V_DIM), BF16),
        scratch_shapes=[pltpu.VMEM((seq // kt, kt, qb), I32), pltpu.VMEM((1, qb), I32),
                        pltpu.VMEM((n_groups, 1, group), F32), pltpu.VMEM((n_groups, 1, group), F32),
                        pltpu.VMEM((n_groups, MLA_KV_RANK, group), F32)],
        compiler_params=_params(2), name="dsa_attn",
    )(qit, wit, qft, ki, kv, kvt, w_uvt)


def _diff_lambda_init(layer_idx):
    import math
    return 0.8 - 0.6 * math.exp(-0.3 * layer_idx)


def _pad_dsa_w_in(w_in):
    d = w_in.shape[0]
    o_cq, o_ckv = 0, MLA_Q_RANK
    o_kr = o_ckv + MLA_KV_RANK
    o_ki = o_kr + MLA_ROPE_DIM
    o_wi = o_ki + IDX_DIM
    out = jnp.zeros((d, _COL_END), w_in.dtype)
    out = out.at[:, _COL_CQ:_COL_CQ + MLA_Q_RANK].set(w_in[:, o_cq:o_ckv])
    out = out.at[:, _COL_CKV:_COL_CKV + MLA_KV_RANK].set(w_in[:, o_ckv:o_kr])
    out = out.at[:, _COL_KROPE:_COL_KROPE + MLA_ROPE_DIM].set(w_in[:, o_kr:o_ki])
    out = out.at[:, _COL_KIDX:_COL_KIDX + IDX_DIM].set(w_in[:, o_ki:o_wi])
    out = out.at[:, _COL_KIDX + IDX_DIM:_COL_KIDX + 2 * IDX_DIM].set(w_in[:, o_ki:o_wi])
    out = out.at[:, _COL_WIDX:_COL_WIDX + IDX_HEADS].set(w_in[:, o_wi:o_wi + IDX_HEADS])
    return out.astype(BF16)


def _router_weights(w_group, b_group, w_expert, b_expert):
    d = w_group.shape[0]
    w = jnp.zeros((d, LANES), F32)
    w = w.at[:, 0:N_GROUPS].set(w_group).at[:, N_GROUPS:N_GROUPS + N_EXPERTS].set(w_expert)
    br = jnp.zeros((1, LANES), F32)
    br = br.at[0, 0:N_GROUPS].set(b_group).at[0, N_GROUPS:N_GROUPS + N_EXPERTS].set(b_expert)
    w_hi = w.astype(BF16)
    w_lo = (w - w_hi.astype(F32)).astype(BF16)
    return w_hi, w_lo, br


def kernel(x, mem, positions, a_w_in, a_lambda, a_subln_g, a_w_out, b_w_in, b_q_norm_g, b_kv_norm_g,
           b_w_uq, b_w_qidx, b_w_uk, b_w_uv, b_w_out, mem_w_kv, xa_w_q, xa_w_out,
           moe_w_group, moe_b_group, moe_w_expert, moe_b_expert, moe_w_gate, moe_w_up, moe_w_down,
           ln_g, ln_b):
    bsz, seq, d = x.shape
    n_mem = mem.shape[1]
    depth = ln_g.shape[0]
    alpha = (2.0 * depth) ** 0.25
    assert seq % ROW_BLOCK == 0 and seq % KEY_TILE == 0
    assert ROW_BLOCK == ATTN_BLOCK and ROW_BLOCK == KEY_TILE and QUERY_BLOCK == 2 * CHUNK

    tabs_d = _rope_lane_tables(positions, DIFF_ROPE_DIM, DIFF_HEAD_DIM, True)
    tabs_r = _rope_lane_tables(positions, MLA_ROPE_DIM, LANES, False)
    tabs_i = _rope_lane_tables(positions, IDX_ROPE_DIM, IDX_DIM, True)

    memkv = _mem_kv(mem.reshape(bsz * n_mem, d), mem_w_kv.astype(BF16))

    h = x.reshape(bsz * seq, d)
    for i in range(depth):
        j = i // N_MIXERS
        if i % N_MIXERS == 0:
            qt, k, vt = _diff_qkv(h, a_w_in[j].astype(BF16), tabs_d)
            mix = _diff_attn(qt, k, vt, a_lambda[j], a_subln_g[j], bsz, seq, _diff_lambda_init(i))
            w_mix_out = a_w_out[j]
        else:
            w_ukt = jnp.pad(jnp.swapaxes(b_w_uk[j], 1, 2), ((0, 0), (MLA_ROPE_DIM, 0), (0, 0))).astype(BF16)
            qft, kv, kvt, qit, ki, wit = _dsa_proj(h, _pad_dsa_w_in(b_w_in[j]), b_q_norm_g[j], b_kv_norm_g[j],
                                                   b_w_uq[j].astype(BF16), b_w_qidx[j].astype(BF16), w_ukt,
                                                   tabs_r, tabs_i)
            w_uvt = jnp.swapaxes(b_w_uv[j], 1, 2).astype(BF16)
            mix = _dsa_attn(qft, kv, kvt, qit, ki, wit, w_uvt, bsz, seq)
            w_mix_out = b_w_out[j]
        h = _proj_res_ln(mix, w_mix_out.astype(BF16), h, ln_g[i, 0], ln_b[i, 0], alpha)
        h = _cross_attn(h, xa_w_q[i].astype(BF16), memkv, xa_w_out[i].astype(BF16),
                        ln_g[i, 1], ln_b[i, 1], alpha, seq, n_mem)
        wr_hi, wr_lo, br = _router_weights(moe_w_group[i], moe_b_group[i], moe_w_expert[i], moe_b_expert[i])
        h = _moe(h, wr_hi, wr_lo, br, moe_w_gate[i].astype(BF16), moe_w_up[i].astype(BF16),
                 moe_w_down[i].astype(BF16), ln_g[i, 2], ln_b[i, 2], alpha)
    return h.reshape(bsz, seq, d)
```

```python
import functools

import jax
import jax.numpy as jnp
from jax import lax
from jax.experimental import pallas as pl
from jax.experimental.pallas import tpu as pltpu

F32 = jnp.float32
BF16 = jnp.bfloat16
I32 = jnp.int32

CHUNK = 64
ROPE_THETA = 500000.0
LN_EPS = 1e-5
N_MIXERS = 2
DIFF_HEADS = 8
DIFF_HEAD_DIM = 64
DIFF_ROPE_DIM = DIFF_HEAD_DIM // 4
MLA_HEADS = 8
MLA_Q_RANK = 256
MLA_KV_RANK = 256
MLA_ROPE_DIM = 32
MLA_NOPE_DIM = 96
MLA_V_DIM = 128
IDX_HEADS = 16
IDX_DIM = 64
IDX_ROPE_DIM = IDX_DIM // 4
IDX_TOPK_MAX = 256
MEM_HEADS = 4
N_GROUPS = 4
EXPERTS_PER_GROUP = 4
N_EXPERTS = N_GROUPS * EXPERTS_PER_GROUP

LANES = 128
VMEM_LIMIT_BYTES = 56 * 1024 * 1024

ROW_BLOCK = 512
ATTN_BLOCK = 512
KEY_TILE = 512
QUERY_BLOCK = 128
HEADS_PER_GROUP = 4
BISECT_STEPS_PER_CHECK = 4
MLA_FEAT = MLA_KV_RANK + LANES

NEG_INF = float("-inf")
NEG_INF_KEY = -2139095041

_NT = (((1,), (1,)), ((), ()))


def _params(n_axes):
    return pltpu.CompilerParams(dimension_semantics=("arbitrary",) * n_axes,
                                vmem_limit_bytes=VMEM_LIMIT_BYTES)


def _dot(a, b):
    return jnp.dot(a, b, preferred_element_type=F32)


def _dot_nt(a, b):
    return lax.dot_general(a, b, _NT, preferred_element_type=F32)


def _layer_norm(z, g, b):
    mu = jnp.mean(z, axis=-1, keepdims=True)
    zc = z - mu
    var = jnp.mean(zc * zc, axis=-1, keepdims=True)
    return zc * lax.rsqrt(var + LN_EPS) * g + b


def _rms_norm(x, g):
    return x * lax.rsqrt(jnp.mean(x * x, axis=-1, keepdims=True) + LN_EPS) * g


def _rope(y, c, s_up, s_dn, half):
    return (y * c + pltpu.roll(y, LANES - half, 1) * s_up + pltpu.roll(y, half, 1) * s_dn)


def _rope_lane_tables(positions, rot_dim, period, keep_rest):
    half = rot_dim // 2
    inv_freq = ROPE_THETA ** (-jnp.arange(0, rot_dim, 2, dtype=F32) / rot_dim)
    ang = positions.astype(F32)[..., None] * inv_freq
    cos, sin = jnp.cos(ang), jnp.sin(ang)
    lead = positions.shape
    rest = jnp.full(lead + (period - rot_dim,), 1.0 if keep_rest else 0.0, F32)
    zrest = jnp.zeros(lead + (period - rot_dim,), F32)
    zhalf = jnp.zeros(lead + (half,), F32)
    reps = LANES // period
    out = []
    for parts in ((cos, cos, rest), (-sin, zhalf, zrest), (zhalf, sin, zrest)):
        t = jnp.concatenate(parts, axis=-1)
        out.append(jnp.tile(t, (1,) * len(lead) + (reps,)).reshape(-1, LANES))
    return out


LOG2E = 1.4426950408889634


def _diff_qkv_kernel(h_ref, w_ref, c_ref, su_ref, sd_ref, qt_ref, k_ref, vt_ref, *, q_scale, half):
    x = h_ref[...].astype(BF16)
    c, su, sd = c_ref[...], su_ref[...], sd_ref[...]
    n_heads = qt_ref.shape[0]
    for j2 in range(3 * n_heads // 2):
        y2 = _dot(x, w_ref[:, j2 * 2 * LANES:(j2 + 1) * 2 * LANES])
        for s in range(2):
            j = 2 * j2 + s
            y = y2[:, s * LANES:(s + 1) * LANES]
            if j < n_heads:
                qt_ref[j, 0] = (_rope(y, c, su, sd, half) * q_scale).T.astype(BF16)
            elif j < 2 * n_heads:
                k_ref[:, (j - n_heads) * LANES:(j - n_heads + 1) * LANES] = _rope(y, c, su, sd, half).astype(BF16)
            else:
                vt_ref[j - 2 * n_heads, 0] = y.T.astype(BF16)


def _diff_qkv(h2, w_in, tabs):
    t, d = h2.shape
    hd = 2 * DIFF_HEAD_DIM
    nb = t // ROW_BLOCK
    kern = functools.partial(_diff_qkv_kernel, q_scale=DIFF_HEAD_DIM ** -0.5 * LOG2E, half=DIFF_ROPE_DIM // 2)
    row = lambda i: (i, 0)
    full = lambda i: (0, 0)
    t_shape = jax.ShapeDtypeStruct((DIFF_HEADS, nb, hd, ROW_BLOCK), BF16)
    t_spec = pl.BlockSpec((DIFF_HEADS, 1, hd, ROW_BLOCK), lambda i: (0, i, 0, 0))
    return pl.pallas_call(
        kern, grid=(nb,),
        in_specs=[pl.BlockSpec((ROW_BLOCK, d), row), pl.BlockSpec(w_in.shape, full)]
        + [pl.BlockSpec((ROW_BLOCK, LANES), row)] * 3,
        out_specs=[t_spec, pl.BlockSpec((ROW_BLOCK, DIFF_HEADS * hd), row), t_spec],
        out_shape=[t_shape, jax.ShapeDtypeStruct((t, DIFF_HEADS * hd), BF16), t_shape],
        compiler_params=_params(1), name="diff_qkv",
    )(h2, w_in, *tabs)


def _diff_attn_kernel(lam_ref, g_ref, qt_ref, k_ref, vt_ref, o_ref, m_ref, l_ref, acc_ref, *,
                      blk, lambda_init):
    i = pl.program_id(2)
    lam = lam_ref[...]
    lam_full = (jnp.exp(jnp.sum(lam[0:1] * lam[1:2], axis=1, keepdims=True))
                - jnp.exp(jnp.sum(lam[2:3] * lam[3:4], axis=1, keepdims=True)) + lambda_init)
    qt = qt_ref[0, 0]
    feat = lax.broadcasted_iota(I32, qt.shape, 0)
    zero = jnp.zeros_like(qt)
    q_maps = (jnp.where(feat < DIFF_HEAD_DIM, qt, zero), jnp.where(feat >= DIFF_HEAD_DIM, qt, zero))
    m_ref[...] = jnp.full(m_ref.shape, NEG_INF, F32)
    l_ref[...] = jnp.zeros(l_ref.shape, F32)
    acc_ref[...] = jnp.zeros(acc_ref.shape, F32)

    def scores(j, c, masked):
        k = k_ref[pl.ds(pl.multiple_of(j * blk, blk), blk), :]
        s = _dot(k, q_maps[c])
        if masked:
            kc = lax.broadcasted_iota(I32, s.shape, 0) // CHUNK
            qc = lax.broadcasted_iota(I32, s.shape, 1) // CHUNK
            s = jnp.where(kc <= qc, s, NEG_INF)
        return s

    def softmax_pv(j, c, s):
        m_old = m_ref[c]
        m_new = jnp.maximum(m_old, jnp.max(s, axis=0, keepdims=True))
        p = jnp.exp2(s - m_new)
        alpha = jnp.exp2(m_old - m_new)
        l_ref[c] = alpha * l_ref[c] + jnp.sum(p, axis=0, keepdims=True)
        acc_ref[c] = alpha * acc_ref[c] + _dot(vt_ref[0, j], p.astype(BF16))
        m_ref[c] = m_new

    def run(blocks):
        items = [(j, c, masked) for (j, masked) in blocks for c in range(2)]
        ahead = 2
        pending = {n: scores(*items[n]) for n in range(min(ahead, len(items)))}
        for n, (j, c, _) in enumerate(items):
            if n + ahead < len(items):
                pending[n + ahead] = scores(*items[n + ahead])
            softmax_pv(j, c, pending.pop(n))

    def full_pair(jj, carry):
        run([(2 * jj, False), (2 * jj + 1, False)])
        return carry

    lax.fori_loop(0, i // 2, full_pair, 0)

    @pl.when(i % 2 == 1)
    def _():
        run([(i - 1, False), (i, True)])

    @pl.when(i % 2 == 0)
    def _():
        run([(i, True)])

    ot = acc_ref[0] * (1.0 / l_ref[0]) - lam_full * (acc_ref[1] * (1.0 / l_ref[1]))
    ot = ot * lax.rsqrt(jnp.mean(ot * ot, axis=0, keepdims=True) + LN_EPS) * (1.0 - lambda_init)
    g = g_ref[...]
    for s in range(blk // LANES):
        o_ref[s * LANES:(s + 1) * LANES, :] = (ot[:, s * LANES:(s + 1) * LANES] * g).T.astype(BF16)


def _diff_attn(qt, k, vt, lam, subln_g, bsz, seq, lambda_init):
    blk = ATTN_BLOCK
    nq = seq // blk
    hd = 2 * DIFF_HEAD_DIM
    kern = functools.partial(_diff_attn_kernel, blk=blk, lambda_init=lambda_init)
    g = jnp.broadcast_to(subln_g.astype(F32)[:, None], (hd, LANES))
    return pl.pallas_call(
        kern, grid=(bsz, DIFF_HEADS, nq),
        in_specs=[
            pl.BlockSpec(lam.shape, lambda b, h, i: (0, 0)),
            pl.BlockSpec((hd, LANES), lambda b, h, i: (0, 0)),
            pl.BlockSpec((1, 1, hd, blk), lambda b, h, i: (h, b * nq + i, 0, 0)),
            pl.BlockSpec((seq, hd), lambda b, h, i: (b, h)),
            pl.BlockSpec((1, nq, hd, blk), lambda b, h, i: (h, b, 0, 0)),
        ],
        out_specs=pl.BlockSpec((blk, hd), lambda b, h, i: (b * nq + i, h)),
        out_shape=jax.ShapeDtypeStruct((bsz * seq, DIFF_HEADS * hd), BF16),
        scratch_shapes=[pltpu.VMEM((2, 1, blk), F32), pltpu.VMEM((2, 1, blk), F32),
                        pltpu.VMEM((2, hd, blk), F32)],
        compiler_params=_params(3), name="diff_attn",
    )(lam, g, qt, k, vt)


def _proj_res_ln_kernel(a_ref, w_ref, h_ref, g_ref, b_ref, o_ref, *, alpha):
    z = alpha * h_ref[...] + _dot(a_ref[...], w_ref[...])
    o_ref[...] = _layer_norm(z, g_ref[...], b_ref[...])


def _proj_res_ln(a, w, h2, g, b, alpha):
    t, d = h2.shape
    k = a.shape[1]
    row = lambda i: (i, 0)
    full = lambda i: (0, 0)
    return pl.pallas_call(
        functools.partial(_proj_res_ln_kernel, alpha=alpha), grid=(t // ROW_BLOCK,),
        in_specs=[pl.BlockSpec((ROW_BLOCK, k), row), pl.BlockSpec((k, d), full),
                  pl.BlockSpec((ROW_BLOCK, d), row), pl.BlockSpec((1, d), full),
                  pl.BlockSpec((1, d), full)],
        out_specs=pl.BlockSpec((ROW_BLOCK, d), row),
        out_shape=jax.ShapeDtypeStruct((t, d), F32),
        compiler_params=_params(1), name="proj_res_ln",
    )(a, w, h2, g.reshape(1, d), b.reshape(1, d))


def _mem_kv_kernel(m_ref, w_ref, o_ref):
    o_ref[...] = _dot(m_ref[...].astype(BF16), w_ref[...]).astype(BF16)


def _mem_kv(mem2, w_kv):
    rows, d = mem2.shape
    n = w_kv.shape[1]
    blk = min(ROW_BLOCK, rows)
    return pl.pallas_call(
        _mem_kv_kernel, grid=(rows // blk,),
        in_specs=[pl.BlockSpec((blk, d), lambda i: (i, 0)), pl.BlockSpec((d, n), lambda i: (0, 0))],
        out_specs=pl.BlockSpec((blk, n), lambda i: (i, 0)),
        out_shape=jax.ShapeDtypeStruct((rows, n), BF16),
        compiler_params=_params(1), name="mem_kv",
    )(mem2, w_kv)


def _cross_attn_kernel(h_ref, wq_ref, kv_ref, wo_ref, g_ref, b_ref, o_ref, *, alpha, q_scale):
    h = h_ref[...]
    d = h.shape[1]
    hd = d // MEM_HEADS
    q = (_dot(h.astype(BF16), wq_ref[...]) * q_scale).astype(BF16)
    outs = []
    for hh in range(MEM_HEADS):
        s = _dot_nt(q[:, hh * hd:(hh + 1) * hd], kv_ref[:, hh * hd:(hh + 1) * hd])
        p = jnp.exp(s - jnp.max(s, axis=1, keepdims=True))
        p = p * (1.0 / jnp.sum(p, axis=1, keepdims=True))
        outs.append(_dot(p.astype(BF16), kv_ref[:, d + hh * hd:d + (hh + 1) * hd]).astype(BF16))
    o = jnp.concatenate(outs, axis=1)
    z = alpha * h + _dot(o, wo_ref[...])
    o_ref[...] = _layer_norm(z, g_ref[...], b_ref[...])


def _cross_attn(h2, w_q, memkv, w_out, g, b, alpha, seq, n_mem):
    t, d = h2.shape
    per_batch = seq // ROW_BLOCK
    row = lambda i: (i, 0)
    full = lambda i: (0, 0)
    kern = functools.partial(_cross_attn_kernel, alpha=alpha, q_scale=(d // MEM_HEADS) ** -0.5)
    return pl.pallas_call(
        kern, grid=(t // ROW_BLOCK,),
        in_specs=[pl.BlockSpec((ROW_BLOCK, d), row), pl.BlockSpec((d, d), full),
                  pl.BlockSpec((n_mem, 2 * d), lambda i: (i // per_batch, 0)),
                  pl.BlockSpec((d, d), full), pl.BlockSpec((1, d), full), pl.BlockSpec((1, d), full)],
        out_specs=pl.BlockSpec((ROW_BLOCK, d), row),
        out_shape=jax.ShapeDtypeStruct((t, d), F32),
        compiler_params=_params(1), name="cross_attn",
    )(h2, w_q, memkv, w_out, g.reshape(1, d), b.reshape(1, d))


def _route(h, wr_hi, wr_lo, br):
    h_hi = h.astype(BF16)
    h_lo = (h - h_hi.astype(F32)).astype(BF16)
    logits = _dot(h_hi, wr_hi) + _dot(h_hi, wr_lo) + _dot(h_lo, wr_hi) + br
    lane = lax.broadcasted_iota(I32, logits.shape, 1).astype(F32)
    gl = jnp.where(lane < N_GROUPS, logits, NEG_INF)
    gmax = jnp.max(gl, axis=1, keepdims=True)
    g_sel = jnp.min(jnp.where(gl == gmax, lane, float(LANES)), axis=1, keepdims=True)
    g_gate = 1.0 / jnp.sum(jnp.exp(gl - gmax), axis=1, keepdims=True)
    first = N_GROUPS + g_sel * EXPERTS_PER_GROUP
    el = jnp.where((lane >= first) & (lane < first + EXPERTS_PER_GROUP), logits, NEG_INF)
    v1 = jnp.max(el, axis=1, keepdims=True)
    i1 = jnp.min(jnp.where(el == v1, lane, float(LANES)), axis=1, keepdims=True)
    el2 = jnp.where(lane == i1, NEG_INF, el)
    v2 = jnp.max(el2, axis=1, keepdims=True)
    i2 = jnp.min(jnp.where(el2 == v2, lane, float(LANES)), axis=1, keepdims=True)
    r = jnp.exp(v2 - v1)
    w1 = g_gate / (1.0 + r)
    w2 = w1 * r
    first_is_low = i1 < i2
    e_lo = jnp.minimum(i1, i2) - first
    e_hi = jnp.maximum(i1, i2) - first
    return g_sel, e_lo, e_hi, jnp.where(first_is_low, w1, w2), jnp.where(first_is_low, w2, w1)


PAIRS_PER_GROUP = EXPERTS_PER_GROUP * (EXPERTS_PER_GROUP - 1) // 2
N_BINS = N_GROUPS * PAIRS_PER_GROUP
MOE_TILE = 256
W_LO_LANE, W_HI_LANE = 0, 64
META_ROWS = 8
DMA_LOOP_UNROLL = 8


def _bin_pairs():
    return [(lo, hi) for lo in range(EXPERTS_PER_GROUP) for hi in range(lo + 1, EXPERTS_PER_GROUP)]


def _moe_route_kernel(h_ref, wrh_ref, wrl_ref, br_ref, aug_ref, meta_ref, cnt_ref, run_ref):
    i = pl.program_id(0)

    @pl.when(i == 0)
    def _():
        run_ref[...] = jnp.zeros(run_ref.shape, F32)

    h = h_ref[...]
    rows, d = h.shape
    g_sel, e_lo, e_hi, w_lo, w_hi = _route(h, wrh_ref[...], wrl_ref[...], br_ref[...])
    pid = e_lo * (7.0 - e_lo) * 0.5 + e_hi - e_lo - 1.0
    bin_id = g_sel * PAIRS_PER_GROUP + pid
    lane = lax.broadcasted_iota(I32, (rows, LANES), 1).astype(F32)
    onehot = jnp.where(lane == bin_id, 1.0, 0.0)
    r_i = lax.broadcasted_iota(I32, (rows, rows), 0)
    c_i = lax.broadcasted_iota(I32, (rows, rows), 1)
    tri = jnp.where(c_i < r_i, 1.0, 0.0).astype(BF16)
    before = _dot(tri, onehot.astype(BF16)) + run_ref[...]
    rank = jnp.sum(before * onehot, axis=1, keepdims=True)
    run_ref[...] += jnp.sum(onehot, axis=0, keepdims=True)
    cnt_ref[...] = jnp.broadcast_to(run_ref[...], cnt_ref.shape)

    aug_ref[:, 0:d] = h
    aug_ref[:, d:d + LANES] = jnp.where(lane < W_HI_LANE, w_lo, w_hi)
    meta = jnp.where(lane == 0.0, bin_id, jnp.where(lane == 1.0, rank, 0.0))
    meta_ref[0] = meta.T[0:META_ROWS, :].astype(I32)


def _moe_route(h2, wr_hi, wr_lo, br):
    t, d = h2.shape
    nb = t // ROW_BLOCK
    row = lambda i: (i, 0)
    full = lambda i: (0, 0)
    return pl.pallas_call(
        _moe_route_kernel, grid=(nb,),
        in_specs=[pl.BlockSpec((ROW_BLOCK, d), row), pl.BlockSpec((d, LANES), full),
                  pl.BlockSpec((d, LANES), full), pl.BlockSpec((1, LANES), full)],
        out_specs=[pl.BlockSpec((ROW_BLOCK, d + LANES), row),
                   pl.BlockSpec((1, META_ROWS, ROW_BLOCK), lambda i: (i, 0, 0)),
                   pl.BlockSpec((8, LANES), full)],
        out_shape=[jax.ShapeDtypeStruct((t, d + LANES), F32),
                   jax.ShapeDtypeStruct((nb, META_ROWS, ROW_BLOCK), I32),
                   jax.ShapeDtypeStruct((8, LANES), F32)],
        scratch_shapes=[pltpu.VMEM((1, LANES), F32)],
        compiler_params=_params(1), name="moe_route",
    )(h2, wr_hi, wr_lo, br)


def _row_dma_loops(copy):
    def issue(r8, carry):
        for u in range(DMA_LOOP_UNROLL):
            copy(r8 * DMA_LOOP_UNROLL + u).start(priority=u % 2)
        return carry

    def drain(r8, carry):
        for u in range(DMA_LOOP_UNROLL):
            copy(r8 * DMA_LOOP_UNROLL + u).wait()
        return carry

    lax.fori_loop(0, ROW_BLOCK // DMA_LOOP_UNROLL, issue, 0)
    lax.fori_loop(0, ROW_BLOCK // DMA_LOOP_UNROLL, drain, 0)


def _moe_dispatch_kernel(off_ref, meta_ref, src_ref, init_ref, dst_ref, sem):
    del init_ref

    def copy(r):
        slot = off_ref[meta_ref[0, 0, r]] + meta_ref[0, 1, r]
        return pltpu.make_async_copy(src_ref.at[pl.ds(r, 1), :], dst_ref.at[pl.ds(slot, 1), :], sem)

    _row_dma_loops(copy)


def _moe_combine_kernel(off_ref, meta_ref, src_ref, dst_ref, sem):
    def copy(r):
        slot = off_ref[meta_ref[0, 0, r]] + meta_ref[0, 1, r]
        return pltpu.make_async_copy(src_ref.at[pl.ds(slot, 1), :], dst_ref.at[pl.ds(r, 1), :], sem)

    _row_dma_loops(copy)


def _moe_permute(off, meta, src, out_rows, init=None):
    nb = meta.shape[0]
    width = src.shape[1]
    dispatch = init is not None
    any_spec = pl.BlockSpec(memory_space=pl.ANY)
    block_spec = pl.BlockSpec((ROW_BLOCK, width), lambda i, off: (i, 0))
    meta_spec = pl.BlockSpec((1, META_ROWS, ROW_BLOCK), lambda i, off: (i, 0, 0), memory_space=pltpu.SMEM)
    grid_spec = pltpu.PrefetchScalarGridSpec(
        num_scalar_prefetch=1, grid=(nb,),
        in_specs=[meta_spec, block_spec, any_spec] if dispatch else [meta_spec, any_spec],
        out_specs=any_spec if dispatch else block_spec,
        scratch_shapes=[pltpu.SemaphoreType.DMA(())])
    return pl.pallas_call(
        _moe_dispatch_kernel if dispatch else _moe_combine_kernel, grid_spec=grid_spec,
        out_shape=jax.ShapeDtypeStruct((out_rows, width), src.dtype),
        input_output_aliases=({3: 0} if dispatch else {}),
        compiler_params=_params(1), name="moe_dispatch" if dispatch else "moe_combine",
    )(*([off, meta, src, init] if dispatch else [off, meta, src]))


def _moe_expert_kernel(e_lo_ref, e_hi_ref, nv_ref, s_ref, wg1_ref, wu1_ref, wd1_ref, wg2_ref, wu2_ref, wd2_ref,
                       g_ref, b_ref, o_ref, *, alpha):
    del e_lo_ref, e_hi_ref
    occupied = pl.program_id(0) < nv_ref[0]

    @pl.when(jnp.logical_not(occupied))
    def _():
        o_ref[...] = jnp.zeros(o_ref.shape, F32)

    @pl.when(occupied)
    def _():
        d = o_ref.shape[1]
        x = s_ref[:, 0:d]
        xb = x.astype(BF16)
        y = jnp.zeros(x.shape, F32)
        for lane0, wg_ref, wu_ref, wd_ref in ((W_LO_LANE, wg1_ref, wu1_ref, wd1_ref),
                                              (W_HI_LANE, wg2_ref, wu2_ref, wd2_ref)):
            c = s_ref[:, d + lane0:d + lane0 + 1]
            a = _dot(xb, wg_ref[0])
            u = _dot(xb, wu_ref[0])
            hid = a * (1.0 / (1.0 + jnp.exp(-a))) * u
            y = y + _dot((c * hid).astype(BF16), wd_ref[0])
        o_ref[...] = _layer_norm(alpha * x + y, g_ref[...], b_ref[...])


def _moe_experts(sorted_rows, tile_e_lo, tile_e_hi, n_valid, w_gate, w_up, w_down, g, b, alpha):
    rows, width = sorted_rows.shape
    d = width - LANES
    _, _, ff = w_gate.shape
    n_tiles = rows // MOE_TILE
    tile = lambda i, lo, hi, nv: (jnp.minimum(i, nv[0] - 1), 0)
    out_tile = lambda i, lo, hi, nv: (i, 0)
    w_lo = lambda i, lo, hi, nv: (lo[jnp.minimum(i, nv[0] - 1)], 0, 0)
    w_hi = lambda i, lo, hi, nv: (hi[jnp.minimum(i, nv[0] - 1)], 0, 0)
    full = lambda i, lo, hi, nv: (0, 0)
    grid_spec = pltpu.PrefetchScalarGridSpec(
        num_scalar_prefetch=3, grid=(n_tiles,),
        in_specs=[pl.BlockSpec((MOE_TILE, width), tile),
                  pl.BlockSpec((1, d, ff), w_lo), pl.BlockSpec((1, d, ff), w_lo), pl.BlockSpec((1, ff, d), w_lo),
                  pl.BlockSpec((1, d, ff), w_hi), pl.BlockSpec((1, d, ff), w_hi), pl.BlockSpec((1, ff, d), w_hi),
                  pl.BlockSpec((1, d), full), pl.BlockSpec((1, d), full)],
        out_specs=pl.BlockSpec((MOE_TILE, d), out_tile))
    return pl.pallas_call(
        functools.partial(_moe_expert_kernel, alpha=alpha), grid_spec=grid_spec,
        out_shape=jax.ShapeDtypeStruct((rows, d), F32),
        compiler_params=_params(1), name="moe_experts",
    )(tile_e_lo, tile_e_hi, n_valid, sorted_rows, w_gate, w_up, w_down, w_gate, w_up, w_down,
      g.reshape(1, d), b.reshape(1, d))


def _moe(h2, wr_hi, wr_lo, br, w_gate, w_up, w_down, g, b, alpha):
    t, d = h2.shape
    aug, meta, counts = _moe_route(h2, wr_hi, wr_lo, br)
    cnt = counts[0, 0:N_BINS].astype(I32)
    padded = (cnt + MOE_TILE - 1) // MOE_TILE * MOE_TILE
    ends = jnp.cumsum(padded)
    off = (ends - padded).astype(I32)
    n_tiles = t // MOE_TILE + N_BINS
    tile_bin = jnp.minimum(jnp.searchsorted(ends, jnp.arange(n_tiles, dtype=I32) * MOE_TILE, side="right"),
                           N_BINS - 1).astype(I32)
    pairs = jnp.asarray(_bin_pairs(), I32)
    group = tile_bin // PAIRS_PER_GROUP
    tile_e_lo = group * EXPERTS_PER_GROUP + pairs[tile_bin % PAIRS_PER_GROUP, 0]
    tile_e_hi = group * EXPERTS_PER_GROUP + pairs[tile_bin % PAIRS_PER_GROUP, 1]
    n_valid = (ends[-1:] // MOE_TILE).astype(I32)
    sorted_rows = _moe_permute(off, meta, aug, n_tiles * MOE_TILE,
                               init=jnp.zeros((n_tiles * MOE_TILE, d + LANES), F32))
    out_sorted = _moe_experts(sorted_rows, tile_e_lo, tile_e_hi, n_valid, w_gate, w_up, w_down, g, b, alpha)
    return _moe_permute(off, meta, out_sorted, t)


_COL_CQ, _COL_CKV, _COL_KROPE, _COL_KIDX, _COL_WIDX, _COL_END = 0, 256, 512, 640, 768, 896


def _dsa_proj_kernel(h_ref, win_ref, gq_ref, gkv_ref, wuq_ref, wqi_ref, wuk_ref,
                     cr_ref, sur_ref, sdr_ref, ci_ref, sui_ref, sdi_ref,
                     qft_ref, kv_ref, kvt_ref, qit_ref, ki_ref, wit_ref, *, q_scale, w_scale):
    y = _dot(h_ref[...].astype(BF16), win_ref[...])
    cr, sur, sdr = cr_ref[...], sur_ref[...], sdr_ref[...]
    ci, sui, sdi = ci_ref[...], sui_ref[...], sdi_ref[...]
    half_r, half_i = MLA_ROPE_DIM // 2, IDX_ROPE_DIM // 2
    qb = QUERY_BLOCK
    n_qb = y.shape[0] // qb
    c_q = _rms_norm(y[:, _COL_CQ:_COL_CKV], gq_ref[...]).astype(BF16)
    c_kv = _rms_norm(y[:, _COL_CKV:_COL_KROPE], gkv_ref[...])
    kv_ref[:, 0:MLA_KV_RANK] = c_kv.astype(BF16)
    kvt_ref[0] = c_kv.T.astype(BF16)
    kv_ref[:, MLA_KV_RANK:MLA_FEAT] = _rope(y[:, _COL_KROPE:_COL_KIDX], cr, sur, sdr, half_r).astype(BF16)
    ki_ref[...] = _rope(y[:, _COL_KIDX:_COL_WIDX], ci, sui, sdi, half_i).astype(BF16)
    wit_ref[...] = (y[:, _COL_WIDX:_COL_END] * w_scale).T[0:IDX_HEADS, :]
    q = _dot(c_q, wuq_ref[...])
    q_bf = q.astype(BF16)
    hd = MLA_ROPE_DIM + MLA_NOPE_DIM
    for hh in range(MLA_HEADS):
        lat_t = (_dot(q_bf[:, hh * hd:(hh + 1) * hd], wuk_ref[hh]) * q_scale).T.astype(BF16)
        rope_t = (_rope(q[:, hh * hd:(hh + 1) * hd], cr, sur, sdr, half_r) * q_scale).T.astype(BF16)
        for bl in range(n_qb):
            qft_ref[bl, 0:MLA_KV_RANK, hh * qb:(hh + 1) * qb] = lat_t[:, bl * qb:(bl + 1) * qb]
            qft_ref[bl, MLA_KV_RANK:MLA_FEAT, hh * qb:(hh + 1) * qb] = rope_t[:, bl * qb:(bl + 1) * qb]
    qi = _dot(c_q, wqi_ref[...])
    for p in range(IDX_HEADS // 2):
        pair_t = _rope(qi[:, p * LANES:(p + 1) * LANES], ci, sui, sdi, half_i).T.astype(BF16)
        for bl in range(n_qb):
            qit_ref[bl, p] = pair_t[:, bl * qb:(bl + 1) * qb]


def _dsa_proj(h2, w_in_p, gq, gkv, w_uq, w_qidx, w_ukt, tabs_r, tabs_i):
    t, d = h2.shape
    row = lambda i: (i, 0)
    full = lambda i: (0, 0)
    full3 = lambda i: (0, 0, 0)
    n_pairs = IDX_HEADS // 2
    qb = QUERY_BLOCK
    n_qb = ROW_BLOCK // qb
    kern = functools.partial(_dsa_proj_kernel, q_scale=(MLA_ROPE_DIM + MLA_NOPE_DIM) ** -0.5 * LOG2E,
                             w_scale=(IDX_HEADS * IDX_DIM) ** -0.5)
    tab = pl.BlockSpec((ROW_BLOCK, LANES), row)
    return pl.pallas_call(
        kern, grid=(t // ROW_BLOCK,),
        in_specs=[pl.BlockSpec((ROW_BLOCK, d), row), pl.BlockSpec(w_in_p.shape, full),
                  pl.BlockSpec((1, MLA_Q_RANK), full), pl.BlockSpec((1, MLA_KV_RANK), full),
                  pl.BlockSpec(w_uq.shape, full), pl.BlockSpec(w_qidx.shape, full),
                  pl.BlockSpec(w_ukt.shape, full3)] + [tab] * 6,
        out_specs=[pl.BlockSpec((n_qb, MLA_FEAT, MLA_HEADS * qb), lambda i: (i, 0, 0)),
                   pl.BlockSpec((ROW_BLOCK, MLA_FEAT), row),
                   pl.BlockSpec((1, MLA_KV_RANK, ROW_BLOCK), lambda i: (i, 0, 0)),
                   pl.BlockSpec((n_qb, n_pairs, LANES, qb), lambda i: (i, 0, 0, 0)),
                   pl.BlockSpec((ROW_BLOCK, LANES), row),
                   pl.BlockSpec((IDX_HEADS, ROW_BLOCK), lambda i: (0, i))],
        out_shape=[jax.ShapeDtypeStruct((t // qb, MLA_FEAT, MLA_HEADS * qb), BF16),
                   jax.ShapeDtypeStruct((t, MLA_FEAT), BF16),
                   jax.ShapeDtypeStruct((t // ROW_BLOCK, MLA_KV_RANK, ROW_BLOCK), BF16),
                   jax.ShapeDtypeStruct((t // qb, n_pairs, LANES, qb), BF16),
                   jax.ShapeDtypeStruct((t, LANES), BF16),
                   jax.ShapeDtypeStruct((IDX_HEADS, t), F32)],
        compiler_params=_params(1), name="dsa_proj",
    )(h2, w_in_p, gq.reshape(1, -1), gkv.reshape(1, -1), w_uq, w_qidx, w_ukt, *tabs_r, *tabs_i)


def _sortable_key(x):
    bits = lax.bitcast_convert_type(x, I32)
    return bits ^ ((bits >> 31) & 0x7FFFFFFF)


def _dsa_attn_kernel(qit_ref, wit_ref, qft_ref, ki_ref, kv_ref, kvt_ref, wuvt_ref, o_ref,
                     key_ref, thr_ref, m_ref, l_ref, acc_ref, *, kt, top_k):
    g = pl.program_id(1)
    qb = QUERY_BLOCK
    n_tiles = ((g + 1) * qb + kt - 1) // kt
    n_pairs = IDX_HEADS // 2
    lane_q = lax.broadcasted_iota(I32, (1, qb), 1)
    n_allowed = (g * (qb // CHUNK) + 1 + lane_q // CHUNK) * CHUNK

    feat = lax.broadcasted_iota(I32, (LANES, qb), 0)
    pair_w = []
    for p in range(n_pairs):
        slab = qit_ref[0, p]
        zero = jnp.zeros_like(slab)
        pair_w.append(jnp.concatenate([jnp.where(feat < IDX_DIM, slab, zero),
                                       jnp.where(feat >= IDX_DIM, slab, zero)], axis=1))
    wt = wit_ref[...]

    def score_tile(t, carry):
        lo, hi = carry
        k = ki_ref[pl.ds(pl.multiple_of(t * kt, kt), kt), :]
        sc = jnp.zeros((kt, qb), F32)
        for p in range(n_pairs):
            lg = _dot(k, pair_w[p])
            sc = (sc + jnp.maximum(lg[:, 0:qb], 0.0) * wt[2 * p:2 * p + 1, :]
                  + jnp.maximum(lg[:, qb:2 * qb], 0.0) * wt[2 * p + 1:2 * p + 2, :])
        kk = t * kt + lax.broadcasted_iota(I32, sc.shape, 0)
        valid = kk < n_allowed
        key_ref[t] = _sortable_key(jnp.where(valid, sc, NEG_INF))
        lo = jnp.minimum(lo, jnp.min(jnp.where(valid, sc, float("inf")), axis=0, keepdims=True))
        hi = jnp.maximum(hi, jnp.max(jnp.where(valid, sc, NEG_INF), axis=0, keepdims=True))
        return lo, hi

    lo_f, hi_f = lax.fori_loop(0, n_tiles, score_tile,
                               (jnp.full((1, qb), float("inf"), F32), jnp.full((1, qb), NEG_INF, F32)))

    keep_all = jnp.full((1, qb), NEG_INF_KEY + 1, I32)
    thr_ref[...] = keep_all

    @pl.when((g + 1) * qb > top_k)
    def _():
        def count_ge(mid):
            def body(t, cnt):
                hit = jnp.where(key_ref[t] >= mid, 1, 0)
                return cnt + jnp.sum(hit.reshape(kt // 8, 8, qb), axis=0)
            cnt = lax.fori_loop(0, n_tiles, body, jnp.zeros((8, qb), I32))
            return jnp.sum(cnt.astype(F32), axis=0, keepdims=True)

        def bisect(st):
            lo, hi = st
            for _ in range(BISECT_STEPS_PER_CHECK):
                mid = (lo >> 1) + (hi >> 1) + ((lo | hi) & 1)
                cnt = count_ge(mid)
                ge = cnt >= float(top_k)
                lo_next = jnp.where(ge, mid, lo)
                hi = jnp.where(cnt == float(top_k), mid, jnp.where(ge, hi, mid - 1))
                lo = lo_next
            return lo, hi

        def unresolved(st):
            lo, hi = st
            return jnp.max(jnp.where(hi > lo, 1.0, 0.0)) > 0.0

        lo, _ = lax.while_loop(unresolved, bisect, (_sortable_key(lo_f), _sortable_key(hi_f)))
        thr_ref[...] = jnp.where(n_allowed > top_k, lo, keep_all)

    thr = thr_ref[...]
    m_ref[...] = jnp.full(m_ref.shape, NEG_INF, F32)
    l_ref[...] = jnp.zeros(l_ref.shape, F32)
    acc_ref[...] = jnp.zeros(acc_ref.shape, F32)
    hpg = HEADS_PER_GROUP
    group = hpg * qb

    def scores(t, gi):
        kv_rows = kv_ref[pl.ds(pl.multiple_of(t * kt, kt), kt), :]
        bias = jnp.where(key_ref[t] >= thr, 0.0, NEG_INF)
        bias_g = jnp.concatenate([bias] * hpg, axis=1)
        return _dot(kv_rows, qft_ref[0, :, gi * group:(gi + 1) * group]) + bias_g

    def softmax_pv(t, gi, s):
        m_old = m_ref[gi]
        m_new = jnp.maximum(m_old, jnp.max(s, axis=0, keepdims=True))
        m_safe = jnp.where(m_new == NEG_INF, 0.0, m_new)
        p = jnp.exp2(s - m_safe)
        alpha = jnp.exp2(m_old - m_safe)
        l_ref[gi] = alpha * l_ref[gi] + jnp.sum(p, axis=0, keepdims=True)
        acc_ref[gi] = alpha * acc_ref[gi] + _dot(kvt_ref[t], p.astype(BF16))
        m_ref[gi] = m_new

    def run(tiles):
        items = [(t, gi) for t in tiles for gi in range(MLA_HEADS // hpg)]
        ahead = 2
        pending = {n: scores(*items[n]) for n in range(min(ahead, len(items)))}
        for n, item in enumerate(items):
            if n + ahead < len(items):
                pending[n + ahead] = scores(*items[n + ahead])
            softmax_pv(*item, pending.pop(n))

    def tile_pair(tt, carry):
        run([2 * tt, 2 * tt + 1])
        return carry

    lax.fori_loop(0, n_tiles // 2, tile_pair, 0)

    @pl.when(n_tiles % 2 == 1)
    def _():
        run([n_tiles - 1])

    for gi in range(MLA_HEADS // hpg):
        o_lat_t = (acc_ref[gi] * (1.0 / l_ref[gi])).astype(BF16)
        for hl in range(hpg):
            hh = gi * hpg + hl
            o_t = _dot(wuvt_ref[hh], o_lat_t[:, hl * qb:(hl + 1) * qb])
            o_ref[:, hh * MLA_V_DIM:(hh + 1) * MLA_V_DIM] = o_t.T.astype(BF16)


def _dsa_attn(qft, kv, kvt, qit, ki, wit, w_uvt, bsz, seq):
    kt = KEY_TILE
    qb = QUERY_BLOCK
    ng = seq // qb
    n_pairs = IDX_HEADS // 2
    top_k = min(IDX_TOPK_MAX, seq // 4)
    kern = functools.partial(_dsa_attn_kernel, kt=kt, top_k=top_k)
    batch2 = lambda b, g: (b, 0)
    n_groups, group = MLA_HEADS // HEADS_PER_GROUP, HEADS_PER_GROUP * qb
    return pl.pallas_call(
        kern, grid=(bsz, ng),
        in_specs=[pl.BlockSpec((1, n_pairs, LANES, qb), lambda b, g: (b * ng + g, 0, 0, 0)),
                  pl.BlockSpec((IDX_HEADS, qb), lambda b, g: (0, b * ng + g)),
                  pl.BlockSpec((1, MLA_FEAT, MLA_HEADS * qb), lambda b, g: (b * ng + g, 0, 0)),
                  pl.BlockSpec((seq, LANES), batch2),
                  pl.BlockSpec((seq, MLA_FEAT), batch2),
                  pl.BlockSpec((seq // kt, MLA_KV_RANK, kt), lambda b, g: (b, 0, 0)),
                  pl.BlockSpec(w_uvt.shape, lambda b, g: (0, 0, 0))],
        out_specs=pl.BlockSpec((qb, MLA_HEADS * MLA_V_DIM), lambda b, g: (b * ng + g, 0)),
        out_shape=jax.ShapeDtypeStruct((bsz * seq, MLA_HEADS * MLA_V_DIM), BF16),
        scratch_shapes=[pltpu.VMEM((seq // kt, kt, qb), I32), pltpu.VMEM((1, qb), I32),
                        pltpu.VMEM((n_groups, 1, group), F32), pltpu.VMEM((n_groups, 1, group), F32),
                        pltpu.VMEM((n_groups, MLA_KV_RANK, group), F32)],
        compiler_params=_params(2), name="dsa_attn",
    )(qit, wit, qft, ki, kv, kvt, w_uvt)


def _diff_lambda_init(layer_idx):
    import math
    return 0.8 - 0.6 * math.exp(-0.3 * layer_idx)


def _pad_dsa_w_in(w_in):
    d = w_in.shape[0]
    o_cq, o_ckv = 0, MLA_Q_RANK
    o_kr = o_ckv + MLA_KV_RANK
    o_ki = o_kr + MLA_ROPE_DIM
    o_wi = o_ki + IDX_DIM
    out = jnp.zeros((d, _COL_END), w_in.dtype)
    out = out.at[:, _COL_CQ:_COL_CQ + MLA_Q_RANK].set(w_in[:, o_cq:o_ckv])
    out = out.at[:, _COL_CKV:_COL_CKV + MLA_KV_RANK].set(w_in[:, o_ckv:o_kr])
    out = out.at[:, _COL_KROPE:_COL_KROPE + MLA_ROPE_DIM].set(w_in[:, o_kr:o_ki])
    out = out.at[:, _COL_KIDX:_COL_KIDX + IDX_DIM].set(w_in[:, o_ki:o_wi])
    out = out.at[:, _COL_KIDX + IDX_DIM:_COL_KIDX + 2 * IDX_DIM].set(w_in[:, o_ki:o_wi])
    out = out.at[:, _COL_WIDX:_COL_WIDX + IDX_HEADS].set(w_in[:, o_wi:o_wi + IDX_HEADS])
    return out.astype(BF16)


def _router_weights(w_group, b_group, w_expert, b_expert):
    d = w_group.shape[0]
    w = jnp.zeros((d, LANES), F32)
    w = w.at[:, 0:N_GROUPS].set(w_group).at[:, N_GROUPS:N_GROUPS + N_EXPERTS].set(w_expert)
    br = jnp.zeros((1, LANES), F32)
    br = br.at[0, 0:N_GROUPS].set(b_group).at[0, N_GROUPS:N_GROUPS + N_EXPERTS].set(b_expert)
    w_hi = w.astype(BF16)
    w_lo = (w - w_hi.astype(F32)).astype(BF16)
    return w_hi, w_lo, br


def kernel(x, mem, positions, a_w_in, a_lambda, a_subln_g, a_w_out, b_w_in, b_q_norm_g, b_kv_norm_g,
           b_w_uq, b_w_qidx, b_w_uk, b_w_uv, b_w_out, mem_w_kv, xa_w_q, xa_w_out,
           moe_w_group, moe_b_group, moe_w_expert, moe_b_expert, moe_w_gate, moe_w_up, moe_w_down,
           ln_g, ln_b):
    bsz, seq, d = x.shape
    n_mem = mem.shape[1]
    depth = ln_g.shape[0]
    alpha = (2.0 * depth) ** 0.25
    assert seq % ROW_BLOCK == 0 and seq % KEY_TILE == 0
    assert ROW_BLOCK == ATTN_BLOCK and ROW_BLOCK == KEY_TILE and QUERY_BLOCK == 2 * CHUNK

    tabs_d = _rope_lane_tables(positions, DIFF_ROPE_DIM, DIFF_HEAD_DIM, True)
    tabs_r = _rope_lane_tables(positions, MLA_ROPE_DIM, LANES, False)
    tabs_i = _rope_lane_tables(positions, IDX_ROPE_DIM, IDX_DIM, True)

    memkv = _mem_kv(mem.reshape(bsz * n_mem, d), mem_w_kv.astype(BF16))

    h = x.reshape(bsz * seq, d)
    for i in range(depth):
        j = i // N_MIXERS
        if i % N_MIXERS == 0:
            qt, k, vt = _diff_qkv(h, a_w_in[j].astype(BF16), tabs_d)
            mix = _diff_attn(qt, k, vt, a_lambda[j], a_subln_g[j], bsz, seq, _diff_lambda_init(i))
            w_mix_out = a_w_out[j]
        else:
            w_ukt = jnp.pad(jnp.swapaxes(b_w_uk[j], 1, 2), ((0, 0), (MLA_ROPE_DIM, 0), (0, 0))).astype(BF16)
            qft, kv, kvt, qit, ki, wit = _dsa_proj(h, _pad_dsa_w_in(b_w_in[j]), b_q_norm_g[j], b_kv_norm_g[j],
                                                   b_w_uq[j].astype(BF16), b_w_qidx[j].astype(BF16), w_ukt,
                                                   tabs_r, tabs_i)
            w_uvt = jnp.swapaxes(b_w_uv[j], 1, 2).astype(BF16)
            mix = _dsa_attn(qft, kv, kvt, qit, ki, wit, w_uvt, bsz, seq)
            w_mix_out = b_w_out[j]
        h = _proj_res_ln(mix, w_mix_out.astype(BF16), h, ln_g[i, 0], ln_b[i, 0], alpha)
        h = _cross_attn(h, xa_w_q[i].astype(BF16), memkv, xa_w_out[i].astype(BF16),
                        ln_g[i, 1], ln_b[i, 1], alpha, seq, n_mem)
        wr_hi, wr_lo, br = _router_weights(moe_w_group[i], moe_b_group[i], moe_w_expert[i], moe_b_expert[i])
        h = _moe(h, wr_hi, wr_lo, br, moe_w_gate[i].astype(BF16), moe_w_up[i].astype(BF16),
                 moe_w_down[i].astype(BF16), ln_g[i, 2], ln_b[i, 2], alpha)
    return h.reshape(bsz, seq, d)
```

```python
import functools

import jax
import jax.numpy as jnp
from jax import lax
from jax.experimental import pallas as pl
from jax.experimental.pallas import tpu as pltpu

F32 = jnp.float32
BF16 = jnp.bfloat16
I32 = jnp.int32

CHUNK = 64
ROPE_THETA = 500000.0
LN_EPS = 1e-5
N_MIXERS = 2
DIFF_HEADS = 8
DIFF_HEAD_DIM = 64
DIFF_ROPE_DIM = DIFF_HEAD_DIM // 4
MLA_HEADS = 8
MLA_Q_RANK = 256
MLA_KV_RANK = 256
MLA_ROPE_DIM = 32
MLA_NOPE_DIM = 96
MLA_V_DIM = 128
IDX_HEADS = 16
IDX_DIM = 64
IDX_ROPE_DIM = IDX_DIM // 4
IDX_TOPK_MAX = 256
MEM_HEADS = 4
N_GROUPS = 4
EXPERTS_PER_GROUP = 4
N_EXPERTS = N_GROUPS * EXPERTS_PER_GROUP

LANES = 128
VMEM_LIMIT_BYTES = 56 * 1024 * 1024

ROW_BLOCK = 512
ATTN_BLOCK = 512
KEY_TILE = 512
QUERY_BLOCK = 128
HEADS_PER_GROUP = 4
BISECT_STEPS_PER_CHECK = 4
BISECT_OPENING_FRACTIONS = (1.0 / 64.0, 0.25)
MLA_FEAT = MLA_KV_RANK + LANES

NEG_INF = float("-inf")
NEG_INF_KEY = -2139095041

_NT = (((1,), (1,)), ((), ()))


def _params(n_axes):
    return pltpu.CompilerParams(dimension_semantics=("arbitrary",) * n_axes,
                                vmem_limit_bytes=VMEM_LIMIT_BYTES)


def _dot(a, b):
    return jnp.dot(a, b, preferred_element_type=F32)


def _dot_nt(a, b):
    return lax.dot_general(a, b, _NT, preferred_element_type=F32)


def _layer_norm(z, g, b):
    mu = jnp.mean(z, axis=-1, keepdims=True)
    zc = z - mu
    var = jnp.mean(zc * zc, axis=-1, keepdims=True)
    return zc * lax.rsqrt(var + LN_EPS) * g + b


def _rms_norm(x, g):
    return x * lax.rsqrt(jnp.mean(x * x, axis=-1, keepdims=True) + LN_EPS) * g


def _rope(y, c, s_up, s_dn, half):
    return (y * c + pltpu.roll(y, LANES - half, 1) * s_up + pltpu.roll(y, half, 1) * s_dn)


def _rope_lane_tables(positions, rot_dim, period, keep_rest):
    half = rot_dim // 2
    inv_freq = ROPE_THETA ** (-jnp.arange(0, rot_dim, 2, dtype=F32) / rot_dim)
    ang = positions.astype(F32)[..., None] * inv_freq
    cos, sin = jnp.cos(ang), jnp.sin(ang)
    lead = positions.shape
    rest = jnp.full(lead + (period - rot_dim,), 1.0 if keep_rest else 0.0, F32)
    zrest = jnp.zeros(lead + (period - rot_dim,), F32)
    zhalf = jnp.zeros(lead + (half,), F32)
    reps = LANES // period
    out = []
    for parts in ((cos, cos, rest), (-sin, zhalf, zrest), (zhalf, sin, zrest)):
        t = jnp.concatenate(parts, axis=-1)
        out.append(jnp.tile(t, (1,) * len(lead) + (reps,)).reshape(-1, LANES))
    return out


LOG2E = 1.4426950408889634


def _diff_qkv_kernel(h_ref, w_ref, c_ref, su_ref, sd_ref, qt_ref, k_ref, vt_ref, *, q_scale, half):
    x = h_ref[...].astype(BF16)
    c, su, sd = c_ref[...], su_ref[...], sd_ref[...]
    n_heads = qt_ref.shape[0]
    for j2 in range(3 * n_heads // 2):
        y2 = _dot(x, w_ref[:, j2 * 2 * LANES:(j2 + 1) * 2 * LANES])
        for s in range(2):
            j = 2 * j2 + s
            y = y2[:, s * LANES:(s + 1) * LANES]
            if j < n_heads:
                qt_ref[j, 0] = (_rope(y, c, su, sd, half) * q_scale).T.astype(BF16)
            elif j < 2 * n_heads:
                k_ref[:, (j - n_heads) * LANES:(j - n_heads + 1) * LANES] = _rope(y, c, su, sd, half).astype(BF16)
            else:
                vt_ref[j - 2 * n_heads, 0] = y.T.astype(BF16)


def _diff_qkv(h2, w_in, tabs):
    t, d = h2.shape
    hd = 2 * DIFF_HEAD_DIM
    nb = t // ROW_BLOCK
    kern = functools.partial(_diff_qkv_kernel, q_scale=DIFF_HEAD_DIM ** -0.5 * LOG2E, half=DIFF_ROPE_DIM // 2)
    row = lambda i: (i, 0)
    full = lambda i: (0, 0)
    t_shape = jax.ShapeDtypeStruct((DIFF_HEADS, nb, hd, ROW_BLOCK), BF16)
    t_spec = pl.BlockSpec((DIFF_HEADS, 1, hd, ROW_BLOCK), lambda i: (0, i, 0, 0))
    return pl.pallas_call(
        kern, grid=(nb,),
        in_specs=[pl.BlockSpec((ROW_BLOCK, d), row), pl.BlockSpec(w_in.shape, full)]
        + [pl.BlockSpec((ROW_BLOCK, LANES), row)] * 3,
        out_specs=[t_spec, pl.BlockSpec((ROW_BLOCK, DIFF_HEADS * hd), row), t_spec],
        out_shape=[t_shape, jax.ShapeDtypeStruct((t, DIFF_HEADS * hd), BF16), t_shape],
        compiler_params=_params(1), name="diff_qkv",
    )(h2, w_in, *tabs)


def _diff_attn_kernel(lam_ref, g_ref, qt_ref, k_ref, vt_ref, o_ref, m_ref, l_ref, acc_ref, *,
                      blk, lambda_init):
    i = pl.program_id(2)
    lam = lam_ref[...]
    lam_full = (jnp.exp(jnp.sum(lam[0:1] * lam[1:2], axis=1, keepdims=True))
                - jnp.exp(jnp.sum(lam[2:3] * lam[3:4], axis=1, keepdims=True)) + lambda_init)
    qt = qt_ref[0, 0]
    feat = lax.broadcasted_iota(I32, qt.shape, 0)
    zero = jnp.zeros_like(qt)
    q_maps = (jnp.where(feat < DIFF_HEAD_DIM, qt, zero), jnp.where(feat >= DIFF_HEAD_DIM, qt, zero))
    m_ref[...] = jnp.full(m_ref.shape, NEG_INF, F32)
    l_ref[...] = jnp.zeros(l_ref.shape, F32)
    acc_ref[...] = jnp.zeros(acc_ref.shape, F32)

    def scores(j, c, masked):
        k = k_ref[pl.ds(pl.multiple_of(j * blk, blk), blk), :]
        s = _dot(k, q_maps[c])
        if masked:
            kc = lax.broadcasted_iota(I32, s.shape, 0) // CHUNK
            qc = lax.broadcasted_iota(I32, s.shape, 1) // CHUNK
            s = jnp.where(kc <= qc, s, NEG_INF)
        return s

    def softmax_pv(j, c, s):
        m_old = m_ref[c]
        m_new = jnp.maximum(m_old, jnp.max(s, axis=0, keepdims=True))
        p = jnp.exp2(s - m_new)
        alpha = jnp.exp2(m_old - m_new)
        l_ref[c] = alpha * l_ref[c] + jnp.sum(p, axis=0, keepdims=True)
        acc_ref[c] = alpha * acc_ref[c] + _dot(vt_ref[0, j], p.astype(BF16))
        m_ref[c] = m_new

    def run(blocks):
        items = [(j, c, masked) for (j, masked) in blocks for c in range(2)]
        ahead = 3
        pending = {n: scores(*items[n]) for n in range(min(ahead, len(items)))}
        for n, (j, c, _) in enumerate(items):
            if n + ahead < len(items):
                pending[n + ahead] = scores(*items[n + ahead])
            softmax_pv(j, c, pending.pop(n))

    def full_pair(jj, carry):
        run([(2 * jj, False), (2 * jj + 1, False)])
        return carry

    lax.fori_loop(0, i // 2, full_pair, 0)

    @pl.when(i % 2 == 1)
    def _():
        run([(i - 1, False), (i, True)])

    @pl.when(i % 2 == 0)
    def _():
        run([(i, True)])

    ot = acc_ref[0] * (1.0 / l_ref[0]) - lam_full * (acc_ref[1] * (1.0 / l_ref[1]))
    ot = ot * lax.rsqrt(jnp.mean(ot * ot, axis=0, keepdims=True) + LN_EPS) * (1.0 - lambda_init)
    g = g_ref[...]
    for s in range(blk // LANES):
        o_ref[s * LANES:(s + 1) * LANES, :] = (ot[:, s * LANES:(s + 1) * LANES] * g).T.astype(BF16)


def _diff_attn(qt, k, vt, lam, subln_g, bsz, seq, lambda_init):
    blk = ATTN_BLOCK
    nq = seq // blk
    hd = 2 * DIFF_HEAD_DIM
    kern = functools.partial(_diff_attn_kernel, blk=blk, lambda_init=lambda_init)
    g = jnp.broadcast_to(subln_g.astype(F32)[:, None], (hd, LANES))
    return pl.pallas_call(
        kern, grid=(bsz, DIFF_HEADS, nq),
        in_specs=[
            pl.BlockSpec(lam.shape, lambda b, h, i: (0, 0)),
            pl.BlockSpec((hd, LANES), lambda b, h, i: (0, 0)),
            pl.BlockSpec((1, 1, hd, blk), lambda b, h, i: (h, b * nq + i, 0, 0)),
            pl.BlockSpec((seq, hd), lambda b, h, i: (b, h)),
            pl.BlockSpec((1, nq, hd, blk), lambda b, h, i: (h, b, 0, 0)),
        ],
        out_specs=pl.BlockSpec((blk, hd), lambda b, h, i: (b * nq + i, h)),
        out_shape=jax.ShapeDtypeStruct((bsz * seq, DIFF_HEADS * hd), BF16),
        scratch_shapes=[pltpu.VMEM((2, 1, blk), F32), pltpu.VMEM((2, 1, blk), F32),
                        pltpu.VMEM((2, hd, blk), F32)],
        compiler_params=_params(3), name="diff_attn",
    )(lam, g, qt, k, vt)


def _proj_res_ln_kernel(a_ref, w_ref, h_ref, g_ref, b_ref, o_ref, *, alpha):
    z = alpha * h_ref[...] + _dot(a_ref[...], w_ref[...])
    o_ref[...] = _layer_norm(z, g_ref[...], b_ref[...])


def _proj_res_ln(a, w, h2, g, b, alpha):
    t, d = h2.shape
    k = a.shape[1]
    row = lambda i: (i, 0)
    full = lambda i: (0, 0)
    return pl.pallas_call(
        functools.partial(_proj_res_ln_kernel, alpha=alpha), grid=(t // ROW_BLOCK,),
        in_specs=[pl.BlockSpec((ROW_BLOCK, k), row), pl.BlockSpec((k, d), full),
                  pl.BlockSpec((ROW_BLOCK, d), row), pl.BlockSpec((1, d), full),
                  pl.BlockSpec((1, d), full)],
        out_specs=pl.BlockSpec((ROW_BLOCK, d), row),
        out_shape=jax.ShapeDtypeStruct((t, d), F32),
        compiler_params=_params(1), name="proj_res_ln",
    )(a, w, h2, g.reshape(1, d), b.reshape(1, d))


def _mem_kv_kernel(m_ref, w_ref, o_ref):
    o_ref[...] = _dot(m_ref[...].astype(BF16), w_ref[...]).astype(BF16)


def _mem_kv(mem2, w_kv):
    rows, d = mem2.shape
    n = w_kv.shape[1]
    blk = min(ROW_BLOCK, rows)
    return pl.pallas_call(
        _mem_kv_kernel, grid=(rows // blk,),
        in_specs=[pl.BlockSpec((blk, d), lambda i: (i, 0)), pl.BlockSpec((d, n), lambda i: (0, 0))],
        out_specs=pl.BlockSpec((blk, n), lambda i: (i, 0)),
        out_shape=jax.ShapeDtypeStruct((rows, n), BF16),
        compiler_params=_params(1), name="mem_kv",
    )(mem2, w_kv)


def _cross_attn_kernel(h_ref, wq_ref, kv_ref, wo_ref, g_ref, b_ref, o_ref, *, alpha, q_scale):
    h = h_ref[...]
    d = h.shape[1]
    hd = d // MEM_HEADS
    q = (_dot(h.astype(BF16), wq_ref[...]) * q_scale).astype(BF16)
    outs = []
    for hh in range(MEM_HEADS):
        s = _dot_nt(q[:, hh * hd:(hh + 1) * hd], kv_ref[:, hh * hd:(hh + 1) * hd])
        p = jnp.exp(s - jnp.max(s, axis=1, keepdims=True))
        p = p * (1.0 / jnp.sum(p, axis=1, keepdims=True))
        outs.append(_dot(p.astype(BF16), kv_ref[:, d + hh * hd:d + (hh + 1) * hd]).astype(BF16))
    o = jnp.concatenate(outs, axis=1)
    z = alpha * h + _dot(o, wo_ref[...])
    o_ref[...] = _layer_norm(z, g_ref[...], b_ref[...])


def _cross_attn(h2, w_q, memkv, w_out, g, b, alpha, seq, n_mem):
    t, d = h2.shape
    per_batch = seq // ROW_BLOCK
    row = lambda i: (i, 0)
    full = lambda i: (0, 0)
    kern = functools.partial(_cross_attn_kernel, alpha=alpha, q_scale=(d // MEM_HEADS) ** -0.5)
    return pl.pallas_call(
        kern, grid=(t // ROW_BLOCK,),
        in_specs=[pl.BlockSpec((ROW_BLOCK, d), row), pl.BlockSpec((d, d), full),
                  pl.BlockSpec((n_mem, 2 * d), lambda i: (i // per_batch, 0)),
                  pl.BlockSpec((d, d), full), pl.BlockSpec((1, d), full), pl.BlockSpec((1, d), full)],
        out_specs=pl.BlockSpec((ROW_BLOCK, d), row),
        out_shape=jax.ShapeDtypeStruct((t, d), F32),
        compiler_params=_params(1), name="cross_attn",
    )(h2, w_q, memkv, w_out, g.reshape(1, d), b.reshape(1, d))


def _route(h, wr_hi, wr_lo, br):
    h_hi = h.astype(BF16)
    h_lo = (h - h_hi.astype(F32)).astype(BF16)
    logits = _dot(h_hi, wr_hi) + _dot(h_hi, wr_lo) + _dot(h_lo, wr_hi) + br
    lane = lax.broadcasted_iota(I32, logits.shape, 1).astype(F32)
    gl = jnp.where(lane < N_GROUPS, logits, NEG_INF)
    gmax = jnp.max(gl, axis=1, keepdims=True)
    g_sel = jnp.min(jnp.where(gl == gmax, lane, float(LANES)), axis=1, keepdims=True)
    g_gate = 1.0 / jnp.sum(jnp.exp(gl - gmax), axis=1, keepdims=True)
    first = N_GROUPS + g_sel * EXPERTS_PER_GROUP
    el = jnp.where((lane >= first) & (lane < first + EXPERTS_PER_GROUP), logits, NEG_INF)
    v1 = jnp.max(el, axis=1, keepdims=True)
    i1 = jnp.min(jnp.where(el == v1, lane, float(LANES)), axis=1, keepdims=True)
    el2 = jnp.where(lane == i1, NEG_INF, el)
    v2 = jnp.max(el2, axis=1, keepdims=True)
    i2 = jnp.min(jnp.where(el2 == v2, lane, float(LANES)), axis=1, keepdims=True)
    r = jnp.exp(v2 - v1)
    w1 = g_gate / (1.0 + r)
    w2 = w1 * r
    first_is_low = i1 < i2
    e_lo = jnp.minimum(i1, i2) - first
    e_hi = jnp.maximum(i1, i2) - first
    return g_sel, e_lo, e_hi, jnp.where(first_is_low, w1, w2), jnp.where(first_is_low, w2, w1)


PAIRS_PER_GROUP = EXPERTS_PER_GROUP * (EXPERTS_PER_GROUP - 1) // 2
N_BINS = N_GROUPS * PAIRS_PER_GROUP
MOE_TILE = 256
W_LO_LANE, W_HI_LANE = 0, 64
META_ROWS = 8
DMA_LOOP_UNROLL = 8


def _bin_pairs():
    return [(lo, hi) for lo in range(EXPERTS_PER_GROUP) for hi in range(lo + 1, EXPERTS_PER_GROUP)]


def _moe_route_kernel(h_ref, wrh_ref, wrl_ref, br_ref, aug_ref, meta_ref, cnt_ref, run_ref):
    i = pl.program_id(0)

    @pl.when(i == 0)
    def _():
        run_ref[...] = jnp.zeros(run_ref.shape, F32)

    h = h_ref[...]
    rows, d = h.shape
    g_sel, e_lo, e_hi, w_lo, w_hi = _route(h, wrh_ref[...], wrl_ref[...], br_ref[...])
    pid = e_lo * (7.0 - e_lo) * 0.5 + e_hi - e_lo - 1.0
    bin_id = g_sel * PAIRS_PER_GROUP + pid
    lane = lax.broadcasted_iota(I32, (rows, LANES), 1).astype(F32)
    onehot = jnp.where(lane == bin_id, 1.0, 0.0)
    r_i = lax.broadcasted_iota(I32, (rows, rows), 0)
    c_i = lax.broadcasted_iota(I32, (rows, rows), 1)
    tri = jnp.where(c_i < r_i, 1.0, 0.0).astype(BF16)
    before = _dot(tri, onehot.astype(BF16)) + run_ref[...]
    rank = jnp.sum(before * onehot, axis=1, keepdims=True)
    run_ref[...] += jnp.sum(onehot, axis=0, keepdims=True)
    cnt_ref[...] = jnp.broadcast_to(run_ref[...], cnt_ref.shape)

    aug_ref[:, 0:d] = h
    aug_ref[:, d:d + LANES] = jnp.where(lane < W_HI_LANE, w_lo, w_hi)
    meta = jnp.where(lane == 0.0, bin_id, jnp.where(lane == 1.0, rank, 0.0))
    meta_ref[0] = meta.T[0:META_ROWS, :].astype(I32)


def _moe_route(h2, wr_hi, wr_lo, br):
    t, d = h2.shape
    nb = t // ROW_BLOCK
    row = lambda i: (i, 0)
    full = lambda i: (0, 0)
    return pl.pallas_call(
        _moe_route_kernel, grid=(nb,),
        in_specs=[pl.BlockSpec((ROW_BLOCK, d), row), pl.BlockSpec((d, LANES), full),
                  pl.BlockSpec((d, LANES), full), pl.BlockSpec((1, LANES), full)],
        out_specs=[pl.BlockSpec((ROW_BLOCK, d + LANES), row),
                   pl.BlockSpec((1, META_ROWS, ROW_BLOCK), lambda i: (i, 0, 0)),
                   pl.BlockSpec((8, LANES), full)],
        out_shape=[jax.ShapeDtypeStruct((t, d + LANES), F32),
                   jax.ShapeDtypeStruct((nb, META_ROWS, ROW_BLOCK), I32),
                   jax.ShapeDtypeStruct((8, LANES), F32)],
        scratch_shapes=[pltpu.VMEM((1, LANES), F32)],
        compiler_params=_params(1), name="moe_route",
    )(h2, wr_hi, wr_lo, br)


def _row_dma_loops(copy):
    def issue(r8, carry):
        for u in range(DMA_LOOP_UNROLL):
            copy(r8 * DMA_LOOP_UNROLL + u).start(priority=u % 2)
        return carry

    def drain(r8, carry):
        for u in range(DMA_LOOP_UNROLL):
            copy(r8 * DMA_LOOP_UNROLL + u).wait()
        return carry

    lax.fori_loop(0, ROW_BLOCK // DMA_LOOP_UNROLL, issue, 0)
    lax.fori_loop(0, ROW_BLOCK // DMA_LOOP_UNROLL, drain, 0)


def _moe_dispatch_kernel(off_ref, meta_ref, src_ref, init_ref, dst_ref, sem):
    del init_ref

    def copy(r):
        slot = off_ref[meta_ref[0, 0, r]] + meta_ref[0, 1, r]
        return pltpu.make_async_copy(src_ref.at[pl.ds(r, 1), :], dst_ref.at[pl.ds(slot, 1), :], sem)

    _row_dma_loops(copy)


def _moe_combine_kernel(off_ref, meta_ref, src_ref, dst_ref, sem):
    def copy(r):
        slot = off_ref[meta_ref[0, 0, r]] + meta_ref[0, 1, r]
        return pltpu.make_async_copy(src_ref.at[pl.ds(slot, 1), :], dst_ref.at[pl.ds(r, 1), :], sem)

    _row_dma_loops(copy)


def _moe_permute(off, meta, src, out_rows, init=None):
    nb = meta.shape[0]
    width = src.shape[1]
    dispatch = init is not None
    any_spec = pl.BlockSpec(memory_space=pl.ANY)
    block_spec = pl.BlockSpec((ROW_BLOCK, width), lambda i, off: (i, 0))
    meta_spec = pl.BlockSpec((1, META_ROWS, ROW_BLOCK), lambda i, off: (i, 0, 0), memory_space=pltpu.SMEM)
    grid_spec = pltpu.PrefetchScalarGridSpec(
        num_scalar_prefetch=1, grid=(nb,),
        in_specs=[meta_spec, block_spec, any_spec] if dispatch else [meta_spec, any_spec],
        out_specs=any_spec if dispatch else block_spec,
        scratch_shapes=[pltpu.SemaphoreType.DMA(())])
    return pl.pallas_call(
        _moe_dispatch_kernel if dispatch else _moe_combine_kernel, grid_spec=grid_spec,
        out_shape=jax.ShapeDtypeStruct((out_rows, width), src.dtype),
        input_output_aliases=({3: 0} if dispatch else {}),
        compiler_params=_params(1), name="moe_dispatch" if dispatch else "moe_combine",
    )(*([off, meta, src, init] if dispatch else [off, meta, src]))


def _moe_expert_kernel(e_lo_ref, e_hi_ref, nv_ref, s_ref, wg1_ref, wu1_ref, wd1_ref, wg2_ref, wu2_ref, wd2_ref,
                       g_ref, b_ref, o_ref, *, alpha):
    del e_lo_ref, e_hi_ref
    occupied = pl.program_id(0) < nv_ref[0]

    @pl.when(jnp.logical_not(occupied))
    def _():
        o_ref[...] = jnp.zeros(o_ref.shape, F32)

    @pl.when(occupied)
    def _():
        d = o_ref.shape[1]
        x = s_ref[:, 0:d]
        xb = x.astype(BF16)
        y = jnp.zeros(x.shape, F32)
        for lane0, wg_ref, wu_ref, wd_ref in ((W_LO_LANE, wg1_ref, wu1_ref, wd1_ref),
                                              (W_HI_LANE, wg2_ref, wu2_ref, wd2_ref)):
            c = s_ref[:, d + lane0:d + lane0 + 1]
            a = _dot(xb, wg_ref[0])
            u = _dot(xb, wu_ref[0])
            hid = a * (1.0 / (1.0 + jnp.exp(-a))) * u
            y = y + _dot((c * hid).astype(BF16), wd_ref[0])
        o_ref[...] = _layer_norm(alpha * x + y, g_ref[...], b_ref[...])


def _moe_experts(sorted_rows, tile_e_lo, tile_e_hi, n_valid, w_gate, w_up, w_down, g, b, alpha):
    rows, width = sorted_rows.shape
    d = width - LANES
    _, _, ff = w_gate.shape
    n_tiles = rows // MOE_TILE
    tile = lambda i, lo, hi, nv: (jnp.minimum(i, nv[0] - 1), 0)
    out_tile = lambda i, lo, hi, nv: (i, 0)
    w_lo = lambda i, lo, hi, nv: (lo[jnp.minimum(i, nv[0] - 1)], 0, 0)
    w_hi = lambda i, lo, hi, nv: (hi[jnp.minimum(i, nv[0] - 1)], 0, 0)
    full = lambda i, lo, hi, nv: (0, 0)
    grid_spec = pltpu.PrefetchScalarGridSpec(
        num_scalar_prefetch=3, grid=(n_tiles,),
        in_specs=[pl.BlockSpec((MOE_TILE, width), tile),
                  pl.BlockSpec((1, d, ff), w_lo), pl.BlockSpec((1, d, ff), w_lo), pl.BlockSpec((1, ff, d), w_lo),
                  pl.BlockSpec((1, d, ff), w_hi), pl.BlockSpec((1, d, ff), w_hi), pl.BlockSpec((1, ff, d), w_hi),
                  pl.BlockSpec((1, d), full), pl.BlockSpec((1, d), full)],
        out_specs=pl.BlockSpec((MOE_TILE, d), out_tile))
    return pl.pallas_call(
        functools.partial(_moe_expert_kernel, alpha=alpha), grid_spec=grid_spec,
        out_shape=jax.ShapeDtypeStruct((rows, d), F32),
        compiler_params=_params(1), name="moe_experts",
    )(tile_e_lo, tile_e_hi, n_valid, sorted_rows, w_gate, w_up, w_down, w_gate, w_up, w_down,
      g.reshape(1, d), b.reshape(1, d))


def _moe(h2, wr_hi, wr_lo, br, w_gate, w_up, w_down, g, b, alpha):
    t, d = h2.shape
    aug, meta, counts = _moe_route(h2, wr_hi, wr_lo, br)
    cnt = counts[0, 0:N_BINS].astype(I32)
    padded = (cnt + MOE_TILE - 1) // MOE_TILE * MOE_TILE
    ends = jnp.cumsum(padded)
    off = (ends - padded).astype(I32)
    n_tiles = t // MOE_TILE + N_BINS
    tile_start = jnp.arange(n_tiles, dtype=I32) * MOE_TILE
    tile_bin = jnp.minimum(jnp.sum((ends[None, :] <= tile_start[:, None]).astype(I32), axis=1), N_BINS - 1)
    pairs = jnp.asarray(_bin_pairs(), I32)
    group = tile_bin // PAIRS_PER_GROUP
    tile_e_lo = group * EXPERTS_PER_GROUP + pairs[tile_bin % PAIRS_PER_GROUP, 0]
    tile_e_hi = group * EXPERTS_PER_GROUP + pairs[tile_bin % PAIRS_PER_GROUP, 1]
    n_valid = (ends[-1:] // MOE_TILE).astype(I32)
    sorted_rows = _moe_permute(off, meta, aug, n_tiles * MOE_TILE,
                               init=jnp.zeros((n_tiles * MOE_TILE, d + LANES), F32))
    out_sorted = _moe_experts(sorted_rows, tile_e_lo, tile_e_hi, n_valid, w_gate, w_up, w_down, g, b, alpha)
    return _moe_permute(off, meta, out_sorted, t)


_COL_CQ, _COL_CKV, _COL_KROPE, _COL_KIDX, _COL_WIDX, _COL_END = 0, 256, 512, 640, 768, 896


def _dsa_proj_kernel(h_ref, win_ref, gq_ref, gkv_ref, wuq_ref, wqi_ref, wuk_ref,
                     cr_ref, sur_ref, sdr_ref, ci_ref, sui_ref, sdi_ref,
                     qft_ref, kv_ref, kvt_ref, qit_ref, ki_ref, wit_ref, *, q_scale, w_scale):
    y = _dot(h_ref[...].astype(BF16), win_ref[...])
    cr, sur, sdr = cr_ref[...], sur_ref[...], sdr_ref[...]
    ci, sui, sdi = ci_ref[...], sui_ref[...], sdi_ref[...]
    half_r, half_i = MLA_ROPE_DIM // 2, IDX_ROPE_DIM // 2
    qb = QUERY_BLOCK
    n_qb = y.shape[0] // qb
    c_q = _rms_norm(y[:, _COL_CQ:_COL_CKV], gq_ref[...]).astype(BF16)
    c_kv = _rms_norm(y[:, _COL_CKV:_COL_KROPE], gkv_ref[...])
    kv_ref[:, 0:MLA_KV_RANK] = c_kv.astype(BF16)
    kvt_ref[0] = c_kv.T.astype(BF16)
    kv_ref[:, MLA_KV_RANK:MLA_FEAT] = _rope(y[:, _COL_KROPE:_COL_KIDX], cr, sur, sdr, half_r).astype(BF16)
    ki_ref[...] = _rope(y[:, _COL_KIDX:_COL_WIDX], ci, sui, sdi, half_i).astype(BF16)
    wit_ref[...] = (y[:, _COL_WIDX:_COL_END] * w_scale).T[0:IDX_HEADS, :]
    q = _dot(c_q, wuq_ref[...])
    q_bf = q.astype(BF16)
    hd = MLA_ROPE_DIM + MLA_NOPE_DIM
    for hh in range(MLA_HEADS):
        lat_t = (_dot(q_bf[:, hh * hd:(hh + 1) * hd], wuk_ref[hh]) * q_scale).T.astype(BF16)
        rope_t = (_rope(q[:, hh * hd:(hh + 1) * hd], cr, sur, sdr, half_r) * q_scale).T.astype(BF16)
        for bl in range(n_qb):
            qft_ref[bl, 0:MLA_KV_RANK, hh * qb:(hh + 1) * qb] = lat_t[:, bl * qb:(bl + 1) * qb]
            qft_ref[bl, MLA_KV_RANK:MLA_FEAT, hh * qb:(hh + 1) * qb] = rope_t[:, bl * qb:(bl + 1) * qb]
    qi = _dot(c_q, wqi_ref[...])
    for p in range(IDX_HEADS // 2):
        pair_t = _rope(qi[:, p * LANES:(p + 1) * LANES], ci, sui, sdi, half_i).T.astype(BF16)
        for bl in range(n_qb):
            qit_ref[bl, p] = pair_t[:, bl * qb:(bl + 1) * qb]


def _dsa_proj(h2, w_in_p, gq, gkv, w_uq, w_qidx, w_ukt, tabs_r, tabs_i):
    t, d = h2.shape
    row = lambda i: (i, 0)
    full = lambda i: (0, 0)
    full3 = lambda i: (0, 0, 0)
    n_pairs = IDX_HEADS // 2
    qb = QUERY_BLOCK
    n_qb = ROW_BLOCK // qb
    kern = functools.partial(_dsa_proj_kernel, q_scale=(MLA_ROPE_DIM + MLA_NOPE_DIM) ** -0.5 * LOG2E,
                             w_scale=(IDX_HEADS * IDX_DIM) ** -0.5)
    tab = pl.BlockSpec((ROW_BLOCK, LANES), row)
    return pl.pallas_call(
        kern, grid=(t // ROW_BLOCK,),
        in_specs=[pl.BlockSpec((ROW_BLOCK, d), row), pl.BlockSpec(w_in_p.shape, full),
                  pl.BlockSpec((1, MLA_Q_RANK), full), pl.BlockSpec((1, MLA_KV_RANK), full),
                  pl.BlockSpec(w_uq.shape, full), pl.BlockSpec(w_qidx.shape, full),
                  pl.BlockSpec(w_ukt.shape, full3)] + [tab] * 6,
        out_specs=[pl.BlockSpec((n_qb, MLA_FEAT, MLA_HEADS * qb), lambda i: (i, 0, 0)),
                   pl.BlockSpec((ROW_BLOCK, MLA_FEAT), row),
                   pl.BlockSpec((1, MLA_KV_RANK, ROW_BLOCK), lambda i: (i, 0, 0)),
                   pl.BlockSpec((n_qb, n_pairs, LANES, qb), lambda i: (i, 0, 0, 0)),
                   pl.BlockSpec((ROW_BLOCK, LANES), row),
                   pl.BlockSpec((IDX_HEADS, ROW_BLOCK), lambda i: (0, i))],
        out_shape=[jax.ShapeDtypeStruct((t // qb, MLA_FEAT, MLA_HEADS * qb), BF16),
                   jax.ShapeDtypeStruct((t, MLA_FEAT), BF16),
                   jax.ShapeDtypeStruct((t // ROW_BLOCK, MLA_KV_RANK, ROW_BLOCK), BF16),
                   jax.ShapeDtypeStruct((t // qb, n_pairs, LANES, qb), BF16),
                   jax.ShapeDtypeStruct((t, LANES), BF16),
                   jax.ShapeDtypeStruct((IDX_HEADS, t), F32)],
        compiler_params=_params(1), name="dsa_proj",
    )(h2, w_in_p, gq.reshape(1, -1), gkv.reshape(1, -1), w_uq, w_qidx, w_ukt, *tabs_r, *tabs_i)


def _sortable_key(x):
    bits = lax.bitcast_convert_type(x, I32)
    return bits ^ ((bits >> 31) & 0x7FFFFFFF)


def _dsa_attn_kernel(qit_ref, wit_ref, qft_ref, ki_ref, kv_ref, kvt_ref, wuvt_ref, o_ref,
                     key_ref, thr_ref, m_ref, l_ref, acc_ref, *, kt, top_k):
    g = pl.program_id(1)
    qb = QUERY_BLOCK
    n_tiles = ((g + 1) * qb + kt - 1) // kt
    n_pairs = IDX_HEADS // 2
    lane_q = lax.broadcasted_iota(I32, (1, qb), 1)
    n_allowed = (g * (qb // CHUNK) + 1 + lane_q // CHUNK) * CHUNK

    feat = lax.broadcasted_iota(I32, (LANES, qb), 0)
    pair_w = []
    for p in range(n_pairs):
        slab = qit_ref[0, p]
        zero = jnp.zeros_like(slab)
        pair_w.append(jnp.concatenate([jnp.where(feat < IDX_DIM, slab, zero),
                                       jnp.where(feat >= IDX_DIM, slab, zero)], axis=1))
    wt = wit_ref[...]

    def score_tile(t, carry):
        lo, hi = carry
        k = ki_ref[pl.ds(pl.multiple_of(t * kt, kt), kt), :]
        sc = jnp.zeros((kt, qb), F32)
        for p in range(n_pairs):
            lg = _dot(k, pair_w[p])
            sc = (sc + jnp.maximum(lg[:, 0:qb], 0.0) * wt[2 * p:2 * p + 1, :]
                  + jnp.maximum(lg[:, qb:2 * qb], 0.0) * wt[2 * p + 1:2 * p + 2, :])
        kk = t * kt + lax.broadcasted_iota(I32, sc.shape, 0)
        valid = kk < n_allowed
        key_ref[t] = _sortable_key(jnp.where(valid, sc, NEG_INF))
        lo = jnp.minimum(lo, jnp.min(jnp.where(valid, sc, float("inf")), axis=0, keepdims=True))
        hi = jnp.maximum(hi, jnp.max(jnp.where(valid, sc, NEG_INF), axis=0, keepdims=True))
        return lo, hi

    lo_f, hi_f = lax.fori_loop(0, n_tiles, score_tile,
                               (jnp.full((1, qb), float("inf"), F32), jnp.full((1, qb), NEG_INF, F32)))

    keep_all = jnp.full((1, qb), NEG_INF_KEY + 1, I32)
    thr_ref[...] = keep_all

    @pl.when((g + 1) * qb > top_k)
    def _():
        def count_ge(mid):
            def body(t, cnt):
                hit = jnp.where(key_ref[t] >= mid, 1, 0)
                return cnt + jnp.sum(hit.reshape(kt // 8, 8, qb), axis=0)
            cnt = lax.fori_loop(0, n_tiles, body, jnp.zeros((8, qb), I32))
            return jnp.sum(cnt.astype(F32), axis=0, keepdims=True)

        def probe(lo, hi, mid):
            cnt = count_ge(mid)
            ge = cnt >= float(top_k)
            lo_next = jnp.where(ge, mid, lo)
            hi_next = jnp.where(cnt == float(top_k), mid, jnp.where(ge, hi, mid - 1))
            return lo_next, hi_next

        def bisect(st):
            lo, hi = st
            for _ in range(BISECT_STEPS_PER_CHECK):
                mid = (lo >> 1) + (hi >> 1) + ((lo | hi) & 1)
                lo, hi = probe(lo, hi, mid)
            return lo, hi

        def unresolved(st):
            lo, hi = st
            return jnp.max(jnp.where(hi > lo, 1.0, 0.0)) > 0.0

        lo, hi = _sortable_key(lo_f), _sortable_key(hi_f)
        for frac in BISECT_OPENING_FRACTIONS:
            split = jnp.minimum(jnp.maximum(_sortable_key(hi_f * frac), lo + 1), hi)
            lo, hi = probe(lo, hi, jnp.where(hi > lo, split, lo))
        lo, _ = lax.while_loop(unresolved, bisect, (lo, hi))
        thr_ref[...] = jnp.where(n_allowed > top_k, lo, keep_all)

    thr = thr_ref[...]
    m_ref[...] = jnp.full(m_ref.shape, NEG_INF, F32)
    l_ref[...] = jnp.zeros(l_ref.shape, F32)
    acc_ref[...] = jnp.zeros(acc_ref.shape, F32)
    hpg = HEADS_PER_GROUP
    group = hpg * qb

    def scores(t, gi):
        kv_rows = kv_ref[pl.ds(pl.multiple_of(t * kt, kt), kt), :]
        bias = jnp.where(key_ref[t] >= thr, 0.0, NEG_INF)
        bias_g = jnp.concatenate([bias] * hpg, axis=1)
        return _dot(kv_rows, qft_ref[0, :, gi * group:(gi + 1) * group]) + bias_g

    def softmax_pv(t, gi, s):
        m_old = m_ref[gi]
        m_new = jnp.maximum(m_old, jnp.max(s, axis=0, keepdims=True))
        m_safe = jnp.where(m_new == NEG_INF, 0.0, m_new)
        p = jnp.exp2(s - m_safe)
        alpha = jnp.exp2(m_old - m_safe)
        l_ref[gi] = alpha * l_ref[gi] + jnp.sum(p, axis=0, keepdims=True)
        acc_ref[gi] = alpha * acc_ref[gi] + _dot(kvt_ref[t], p.astype(BF16))
        m_ref[gi] = m_new

    def run(tiles):
        items = [(t, gi) for t in tiles for gi in range(MLA_HEADS // hpg)]
        ahead = 2
        pending = {n: scores(*items[n]) for n in range(min(ahead, len(items)))}
        for n, item in enumerate(items):
            if n + ahead < len(items):
                pending[n + ahead] = scores(*items[n + ahead])
            softmax_pv(*item, pending.pop(n))

    def tile_pair(tt, carry):
        run([2 * tt, 2 * tt + 1])
        return carry

    lax.fori_loop(0, n_tiles // 2, tile_pair, 0)

    @pl.when(n_tiles % 2 == 1)
    def _():
        run([n_tiles - 1])

    for gi in range(MLA_HEADS // hpg):
        o_lat_t = (acc_ref[gi] * (1.0 / l_ref[gi])).astype(BF16)
        for hl in range(hpg):
            hh = gi * hpg + hl
            o_t = _dot(wuvt_ref[hh], o_lat_t[:, hl * qb:(hl + 1) * qb])
            o_ref[:, hh * MLA_V_DIM:(hh + 1) * MLA_V_DIM] = o_t.T.astype(BF16)


def _dsa_attn(qft, kv, kvt, qit, ki, wit, w_uvt, bsz, seq):
    kt = KEY_TILE
    qb = QUERY_BLOCK
    ng = seq // qb
    n_pairs = IDX_HEADS // 2
    top_k = min(IDX_TOPK_MAX, seq // 4)
    kern = functools.partial(_dsa_attn_kernel, kt=kt, top_k=top_k)
    batch2 = lambda b, g: (b, 0)
    n_groups, group = MLA_HEADS // HEADS_PER_GROUP, HEADS_PER_GROUP * qb
    return pl.pallas_call(
        kern, grid=(bsz, ng),
        in_specs=[pl.BlockSpec((1, n_pairs, LANES, qb), lambda b, g: (b * ng + g, 0, 0, 0)),
                  pl.BlockSpec((IDX_HEADS, qb), lambda b, g: (0, b * ng + g)),
                  pl.BlockSpec((1, MLA_FEAT, MLA_HEADS * qb), lambda b, g: (b * ng + g, 0, 0)),
                  pl.BlockSpec((seq, LANES), batch2),
                  pl.BlockSpec((seq, MLA_FEAT), batch2),
                  pl.BlockSpec((seq // kt, MLA_KV_RANK, kt), lambda b, g: (b, 0, 0)),
                  pl.BlockSpec(w_uvt.shape, lambda b, g: (0, 0, 0))],
        out_specs=pl.BlockSpec((qb, MLA_HEADS * MLA_V_DIM), lambda b, g: (b * ng + g, 0)),
        out_shape=jax.ShapeDtypeStruct((bsz * seq, MLA_HEADS * MLA_V_DIM), BF16),
        scratch_shapes=[pltpu.VMEM((seq // kt, kt, qb), I32), pltpu.VMEM((1, qb), I32),
                        pltpu.VMEM((n_groups, 1, group), F32), pltpu.VMEM((n_groups, 1, group), F32),
                        pltpu.VMEM((n_groups, MLA_KV_RANK, group), F32)],
        compiler_params=_params(2), name="dsa_attn",
    )(qit, wit, qft, ki, kv, kvt, w_uvt)


def _diff_lambda_init(layer_idx):
    import math
    return 0.8 - 0.6 * math.exp(-0.3 * layer_idx)


def _pad_dsa_w_in(w_in):
    d = w_in.shape[0]
    o_cq, o_ckv = 0, MLA_Q_RANK
    o_kr = o_ckv + MLA_KV_RANK
    o_ki = o_kr + MLA_ROPE_DIM
    o_wi = o_ki + IDX_DIM
    out = jnp.zeros((d, _COL_END), w_in.dtype)
    out = out.at[:, _COL_CQ:_COL_CQ + MLA_Q_RANK].set(w_in[:, o_cq:o_ckv])
    out = out.at[:, _COL_CKV:_COL_CKV + MLA_KV_RANK].set(w_in[:, o_ckv:o_kr])
    out = out.at[:, _COL_KROPE:_COL_KROPE + MLA_ROPE_DIM].set(w_in[:, o_kr:o_ki])
    out = out.at[:, _COL_KIDX:_COL_KIDX + IDX_DIM].set(w_in[:, o_ki:o_wi])
    out = out.at[:, _COL_KIDX + IDX_DIM:_COL_KIDX + 2 * IDX_DIM].set(w_in[:, o_ki:o_wi])
    out = out.at[:, _COL_WIDX:_COL_WIDX + IDX_HEADS].set(w_in[:, o_wi:o_wi + IDX_HEADS])
    return out.astype(BF16)


def _router_weights(w_group, b_group, w_expert, b_expert):
    d = w_group.shape[0]
    w = jnp.zeros((d, LANES), F32)
    w = w.at[:, 0:N_GROUPS].set(w_group).at[:, N_GROUPS:N_GROUPS + N_EXPERTS].set(w_expert)
    br = jnp.zeros((1, LANES), F32)
    br = br.at[0, 0:N_GROUPS].set(b_group).at[0, N_GROUPS:N_GROUPS + N_EXPERTS].set(b_expert)
    w_hi = w.astype(BF16)
    w_lo = (w - w_hi.astype(F32)).astype(BF16)
    return w_hi, w_lo, br


def kernel(x, mem, positions, a_w_in, a_lambda, a_subln_g, a_w_out, b_w_in, b_q_norm_g, b_kv_norm_g,
           b_w_uq, b_w_qidx, b_w_uk, b_w_uv, b_w_out, mem_w_kv, xa_w_q, xa_w_out,
           moe_w_group, moe_b_group, moe_w_expert, moe_b_expert, moe_w_gate, moe_w_up, moe_w_down,
           ln_g, ln_b):
    bsz, seq, d = x.shape
    n_mem = mem.shape[1]
    depth = ln_g.shape[0]
    alpha = (2.0 * depth) ** 0.25
    assert seq % ROW_BLOCK == 0 and seq % KEY_TILE == 0
    assert ROW_BLOCK == ATTN_BLOCK and ROW_BLOCK == KEY_TILE and QUERY_BLOCK == 2 * CHUNK

    tabs_d = _rope_lane_tables(positions, DIFF_ROPE_DIM, DIFF_HEAD_DIM, True)
    tabs_r = _rope_lane_tables(positions, MLA_ROPE_DIM, LANES, False)
    tabs_i = _rope_lane_tables(positions, IDX_ROPE_DIM, IDX_DIM, True)

    memkv = _mem_kv(mem.reshape(bsz * n_mem, d), mem_w_kv.astype(BF16))

    h = x.reshape(bsz * seq, d)
    for i in range(depth):
        j = i // N_MIXERS
        if i % N_MIXERS == 0:
            qt, k, vt = _diff_qkv(h, a_w_in[j].astype(BF16), tabs_d)
            mix = _diff_attn(qt, k, vt, a_lambda[j], a_subln_g[j], bsz, seq, _diff_lambda_init(i))
            w_mix_out = a_w_out[j]
        else:
            w_ukt = jnp.pad(jnp.swapaxes(b_w_uk[j], 1, 2), ((0, 0), (MLA_ROPE_DIM, 0), (0, 0))).astype(BF16)
            qft, kv, kvt, qit, ki, wit = _dsa_proj(h, _pad_dsa_w_in(b_w_in[j]), b_q_norm_g[j], b_kv_norm_g[j],
                                                   b_w_uq[j].astype(BF16), b_w_qidx[j].astype(BF16), w_ukt,
                                                   tabs_r, tabs_i)
            w_uvt = jnp.swapaxes(b_w_uv[j], 1, 2).astype(BF16)
            mix = _dsa_attn(qft, kv, kvt, qit, ki, wit, w_uvt, bsz, seq)
            w_mix_out = b_w_out[j]
        h = _proj_res_ln(mix, w_mix_out.astype(BF16), h, ln_g[i, 0], ln_b[i, 0], alpha)
        h = _cross_attn(h, xa_w_q[i].astype(BF16), memkv, xa_w_out[i].astype(BF16),
                        ln_g[i, 1], ln_b[i, 1], alpha, seq, n_mem)
        wr_hi, wr_lo, br = _router_weights(moe_w_group[i], moe_b_group[i], moe_w_expert[i], moe_b_expert[i])
        h = _moe(h, wr_hi, wr_lo, br, moe_w_gate[i].astype(BF16), moe_w_up[i].astype(BF16),
                 moe_w_down[i].astype(BF16), ln_g[i, 2], ln_b[i, 2], alpha)
    return h.reshape(bsz, seq, d)
```

```python
import functools

import jax
import jax.numpy as jnp
from jax import lax
from jax.experimental import pallas as pl
from jax.experimental.pallas import tpu as pltpu

F32 = jnp.float32
BF16 = jnp.bfloat16
I32 = jnp.int32

CHUNK = 64
ROPE_THETA = 500000.0
LN_EPS = 1e-5
N_MIXERS = 2
DIFF_HEADS = 8
DIFF_HEAD_DIM = 64
DIFF_ROPE_DIM = DIFF_HEAD_DIM // 4
MLA_HEADS = 8
MLA_Q_RANK = 256
MLA_KV_RANK = 256
MLA_ROPE_DIM = 32
MLA_NOPE_DIM = 96
MLA_V_DIM = 128
IDX_HEADS = 16
IDX_DIM = 64
IDX_ROPE_DIM = IDX_DIM // 4
IDX_TOPK_MAX = 256
MEM_HEADS = 4
N_GROUPS = 4
EXPERTS_PER_GROUP = 4
N_EXPERTS = N_GROUPS * EXPERTS_PER_GROUP

LANES = 128
VMEM_LIMIT_BYTES = 56 * 1024 * 1024

ROW_BLOCK = 512
ATTN_BLOCK = 512
KEY_TILE = 512
QUERY_BLOCK = 128
HEADS_PER_GROUP = 2
BISECT_STEPS_PER_CHECK = 4
BISECT_OPENING_FRACTIONS = (1.0 / 64.0, 0.25)
MLA_FEAT = MLA_KV_RANK + LANES

NEG_INF = float("-inf")
NEG_INF_KEY = -2139095041

_NT = (((1,), (1,)), ((), ()))


def _params(n_axes):
    return pltpu.CompilerParams(dimension_semantics=("arbitrary",) * n_axes,
                                vmem_limit_bytes=VMEM_LIMIT_BYTES)


def _dot(a, b):
    return jnp.dot(a, b, preferred_element_type=F32)


def _dot_nt(a, b):
    return lax.dot_general(a, b, _NT, preferred_element_type=F32)


def _layer_norm(z, g, b):
    mu = jnp.mean(z, axis=-1, keepdims=True)
    zc = z - mu
    var = jnp.mean(zc * zc, axis=-1, keepdims=True)
    return zc * lax.rsqrt(var + LN_EPS) * g + b


def _rms_norm(x, g):
    return x * lax.rsqrt(jnp.mean(x * x, axis=-1, keepdims=True) + LN_EPS) * g


def _rope(y, c, s_up, s_dn, half):
    return (y * c + pltpu.roll(y, LANES - half, 1) * s_up + pltpu.roll(y, half, 1) * s_dn)


def _rope_lane_tables(positions, rot_dim, period, keep_rest):
    half = rot_dim // 2
    inv_freq = ROPE_THETA ** (-jnp.arange(0, rot_dim, 2, dtype=F32) / rot_dim)
    ang = positions.astype(F32)[..., None] * inv_freq
    cos, sin = jnp.cos(ang), jnp.sin(ang)
    lead = positions.shape
    rest = jnp.full(lead + (period - rot_dim,), 1.0 if keep_rest else 0.0, F32)
    zrest = jnp.zeros(lead + (period - rot_dim,), F32)
    zhalf = jnp.zeros(lead + (half,), F32)
    reps = LANES // period
    out = []
    for parts in ((cos, cos, rest), (-sin, zhalf, zrest), (zhalf, sin, zrest)):
        t = jnp.concatenate(parts, axis=-1)
        out.append(jnp.tile(t, (1,) * len(lead) + (reps,)).reshape(-1, LANES))
    return out


LOG2E = 1.4426950408889634


def _diff_qkv_kernel(h_ref, w_ref, c_ref, su_ref, sd_ref, qt_ref, k_ref, vt_ref, *, q_scale, half):
    x = h_ref[...].astype(BF16)
    c, su, sd = c_ref[...], su_ref[...], sd_ref[...]
    n_heads = qt_ref.shape[0]
    for j2 in range(3 * n_heads // 2):
        y2 = _dot(x, w_ref[:, j2 * 2 * LANES:(j2 + 1) * 2 * LANES])
        for s in range(2):
            j = 2 * j2 + s
            y = y2[:, s * LANES:(s + 1) * LANES]
            if j < n_heads:
                qt_ref[j, 0] = (_rope(y, c, su, sd, half) * q_scale).T.astype(BF16)
            elif j < 2 * n_heads:
                k_ref[:, (j - n_heads) * LANES:(j - n_heads + 1) * LANES] = _rope(y, c, su, sd, half).astype(BF16)
            else:
                vt_ref[j - 2 * n_heads, 0] = y.T.astype(BF16)


def _diff_qkv(h2, w_in, tabs):
    t, d = h2.shape
    hd = 2 * DIFF_HEAD_DIM
    nb = t // ROW_BLOCK
    kern = functools.partial(_diff_qkv_kernel, q_scale=DIFF_HEAD_DIM ** -0.5 * LOG2E, half=DIFF_ROPE_DIM // 2)
    row = lambda i: (i, 0)
    full = lambda i: (0, 0)
    t_shape = jax.ShapeDtypeStruct((DIFF_HEADS, nb, hd, ROW_BLOCK), BF16)
    t_spec = pl.BlockSpec((DIFF_HEADS, 1, hd, ROW_BLOCK), lambda i: (0, i, 0, 0))
    return pl.pallas_call(
        kern, grid=(nb,),
        in_specs=[pl.BlockSpec((ROW_BLOCK, d), row), pl.BlockSpec(w_in.shape, full)]
        + [pl.BlockSpec((ROW_BLOCK, LANES), row)] * 3,
        out_specs=[t_spec, pl.BlockSpec((ROW_BLOCK, DIFF_HEADS * hd), row), t_spec],
        out_shape=[t_shape, jax.ShapeDtypeStruct((t, DIFF_HEADS * hd), BF16), t_shape],
        compiler_params=_params(1), name="diff_qkv",
    )(h2, w_in, *tabs)


def _diff_attn_kernel(lam_ref, g_ref, qt_ref, k_ref, vt_ref, o_ref, m_ref, l_ref, acc_ref, *,
                      blk, lambda_init):
    i = pl.program_id(2)
    lam = lam_ref[...]
    lam_full = (jnp.exp(jnp.sum(lam[0:1] * lam[1:2], axis=1, keepdims=True))
                - jnp.exp(jnp.sum(lam[2:3] * lam[3:4], axis=1, keepdims=True)) + lambda_init)
    qt = qt_ref[0, 0]
    feat = lax.broadcasted_iota(I32, qt.shape, 0)
    zero = jnp.zeros_like(qt)
    q_maps = (jnp.where(feat < DIFF_HEAD_DIM, qt, zero), jnp.where(feat >= DIFF_HEAD_DIM, qt, zero))
    m_ref[...] = jnp.full(m_ref.shape, NEG_INF, F32)
    l_ref[...] = jnp.zeros(l_ref.shape, F32)
    acc_ref[...] = jnp.zeros(acc_ref.shape, F32)

    half_w = blk // 2

    def scores(j, c, hf, masked):
        k = k_ref[pl.ds(pl.multiple_of(j * blk, blk), blk), :]
        s = _dot(k, q_maps[c][:, hf * half_w:(hf + 1) * half_w])
        if masked:
            kc = lax.broadcasted_iota(I32, s.shape, 0) // CHUNK
            qc = (hf * half_w + lax.broadcasted_iota(I32, s.shape, 1)) // CHUNK
            s = jnp.where(kc <= qc, s, NEG_INF)
        return s

    def softmax_pv(j, c, hf, s):
        idx = 2 * c + hf
        m_old = m_ref[idx]
        m_new = jnp.maximum(m_old, jnp.max(s, axis=0, keepdims=True))
        p = jnp.exp2(s - m_new)
        alpha = jnp.exp2(m_old - m_new)
        l_ref[idx] = alpha * l_ref[idx] + jnp.sum(p, axis=0, keepdims=True)
        acc_ref[idx] = alpha * acc_ref[idx] + _dot(vt_ref[0, j], p.astype(BF16))
        m_ref[idx] = m_new

    def run(blocks):
        items = [(j, c, hf, masked) for (j, masked) in blocks for c in range(2) for hf in range(2)]
        ahead = 4
        pending = {n: scores(*items[n]) for n in range(min(ahead, len(items)))}
        for n, (j, c, hf, _) in enumerate(items):
            if n + ahead < len(items):
                pending[n + ahead] = scores(*items[n + ahead])
            softmax_pv(j, c, hf, pending.pop(n))

    def full_pair(jj, carry):
        run([(2 * jj, False), (2 * jj + 1, False)])
        return carry

    lax.fori_loop(0, i // 2, full_pair, 0)

    @pl.when(i % 2 == 1)
    def _():
        run([(i - 1, False), (i, True)])

    @pl.when(i % 2 == 0)
    def _():
        run([(i, True)])

    norm = [acc_ref[idx] * (1.0 / l_ref[idx]) for idx in range(4)]
    ot = (jnp.concatenate(norm[0:2], axis=1) - lam_full * jnp.concatenate(norm[2:4], axis=1))
    ot = ot * lax.rsqrt(jnp.mean(ot * ot, axis=0, keepdims=True) + LN_EPS) * (1.0 - lambda_init)
    g = g_ref[...]
    for s in range(blk // LANES):
        o_ref[s * LANES:(s + 1) * LANES, :] = (ot[:, s * LANES:(s + 1) * LANES] * g).T.astype(BF16)


def _diff_attn(qt, k, vt, lam, subln_g, bsz, seq, lambda_init):
    blk = ATTN_BLOCK
    nq = seq // blk
    hd = 2 * DIFF_HEAD_DIM
    kern = functools.partial(_diff_attn_kernel, blk=blk, lambda_init=lambda_init)
    g = jnp.broadcast_to(subln_g.astype(F32)[:, None], (hd, LANES))
    return pl.pallas_call(
        kern, grid=(bsz, DIFF_HEADS, nq),
        in_specs=[
            pl.BlockSpec(lam.shape, lambda b, h, i: (0, 0)),
            pl.BlockSpec((hd, LANES), lambda b, h, i: (0, 0)),
            pl.BlockSpec((1, 1, hd, blk), lambda b, h, i: (h, b * nq + i, 0, 0)),
            pl.BlockSpec((seq, hd), lambda b, h, i: (b, h)),
            pl.BlockSpec((1, nq, hd, blk), lambda b, h, i: (h, b, 0, 0)),
        ],
        out_specs=pl.BlockSpec((blk, hd), lambda b, h, i: (b * nq + i, h)),
        out_shape=jax.ShapeDtypeStruct((bsz * seq, DIFF_HEADS * hd), BF16),
        scratch_shapes=[pltpu.VMEM((4, 1, blk // 2), F32), pltpu.VMEM((4, 1, blk // 2), F32),
                        pltpu.VMEM((4, hd, blk // 2), F32)],
        compiler_params=_params(3), name="diff_attn",
    )(lam, g, qt, k, vt)


def _mem_kv_kernel(m_ref, w_ref, o_ref):
    o_ref[...] = _dot(m_ref[...].astype(BF16), w_ref[...]).astype(BF16)


def _mem_kv(mem2, w_kv):
    rows, d = mem2.shape
    n = w_kv.shape[1]
    blk = min(ROW_BLOCK, rows)
    return pl.pallas_call(
        _mem_kv_kernel, grid=(rows // blk,),
        in_specs=[pl.BlockSpec((blk, d), lambda i: (i, 0)), pl.BlockSpec((d, n), lambda i: (0, 0))],
        out_specs=pl.BlockSpec((blk, n), lambda i: (i, 0)),
        out_shape=jax.ShapeDtypeStruct((rows, n), BF16),
        compiler_params=_params(1), name="mem_kv",
    )(mem2, w_kv)


def _cross_attn_block(h, wq_ref, kv_ref, wo_ref, g, b, alpha):
    d = h.shape[1]
    hd = d // MEM_HEADS
    q = (_dot(h.astype(BF16), wq_ref[...]) * hd ** -0.5).astype(BF16)
    outs = []
    for hh in range(MEM_HEADS):
        s = _dot_nt(q[:, hh * hd:(hh + 1) * hd], kv_ref[:, hh * hd:(hh + 1) * hd])
        p = jnp.exp(s - jnp.max(s, axis=1, keepdims=True))
        p = p * (1.0 / jnp.sum(p, axis=1, keepdims=True))
        outs.append(_dot(p.astype(BF16), kv_ref[:, d + hh * hd:d + (hh + 1) * hd]).astype(BF16))
    o = jnp.concatenate(outs, axis=1)
    return _layer_norm(alpha * h + _dot(o, wo_ref[...]), g, b)


def _route(h, wr_hi, wr_lo, br):
    h_hi = h.astype(BF16)
    h_lo = (h - h_hi.astype(F32)).astype(BF16)
    logits = _dot(h_hi, wr_hi) + _dot(h_hi, wr_lo) + _dot(h_lo, wr_hi) + br
    lane = lax.broadcasted_iota(I32, logits.shape, 1).astype(F32)
    gl = jnp.where(lane < N_GROUPS, logits, NEG_INF)
    gmax = jnp.max(gl, axis=1, keepdims=True)
    g_sel = jnp.min(jnp.where(gl == gmax, lane, float(LANES)), axis=1, keepdims=True)
    g_gate = 1.0 / jnp.sum(jnp.exp(gl - gmax), axis=1, keepdims=True)
    first = N_GROUPS + g_sel * EXPERTS_PER_GROUP
    el = jnp.where((lane >= first) & (lane < first + EXPERTS_PER_GROUP), logits, NEG_INF)
    v1 = jnp.max(el, axis=1, keepdims=True)
    i1 = jnp.min(jnp.where(el == v1, lane, float(LANES)), axis=1, keepdims=True)
    el2 = jnp.where(lane == i1, NEG_INF, el)
    v2 = jnp.max(el2, axis=1, keepdims=True)
    i2 = jnp.min(jnp.where(el2 == v2, lane, float(LANES)), axis=1, keepdims=True)
    r = jnp.exp(v2 - v1)
    w1 = g_gate / (1.0 + r)
    w2 = w1 * r
    first_is_low = i1 < i2
    e_lo = jnp.minimum(i1, i2) - first
    e_hi = jnp.maximum(i1, i2) - first
    return g_sel, e_lo, e_hi, jnp.where(first_is_low, w1, w2), jnp.where(first_is_low, w2, w1)


PAIRS_PER_GROUP = EXPERTS_PER_GROUP * (EXPERTS_PER_GROUP - 1) // 2
N_BINS = N_GROUPS * PAIRS_PER_GROUP
MOE_TILE = 256
W_LO_LANE, W_HI_LANE = 0, 64
META_ROWS = 8
DMA_LOOP_UNROLL = 8


def _bin_pairs():
    return [(lo, hi) for lo in range(EXPERTS_PER_GROUP) for hi in range(lo + 1, EXPERTS_PER_GROUP)]


def _route_block(h, wrh_ref, wrl_ref, br_ref, aug_ref, meta_ref, cnt_ref, run_ref):
    rows, d = h.shape
    g_sel, e_lo, e_hi, w_lo, w_hi = _route(h, wrh_ref[...], wrl_ref[...], br_ref[...])
    pid = e_lo * (7.0 - e_lo) * 0.5 + e_hi - e_lo - 1.0
    bin_id = g_sel * PAIRS_PER_GROUP + pid
    lane = lax.broadcasted_iota(I32, (rows, LANES), 1).astype(F32)
    onehot = jnp.where(lane == bin_id, 1.0, 0.0)
    r_i = lax.broadcasted_iota(I32, (rows, rows), 0)
    c_i = lax.broadcasted_iota(I32, (rows, rows), 1)
    tri = jnp.where(c_i < r_i, 1.0, 0.0).astype(BF16)
    before = _dot(tri, onehot.astype(BF16)) + run_ref[...]
    rank = jnp.sum(before * onehot, axis=1, keepdims=True)
    run_ref[...] += jnp.sum(onehot, axis=0, keepdims=True)
    cnt_ref[...] = jnp.broadcast_to(run_ref[...], cnt_ref.shape)

    aug_ref[:, 0:d] = h
    aug_ref[:, d:d + LANES] = jnp.where(lane < W_HI_LANE, w_lo, w_hi)
    meta = jnp.where(lane == 0.0, bin_id, jnp.where(lane == 1.0, rank, 0.0))
    meta_ref[0] = meta.T[0:META_ROWS, :].astype(I32)


def _layer_mid_kernel(mix_ref, wmix_ref, h_ref, ln_ref, wq_ref, kv_ref, wo_ref, wrh_ref, wrl_ref, br_ref,
                      aug_ref, meta_ref, cnt_ref, run_ref, *, alpha):
    @pl.when(pl.program_id(0) == 0)
    def _():
        run_ref[...] = jnp.zeros(run_ref.shape, F32)

    ln = ln_ref[...]
    h1 = _layer_norm(alpha * h_ref[...] + _dot(mix_ref[...], wmix_ref[...]), ln[0:1], ln[1:2])
    h2 = _cross_attn_block(h1, wq_ref, kv_ref, wo_ref, ln[2:3], ln[3:4], alpha)
    _route_block(h2, wrh_ref, wrl_ref, br_ref, aug_ref, meta_ref, cnt_ref, run_ref)


def _layer_mid(mix, w_mix, h2, ln_rows, w_q, memkv, w_out, wr_hi, wr_lo, br, alpha, seq, n_mem):
    t, d = h2.shape
    k = mix.shape[1]
    nb = t // ROW_BLOCK
    per_batch = seq // ROW_BLOCK
    row = lambda i: (i, 0)
    full = lambda i: (0, 0)
    return pl.pallas_call(
        functools.partial(_layer_mid_kernel, alpha=alpha), grid=(nb,),
        in_specs=[pl.BlockSpec((ROW_BLOCK, k), row), pl.BlockSpec((k, d), full),
                  pl.BlockSpec((ROW_BLOCK, d), row), pl.BlockSpec(ln_rows.shape, full),
                  pl.BlockSpec((d, d), full), pl.BlockSpec((n_mem, 2 * d), lambda i: (i // per_batch, 0)),
                  pl.BlockSpec((d, d), full), pl.BlockSpec((d, LANES), full),
                  pl.BlockSpec((d, LANES), full), pl.BlockSpec((1, LANES), full)],
        out_specs=[pl.BlockSpec((ROW_BLOCK, d + LANES), row),
                   pl.BlockSpec((1, META_ROWS, ROW_BLOCK), lambda i: (i, 0, 0)),
                   pl.BlockSpec((8, LANES), full)],
        out_shape=[jax.ShapeDtypeStruct((t, d + LANES), F32),
                   jax.ShapeDtypeStruct((nb, META_ROWS, ROW_BLOCK), I32),
                   jax.ShapeDtypeStruct((8, LANES), F32)],
        scratch_shapes=[pltpu.VMEM((1, LANES), F32)],
        compiler_params=_params(1), name="layer_mid",
    )(mix, w_mix, h2, ln_rows, w_q, memkv, w_out, wr_hi, wr_lo, br)


def _row_dma_loops(copy):
    def issue(r8, carry):
        for u in range(DMA_LOOP_UNROLL):
            copy(r8 * DMA_LOOP_UNROLL + u).start(priority=u % 2)
        return carry

    def drain(r8, carry):
        for u in range(DMA_LOOP_UNROLL):
            copy(r8 * DMA_LOOP_UNROLL + u).wait()
        return carry

    lax.fori_loop(0, ROW_BLOCK // DMA_LOOP_UNROLL, issue, 0)
    lax.fori_loop(0, ROW_BLOCK // DMA_LOOP_UNROLL, drain, 0)


def _moe_dispatch_kernel(off_ref, meta_ref, src_ref, init_ref, dst_ref, sem):
    del init_ref

    def copy(r):
        slot = off_ref[meta_ref[0, 0, r]] + meta_ref[0, 1, r]
        return pltpu.make_async_copy(src_ref.at[pl.ds(r, 1), :], dst_ref.at[pl.ds(slot, 1), :], sem)

    _row_dma_loops(copy)


def _moe_combine_kernel(off_ref, meta_ref, src_ref, dst_ref, sem):
    def copy(r):
        slot = off_ref[meta_ref[0, 0, r]] + meta_ref[0, 1, r]
        return pltpu.make_async_copy(src_ref.at[pl.ds(slot, 1), :], dst_ref.at[pl.ds(r, 1), :], sem)

    _row_dma_loops(copy)


def _moe_permute(off, meta, src, out_rows, init=None):
    nb = meta.shape[0]
    width = src.shape[1]
    dispatch = init is not None
    any_spec = pl.BlockSpec(memory_space=pl.ANY)
    block_spec = pl.BlockSpec((ROW_BLOCK, width), lambda i, off: (i, 0))
    meta_spec = pl.BlockSpec((1, META_ROWS, ROW_BLOCK), lambda i, off: (i, 0, 0), memory_space=pltpu.SMEM)
    grid_spec = pltpu.PrefetchScalarGridSpec(
        num_scalar_prefetch=1, grid=(nb,),
        in_specs=[meta_spec, block_spec, any_spec] if dispatch else [meta_spec, any_spec],
        out_specs=any_spec if dispatch else block_spec,
        scratch_shapes=[pltpu.SemaphoreType.DMA(())])
    return pl.pallas_call(
        _moe_dispatch_kernel if dispatch else _moe_combine_kernel, grid_spec=grid_spec,
        out_shape=jax.ShapeDtypeStruct((out_rows, width), src.dtype),
        input_output_aliases=({3: 0} if dispatch else {}),
        compiler_params=_params(1), name="moe_dispatch" if dispatch else "moe_combine",
    )(*([off, meta, src, init] if dispatch else [off, meta, src]))


def _moe_expert_kernel(e_lo_ref, e_hi_ref, nv_ref, s_ref, wg1_ref, wu1_ref, wd1_ref, wg2_ref, wu2_ref, wd2_ref,
                       g_ref, b_ref, o_ref, *, alpha):
    del e_lo_ref, e_hi_ref
    occupied = pl.program_id(0) < nv_ref[0]

    @pl.when(jnp.logical_not(occupied))
    def _():
        o_ref[...] = jnp.zeros(o_ref.shape, F32)

    @pl.when(occupied)
    def _():
        d = o_ref.shape[1]
        x = s_ref[:, 0:d]
        xb = x.astype(BF16)
        y = jnp.zeros(x.shape, F32)
        for lane0, wg_ref, wu_ref, wd_ref in ((W_LO_LANE, wg1_ref, wu1_ref, wd1_ref),
                                              (W_HI_LANE, wg2_ref, wu2_ref, wd2_ref)):
            c = s_ref[:, d + lane0:d + lane0 + 1]
            a = _dot(xb, wg_ref[0])
            u = _dot(xb, wu_ref[0])
            hid = a * (1.0 / (1.0 + jnp.exp(-a))) * u
            y = y + _dot((c * hid).astype(BF16), wd_ref[0])
        o_ref[...] = _layer_norm(alpha * x + y, g_ref[...], b_ref[...])


def _moe_experts(sorted_rows, tile_e_lo, tile_e_hi, n_valid, w_gate, w_up, w_down, g, b, alpha):
    rows, width = sorted_rows.shape
    d = width - LANES
    _, _, ff = w_gate.shape
    n_tiles = rows // MOE_TILE
    tile = lambda i, lo, hi, nv: (jnp.minimum(i, nv[0] - 1), 0)
    out_tile = lambda i, lo, hi, nv: (i, 0)
    w_lo = lambda i, lo, hi, nv: (lo[jnp.minimum(i, nv[0] - 1)], 0, 0)
    w_hi = lambda i, lo, hi, nv: (hi[jnp.minimum(i, nv[0] - 1)], 0, 0)
    full = lambda i, lo, hi, nv: (0, 0)
    grid_spec = pltpu.PrefetchScalarGridSpec(
        num_scalar_prefetch=3, grid=(n_tiles,),
        in_specs=[pl.BlockSpec((MOE_TILE, width), tile),
                  pl.BlockSpec((1, d, ff), w_lo), pl.BlockSpec((1, d, ff), w_lo), pl.BlockSpec((1, ff, d), w_lo),
                  pl.BlockSpec((1, d, ff), w_hi), pl.BlockSpec((1, d, ff), w_hi), pl.BlockSpec((1, ff, d), w_hi),
                  pl.BlockSpec((1, d), full), pl.BlockSpec((1, d), full)],
        out_specs=pl.BlockSpec((MOE_TILE, d), out_tile))
    return pl.pallas_call(
        functools.partial(_moe_expert_kernel, alpha=alpha), grid_spec=grid_spec,
        out_shape=jax.ShapeDtypeStruct((rows, d), F32),
        compiler_params=_params(1), name="moe_experts",
    )(tile_e_lo, tile_e_hi, n_valid, sorted_rows, w_gate, w_up, w_down, w_gate, w_up, w_down,
      g.reshape(1, d), b.reshape(1, d))


def _moe(aug, meta, counts, w_gate, w_up, w_down, g, b, alpha):
    t, d = aug.shape[0], aug.shape[1] - LANES
    cnt = counts[0, 0:N_BINS].astype(I32)
    padded = (cnt + MOE_TILE - 1) // MOE_TILE * MOE_TILE
    ends = jnp.cumsum(padded)
    off = (ends - padded).astype(I32)
    n_tiles = t // MOE_TILE + N_BINS
    tile_start = jnp.arange(n_tiles, dtype=I32) * MOE_TILE
    tile_bin = jnp.minimum(jnp.sum((ends[None, :] <= tile_start[:, None]).astype(I32), axis=1), N_BINS - 1)
    pairs = jnp.asarray(_bin_pairs(), I32)
    group = tile_bin // PAIRS_PER_GROUP
    tile_e_lo = group * EXPERTS_PER_GROUP + pairs[tile_bin % PAIRS_PER_GROUP, 0]
    tile_e_hi = group * EXPERTS_PER_GROUP + pairs[tile_bin % PAIRS_PER_GROUP, 1]
    n_valid = (ends[-1:] // MOE_TILE).astype(I32)
    sorted_rows = _moe_permute(off, meta, aug, n_tiles * MOE_TILE,
                               init=jnp.zeros((n_tiles * MOE_TILE, d + LANES), F32))
    out_sorted = _moe_experts(sorted_rows, tile_e_lo, tile_e_hi, n_valid, w_gate, w_up, w_down, g, b, alpha)
    return _moe_permute(off, meta, out_sorted, t)


_COL_CQ, _COL_CKV, _COL_KROPE, _COL_KIDX, _COL_WIDX, _COL_END = 0, 256, 512, 640, 768, 896


def _dsa_proj_kernel(h_ref, win_ref, gq_ref, gkv_ref, wuq_ref, wqi_ref, wuk_ref,
                     cr_ref, sur_ref, sdr_ref, ci_ref, sui_ref, sdi_ref,
                     qft_ref, kv_ref, kvt_ref, qit_ref, ki_ref, wit_ref, *, q_scale, w_scale):
    y = _dot(h_ref[...].astype(BF16), win_ref[...])
    cr, sur, sdr = cr_ref[...], sur_ref[...], sdr_ref[...]
    ci, sui, sdi = ci_ref[...], sui_ref[...], sdi_ref[...]
    half_r, half_i = MLA_ROPE_DIM // 2, IDX_ROPE_DIM // 2
    qb = QUERY_BLOCK
    n_qb = y.shape[0] // qb
    c_q = _rms_norm(y[:, _COL_CQ:_COL_CKV], gq_ref[...]).astype(BF16)
    c_kv = _rms_norm(y[:, _COL_CKV:_COL_KROPE], gkv_ref[...])
    kv_ref[:, 0:MLA_KV_RANK] = c_kv.astype(BF16)
    kvt_ref[0] = c_kv.T.astype(BF16)
    kv_ref[:, MLA_KV_RANK:MLA_FEAT] = _rope(y[:, _COL_KROPE:_COL_KIDX], cr, sur, sdr, half_r).astype(BF16)
    ki_ref[...] = _rope(y[:, _COL_KIDX:_COL_WIDX], ci, sui, sdi, half_i).astype(BF16)
    wit_ref[...] = (y[:, _COL_WIDX:_COL_END] * w_scale).T[0:IDX_HEADS, :]
    q = _dot(c_q, wuq_ref[...])
    q_bf = q.astype(BF16)
    hd = MLA_ROPE_DIM + MLA_NOPE_DIM
    for hh in range(MLA_HEADS):
        lat_t = (_dot(q_bf[:, hh * hd:(hh + 1) * hd], wuk_ref[hh]) * q_scale).T.astype(BF16)
        rope_t = (_rope(q[:, hh * hd:(hh + 1) * hd], cr, sur, sdr, half_r) * q_scale).T.astype(BF16)
        for bl in range(n_qb):
            qft_ref[bl, 0:MLA_KV_RANK, hh * qb:(hh + 1) * qb] = lat_t[:, bl * qb:(bl + 1) * qb]
            qft_ref[bl, MLA_KV_RANK:MLA_FEAT, hh * qb:(hh + 1) * qb] = rope_t[:, bl * qb:(bl + 1) * qb]
    qi = _dot(c_q, wqi_ref[...])
    for p in range(IDX_HEADS // 2):
        pair_t = _rope(qi[:, p * LANES:(p + 1) * LANES], ci, sui, sdi, half_i).T.astype(BF16)
        for bl in range(n_qb):
            qit_ref[bl, p] = pair_t[:, bl * qb:(bl + 1) * qb]


def _dsa_proj(h2, w_in_p, gq, gkv, w_uq, w_qidx, w_ukt, tabs_r, tabs_i):
    t, d = h2.shape
    row = lambda i: (i, 0)
    full = lambda i: (0, 0)
    full3 = lambda i: (0, 0, 0)
    n_pairs = IDX_HEADS // 2
    qb = QUERY_BLOCK
    n_qb = ROW_BLOCK // qb
    kern = functools.partial(_dsa_proj_kernel, q_scale=(MLA_ROPE_DIM + MLA_NOPE_DIM) ** -0.5 * LOG2E,
                             w_scale=(IDX_HEADS * IDX_DIM) ** -0.5)
    tab = pl.BlockSpec((ROW_BLOCK, LANES), row)
    return pl.pallas_call(
        kern, grid=(t // ROW_BLOCK,),
        in_specs=[pl.BlockSpec((ROW_BLOCK, d), row), pl.BlockSpec(w_in_p.shape, full),
                  pl.BlockSpec((1, MLA_Q_RANK), full), pl.BlockSpec((1, MLA_KV_RANK), full),
                  pl.BlockSpec(w_uq.shape, full), pl.BlockSpec(w_qidx.shape, full),
                  pl.BlockSpec(w_ukt.shape, full3)] + [tab] * 6,
        out_specs=[pl.BlockSpec((n_qb, MLA_FEAT, MLA_HEADS * qb), lambda i: (i, 0, 0)),
                   pl.BlockSpec((ROW_BLOCK, MLA_FEAT), row),
                   pl.BlockSpec((1, MLA_KV_RANK, ROW_BLOCK), lambda i: (i, 0, 0)),
                   pl.BlockSpec((n_qb, n_pairs, LANES, qb), lambda i: (i, 0, 0, 0)),
                   pl.BlockSpec((ROW_BLOCK, LANES), row),
                   pl.BlockSpec((IDX_HEADS, ROW_BLOCK), lambda i: (0, i))],
        out_shape=[jax.ShapeDtypeStruct((t // qb, MLA_FEAT, MLA_HEADS * qb), BF16),
                   jax.ShapeDtypeStruct((t, MLA_FEAT), BF16),
                   jax.ShapeDtypeStruct((t // ROW_BLOCK, MLA_KV_RANK, ROW_BLOCK), BF16),
                   jax.ShapeDtypeStruct((t // qb, n_pairs, LANES, qb), BF16),
                   jax.ShapeDtypeStruct((t, LANES), BF16),
                   jax.ShapeDtypeStruct((IDX_HEADS, t), F32)],
        compiler_params=_params(1), name="dsa_proj",
    )(h2, w_in_p, gq.reshape(1, -1), gkv.reshape(1, -1), w_uq, w_qidx, w_ukt, *tabs_r, *tabs_i)


def _sortable_key(x):
    bits = lax.bitcast_convert_type(x, I32)
    return bits ^ ((bits >> 31) & 0x7FFFFFFF)


def _dsa_attn_kernel(qit_ref, wit_ref, qft_ref, ki_ref, kv_ref, kvt_ref, wuvt_ref, o_ref,
                     key_ref, thr_ref, m_ref, l_ref, acc_ref, *, kt, top_k):
    g = pl.program_id(1)
    qb = QUERY_BLOCK
    n_tiles = ((g + 1) * qb + kt - 1) // kt
    n_pairs = IDX_HEADS // 2
    lane_q = lax.broadcasted_iota(I32, (1, qb), 1)
    n_allowed = (g * (qb // CHUNK) + 1 + lane_q // CHUNK) * CHUNK

    feat = lax.broadcasted_iota(I32, (LANES, qb), 0)
    pair_w = []
    for p in range(n_pairs):
        slab = qit_ref[0, p]
        zero = jnp.zeros_like(slab)
        pair_w.append(jnp.concatenate([jnp.where(feat < IDX_DIM, slab, zero),
                                       jnp.where(feat >= IDX_DIM, slab, zero)], axis=1))
    wt = wit_ref[...]

    def score_tile(t, carry):
        lo, hi = carry
        k = ki_ref[pl.ds(pl.multiple_of(t * kt, kt), kt), :]
        sc = jnp.zeros((kt, qb), F32)
        for p in range(n_pairs):
            lg = _dot(k, pair_w[p])
            sc = (sc + jnp.maximum(lg[:, 0:qb], 0.0) * wt[2 * p:2 * p + 1, :]
                  + jnp.maximum(lg[:, qb:2 * qb], 0.0) * wt[2 * p + 1:2 * p + 2, :])
        kk = t * kt + lax.broadcasted_iota(I32, sc.shape, 0)
        valid = kk < n_allowed
        key_ref[t] = _sortable_key(jnp.where(valid, sc, NEG_INF))
        lo = jnp.minimum(lo, jnp.min(jnp.where(valid, sc, float("inf")), axis=0, keepdims=True))
        hi = jnp.maximum(hi, jnp.max(jnp.where(valid, sc, NEG_INF), axis=0, keepdims=True))
        return lo, hi

    lo_f, hi_f = lax.fori_loop(0, n_tiles, score_tile,
                               (jnp.full((1, qb), float("inf"), F32), jnp.full((1, qb), NEG_INF, F32)))

    keep_all = jnp.full((1, qb), NEG_INF_KEY + 1, I32)
    thr_ref[...] = keep_all

    @pl.when((g + 1) * qb > top_k)
    def _():
        def count_ge(mid):
            def body(t, cnt):
                hit = jnp.where(key_ref[t] >= mid, 1, 0)
                return cnt + jnp.sum(hit.reshape(kt // 8, 8, qb), axis=0)
            cnt = lax.fori_loop(0, n_tiles, body, jnp.zeros((8, qb), I32))
            return jnp.sum(cnt.astype(F32), axis=0, keepdims=True)

        def probe(lo, hi, mid):
            cnt = count_ge(mid)
            ge = cnt >= float(top_k)
            lo_next = jnp.where(ge, mid, lo)
            hi_next = jnp.where(cnt == float(top_k), mid, jnp.where(ge, hi, mid - 1))
            return lo_next, hi_next

        def bisect(st):
            lo, hi = st
            for _ in range(BISECT_STEPS_PER_CHECK):
                mid = (lo >> 1) + (hi >> 1) + ((lo | hi) & 1)
                lo, hi = probe(lo, hi, mid)
            return lo, hi

        def unresolved(st):
            lo, hi = st
            return jnp.max(jnp.where(hi > lo, 1.0, 0.0)) > 0.0

        lo, hi = _sortable_key(lo_f), _sortable_key(hi_f)
        for frac in BISECT_OPENING_FRACTIONS:
            split = jnp.minimum(jnp.maximum(_sortable_key(hi_f * frac), lo + 1), hi)
            lo, hi = probe(lo, hi, jnp.where(hi > lo, split, lo))
        lo, _ = lax.while_loop(unresolved, bisect, (lo, hi))
        thr_ref[...] = jnp.where(n_allowed > top_k, lo, keep_all)

    thr = thr_ref[...]
    m_ref[...] = jnp.full(m_ref.shape, NEG_INF, F32)
    l_ref[...] = jnp.zeros(l_ref.shape, F32)
    acc_ref[...] = jnp.zeros(acc_ref.shape, F32)
    hpg = HEADS_PER_GROUP
    group = hpg * qb

    def scores(t, gi):
        kv_rows = kv_ref[pl.ds(pl.multiple_of(t * kt, kt), kt), :]
        bias = jnp.where(key_ref[t] >= thr, 0.0, NEG_INF)
        bias_g = jnp.concatenate([bias] * hpg, axis=1)
        return _dot(kv_rows, qft_ref[0, :, gi * group:(gi + 1) * group]) + bias_g

    def softmax_pv(t, gi, s):
        m_old = m_ref[gi]
        m_new = jnp.maximum(m_old, jnp.max(s, axis=0, keepdims=True))
        m_safe = jnp.where(m_new == NEG_INF, 0.0, m_new)
        p = jnp.exp2(s - m_safe)
        alpha = jnp.exp2(m_old - m_safe)
        l_ref[gi] = alpha * l_ref[gi] + jnp.sum(p, axis=0, keepdims=True)
        acc_ref[gi] = alpha * acc_ref[gi] + _dot(kvt_ref[t], p.astype(BF16))
        m_ref[gi] = m_new

    def run(tiles):
        items = [(t, gi) for t in tiles for gi in range(MLA_HEADS // hpg)]
        ahead = 4
        pending = {n: scores(*items[n]) for n in range(min(ahead, len(items)))}
        for n, item in enumerate(items):
            if n + ahead < len(items):
                pending[n + ahead] = scores(*items[n + ahead])
            softmax_pv(*item, pending.pop(n))

    def tile_pair(tt, carry):
        run([2 * tt, 2 * tt + 1])
        return carry

    lax.fori_loop(0, n_tiles // 2, tile_pair, 0)

    @pl.when(n_tiles % 2 == 1)
    def _():
        run([n_tiles - 1])

    for gi in range(MLA_HEADS // hpg):
        o_lat_t = (acc_ref[gi] * (1.0 / l_ref[gi])).astype(BF16)
        for hl in range(hpg):
            hh = gi * hpg + hl
            o_t = _dot(wuvt_ref[hh], o_lat_t[:, hl * qb:(hl + 1) * qb])
            o_ref[:, hh * MLA_V_DIM:(hh + 1) * MLA_V_DIM] = o_t.T.astype(BF16)


def _dsa_attn(qft, kv, kvt, qit, ki, wit, w_uvt, bsz, seq):
    kt = KEY_TILE
    qb = QUERY_BLOCK
    ng = seq // qb
    n_pairs = IDX_HEADS // 2
    top_k = min(IDX_TOPK_MAX, seq // 4)
    kern = functools.partial(_dsa_attn_kernel, kt=kt, top_k=top_k)
    batch2 = lambda b, g: (b, 0)
    n_groups, group = MLA_HEADS // HEADS_PER_GROUP, HEADS_PER_GROUP * qb
    return pl.pallas_call(
        kern, grid=(bsz, ng),
        in_specs=[pl.BlockSpec((1, n_pairs, LANES, qb), lambda b, g: (b * ng + g, 0, 0, 0)),
                  pl.BlockSpec((IDX_HEADS, qb), lambda b, g: (0, b * ng + g)),
                  pl.BlockSpec((1, MLA_FEAT, MLA_HEADS * qb), lambda b, g: (b * ng + g, 0, 0)),
                  pl.BlockSpec((seq, LANES), batch2),
                  pl.BlockSpec((seq, MLA_FEAT), batch2),
                  pl.BlockSpec((seq // kt, MLA_KV_RANK, kt), lambda b, g: (b, 0, 0)),
                  pl.BlockSpec(w_uvt.shape, lambda b, g: (0, 0, 0))],
        out_specs=pl.BlockSpec((qb, MLA_HEADS * MLA_V_DIM), lambda b, g: (b * ng + g, 0)),
        out_shape=jax.ShapeDtypeStruct((bsz * seq, MLA_HEADS * MLA_V_DIM), BF16),
        scratch_shapes=[pltpu.VMEM((seq // kt, kt, qb), I32), pltpu.VMEM((1, qb), I32),
                        pltpu.VMEM((n_groups, 1, group), F32), pltpu.VMEM((n_groups, 1, group), F32),
                        pltpu.VMEM((n_groups, MLA_KV_RANK, group), F32)],
        compiler_params=_params(2), name="dsa_attn",
    )(qit, wit, qft, ki, kv, kvt, w_uvt)


def _diff_lambda_init(layer_idx):
    import math
    return 0.8 - 0.6 * math.exp(-0.3 * layer_idx)


def _pad_dsa_w_in(w_in):
    d = w_in.shape[0]
    o_cq, o_ckv = 0, MLA_Q_RANK
    o_kr = o_ckv + MLA_KV_RANK
    o_ki = o_kr + MLA_ROPE_DIM
    o_wi = o_ki + IDX_DIM
    out = jnp.zeros((d, _COL_END), w_in.dtype)
    out = out.at[:, _COL_CQ:_COL_CQ + MLA_Q_RANK].set(w_in[:, o_cq:o_ckv])
    out = out.at[:, _COL_CKV:_COL_CKV + MLA_KV_RANK].set(w_in[:, o_ckv:o_kr])
    out = out.at[:, _COL_KROPE:_COL_KROPE + MLA_ROPE_DIM].set(w_in[:, o_kr:o_ki])
    out = out.at[:, _COL_KIDX:_COL_KIDX + IDX_DIM].set(w_in[:, o_ki:o_wi])
    out = out.at[:, _COL_KIDX + IDX_DIM:_COL_KIDX + 2 * IDX_DIM].set(w_in[:, o_ki:o_wi])
    out = out.at[:, _COL_WIDX:_COL_WIDX + IDX_HEADS].set(w_in[:, o_wi:o_wi + IDX_HEADS])
    return out.astype(BF16)


def _router_weights(w_group, b_group, w_expert, b_expert):
    d = w_group.shape[0]
    w = jnp.zeros((d, LANES), F32)
    w = w.at[:, 0:N_GROUPS].set(w_group).at[:, N_GROUPS:N_GROUPS + N_EXPERTS].set(w_expert)
    br = jnp.zeros((1, LANES), F32)
    br = br.at[0, 0:N_GROUPS].set(b_group).at[0, N_GROUPS:N_GROUPS + N_EXPERTS].set(b_expert)
    w_hi = w.astype(BF16)
    w_lo = (w - w_hi.astype(F32)).astype(BF16)
    return w_hi, w_lo, br


def kernel(x, mem, positions, a_w_in, a_lambda, a_subln_g, a_w_out, b_w_in, b_q_norm_g, b_kv_norm_g,
           b_w_uq, b_w_qidx, b_w_uk, b_w_uv, b_w_out, mem_w_kv, xa_w_q, xa_w_out,
           moe_w_group, moe_b_group, moe_w_expert, moe_b_expert, moe_w_gate, moe_w_up, moe_w_down,
           ln_g, ln_b):
    bsz, seq, d = x.shape
    n_mem = mem.shape[1]
    depth = ln_g.shape[0]
    alpha = (2.0 * depth) ** 0.25
    assert seq % ROW_BLOCK == 0 and seq % KEY_TILE == 0
    assert ROW_BLOCK == ATTN_BLOCK and ROW_BLOCK == KEY_TILE and QUERY_BLOCK == 2 * CHUNK

    tabs_d = _rope_lane_tables(positions, DIFF_ROPE_DIM, DIFF_HEAD_DIM, True)
    tabs_r = _rope_lane_tables(positions, MLA_ROPE_DIM, LANES, False)
    tabs_i = _rope_lane_tables(positions, IDX_ROPE_DIM, IDX_DIM, True)

    memkv = _mem_kv(mem.reshape(bsz * n_mem, d), mem_w_kv.astype(BF16))

    h = x.reshape(bsz * seq, d)
    for i in range(depth):
        j = i // N_MIXERS
        if i % N_MIXERS == 0:
            qt, k, vt = _diff_qkv(h, a_w_in[j].astype(BF16), tabs_d)
            mix = _diff_attn(qt, k, vt, a_lambda[j], a_subln_g[j], bsz, seq, _diff_lambda_init(i))
            w_mix_out = a_w_out[j]
        else:
            w_ukt = jnp.pad(jnp.swapaxes(b_w_uk[j], 1, 2), ((0, 0), (MLA_ROPE_DIM, 0), (0, 0))).astype(BF16)
            qft, kv, kvt, qit, ki, wit = _dsa_proj(h, _pad_dsa_w_in(b_w_in[j]), b_q_norm_g[j], b_kv_norm_g[j],
                                                   b_w_uq[j].astype(BF16), b_w_qidx[j].astype(BF16), w_ukt,
                                                   tabs_r, tabs_i)
            w_uvt = jnp.swapaxes(b_w_uv[j], 1, 2).astype(BF16)
            mix = _dsa_attn(qft, kv, kvt, qit, ki, wit, w_uvt, bsz, seq)
            w_mix_out = b_w_out[j]
        wr_hi, wr_lo, br = _router_weights(moe_w_group[i], moe_b_group[i], moe_w_expert[i], moe_b_expert[i])
        ln_rows = jnp.stack([ln_g[i, 0], ln_b[i, 0], ln_g[i, 1], ln_b[i, 1]])
        aug, meta, counts = _layer_mid(mix, w_mix_out.astype(BF16), h, ln_rows, xa_w_q[i].astype(BF16), memkv,
                                       xa_w_out[i].astype(BF16), wr_hi, wr_lo, br, alpha, seq, n_mem)
        h = _moe(aug, meta, counts, moe_w_gate[i].astype(BF16), moe_w_up[i].astype(BF16),
                 moe_w_down[i].astype(BF16), ln_g[i, 2], ln_b[i, 2], alpha)
    return h.reshape(bsz, seq, d)
```

```python
import functools

import jax
import jax.numpy as jnp
from jax import lax
from jax.experimental import pallas as pl
from jax.experimental.pallas import tpu as pltpu

F32 = jnp.float32
BF16 = jnp.bfloat16
I32 = jnp.int32

CHUNK = 64
ROPE_THETA = 500000.0
LN_EPS = 1e-5
N_MIXERS = 2
DIFF_HEADS = 8
DIFF_HEAD_DIM = 64
DIFF_ROPE_DIM = DIFF_HEAD_DIM // 4
MLA_HEADS = 8
MLA_Q_RANK = 256
MLA_KV_RANK = 256
MLA_ROPE_DIM = 32
MLA_NOPE_DIM = 96
MLA_V_DIM = 128
IDX_HEADS = 16
IDX_DIM = 64
IDX_ROPE_DIM = IDX_DIM // 4
IDX_TOPK_MAX = 256
MEM_HEADS = 4
N_GROUPS = 4
EXPERTS_PER_GROUP = 4
N_EXPERTS = N_GROUPS * EXPERTS_PER_GROUP

LANES = 128
VMEM_LIMIT_BYTES = 56 * 1024 * 1024

ROW_BLOCK = 512
ATTN_BLOCK = 512
KEY_TILE = 512
QUERY_BLOCK = 128
HEADS_PER_GROUP = 2
BISECT_STEPS_PER_CHECK = 4
BISECT_OPENING_FRACTIONS = (0.125, 0.5)
MLA_FEAT = MLA_KV_RANK + LANES

NEG_INF = float("-inf")
NEG_INF_KEY = -2139095041

_NT = (((1,), (1,)), ((), ()))


def _params(n_axes):
    return pltpu.CompilerParams(dimension_semantics=("arbitrary",) * n_axes,
                                vmem_limit_bytes=VMEM_LIMIT_BYTES)


def _dot(a, b):
    return jnp.dot(a, b, preferred_element_type=F32)


def _dot_nt(a, b):
    return lax.dot_general(a, b, _NT, preferred_element_type=F32)


def _layer_norm(z, g, b):
    mu = jnp.mean(z, axis=-1, keepdims=True)
    zc = z - mu
    var = jnp.mean(zc * zc, axis=-1, keepdims=True)
    return zc * lax.rsqrt(var + LN_EPS) * g + b


def _rms_norm(x, g):
    return x * lax.rsqrt(jnp.mean(x * x, axis=-1, keepdims=True) + LN_EPS) * g


def _rope(y, c, s_up, s_dn, half):
    return (y * c + pltpu.roll(y, LANES - half, 1) * s_up + pltpu.roll(y, half, 1) * s_dn)


def _rope_lane_tables(positions, rot_dim, period, keep_rest):
    half = rot_dim // 2
    inv_freq = ROPE_THETA ** (-jnp.arange(0, rot_dim, 2, dtype=F32) / rot_dim)
    ang = positions.astype(F32)[..., None] * inv_freq
    cos, sin = jnp.cos(ang), jnp.sin(ang)
    lead = positions.shape
    rest = jnp.full(lead + (period - rot_dim,), 1.0 if keep_rest else 0.0, F32)
    zrest = jnp.zeros(lead + (period - rot_dim,), F32)
    zhalf = jnp.zeros(lead + (half,), F32)
    reps = LANES // period
    out = []
    for parts in ((cos, cos, rest), (-sin, zhalf, zrest), (zhalf, sin, zrest)):
        t = jnp.concatenate(parts, axis=-1)
        out.append(jnp.tile(t, (1,) * len(lead) + (reps,)).reshape(-1, LANES))
    return out


LOG2E = 1.4426950408889634


def _diff_qkv_kernel(h_ref, w_ref, c_ref, su_ref, sd_ref, qt_ref, k_ref, vt_ref, *, q_scale, half):
    x = h_ref[...].astype(BF16)
    c, su, sd = c_ref[...], su_ref[...], sd_ref[...]
    n_heads = qt_ref.shape[0]
    for j2 in range(3 * n_heads // 2):
        y2 = _dot(x, w_ref[:, j2 * 2 * LANES:(j2 + 1) * 2 * LANES])
        for s in range(2):
            j = 2 * j2 + s
            y = y2[:, s * LANES:(s + 1) * LANES]
            if j < n_heads:
                qt_ref[j, 0] = (_rope(y, c, su, sd, half) * q_scale).T.astype(BF16)
            elif j < 2 * n_heads:
                k_ref[:, (j - n_heads) * LANES:(j - n_heads + 1) * LANES] = _rope(y, c, su, sd, half).astype(BF16)
            else:
                vt_ref[j - 2 * n_heads, 0] = y.T.astype(BF16)


def _diff_qkv(h2, w_in, tabs):
    t, d = h2.shape
    hd = 2 * DIFF_HEAD_DIM
    nb = t // ROW_BLOCK
    kern = functools.partial(_diff_qkv_kernel, q_scale=DIFF_HEAD_DIM ** -0.5 * LOG2E, half=DIFF_ROPE_DIM // 2)
    row = lambda i: (i, 0)
    full = lambda i: (0, 0)
    t_shape = jax.ShapeDtypeStruct((DIFF_HEADS, nb, hd, ROW_BLOCK), BF16)
    t_spec = pl.BlockSpec((DIFF_HEADS, 1, hd, ROW_BLOCK), lambda i: (0, i, 0, 0))
    return pl.pallas_call(
        kern, grid=(nb,),
        in_specs=[pl.BlockSpec((ROW_BLOCK, d), row), pl.BlockSpec(w_in.shape, full)]
        + [pl.BlockSpec((ROW_BLOCK, LANES), row)] * 3,
        out_specs=[t_spec, pl.BlockSpec((ROW_BLOCK, DIFF_HEADS * hd), row), t_spec],
        out_shape=[t_shape, jax.ShapeDtypeStruct((t, DIFF_HEADS * hd), BF16), t_shape],
        compiler_params=_params(1), name="diff_qkv",
    )(h2, w_in, *tabs)


def _diff_attn_kernel(lam_ref, g_ref, qt_ref, k_ref, vt_ref, o_ref, m_ref, l_ref, acc_ref, *,
                      blk, lambda_init):
    i = pl.program_id(2)
    lam = lam_ref[...]
    lam_full = (jnp.exp(jnp.sum(lam[0:1] * lam[1:2], axis=1, keepdims=True))
                - jnp.exp(jnp.sum(lam[2:3] * lam[3:4], axis=1, keepdims=True)) + lambda_init)
    qt = qt_ref[0, 0]
    feat = lax.broadcasted_iota(I32, qt.shape, 0)
    zero = jnp.zeros_like(qt)
    q_maps = (jnp.where(feat < DIFF_HEAD_DIM, qt, zero), jnp.where(feat >= DIFF_HEAD_DIM, qt, zero))
    m_ref[...] = jnp.full(m_ref.shape, NEG_INF, F32)
    l_ref[...] = jnp.zeros(l_ref.shape, F32)
    acc_ref[...] = jnp.zeros(acc_ref.shape, F32)

    half_w = blk // 2

    def scores(j, c, hf, masked):
        k = k_ref[pl.ds(pl.multiple_of(j * blk, blk), blk), :]
        s = _dot(k, q_maps[c][:, hf * half_w:(hf + 1) * half_w])
        if masked:
            kc = lax.broadcasted_iota(I32, s.shape, 0) // CHUNK
            qc = (hf * half_w + lax.broadcasted_iota(I32, s.shape, 1)) // CHUNK
            s = jnp.where(kc <= qc, s, NEG_INF)
        return s

    def softmax_pv(j, c, hf, s):
        idx = 2 * c + hf
        m_old = m_ref[idx]
        m_new = jnp.maximum(m_old, jnp.max(s, axis=0, keepdims=True))
        p = jnp.exp2(s - m_new)
        alpha = jnp.exp2(m_old - m_new)
        l_ref[idx] = alpha * l_ref[idx] + jnp.sum(p, axis=0, keepdims=True)
        acc_ref[idx] = alpha * acc_ref[idx] + _dot(vt_ref[0, j], p.astype(BF16))
        m_ref[idx] = m_new

    def run(blocks):
        items = [(j, c, hf, masked) for (j, masked) in blocks for c in range(2) for hf in range(2)]
        ahead = 4
        pending = {n: scores(*items[n]) for n in range(min(ahead, len(items)))}
        for n, (j, c, hf, _) in enumerate(items):
            if n + ahead < len(items):
                pending[n + ahead] = scores(*items[n + ahead])
            softmax_pv(j, c, hf, pending.pop(n))

    def full_pair(jj, carry):
        run([(2 * jj, False), (2 * jj + 1, False)])
        return carry

    lax.fori_loop(0, i // 2, full_pair, 0)

    @pl.when(i % 2 == 1)
    def _():
        run([(i - 1, False), (i, True)])

    @pl.when(i % 2 == 0)
    def _():
        run([(i, True)])

    norm = [acc_ref[idx] * (1.0 / l_ref[idx]) for idx in range(4)]
    ot = (jnp.concatenate(norm[0:2], axis=1) - lam_full * jnp.concatenate(norm[2:4], axis=1))
    ot = ot * lax.rsqrt(jnp.mean(ot * ot, axis=0, keepdims=True) + LN_EPS) * (1.0 - lambda_init)
    g = g_ref[...]
    for s in range(blk // LANES):
        o_ref[s * LANES:(s + 1) * LANES, :] = (ot[:, s * LANES:(s + 1) * LANES] * g).T.astype(BF16)


def _diff_attn(qt, k, vt, lam, subln_g, bsz, seq, lambda_init):
    blk = ATTN_BLOCK
    nq = seq // blk
    hd = 2 * DIFF_HEAD_DIM
    kern = functools.partial(_diff_attn_kernel, blk=blk, lambda_init=lambda_init)
    g = jnp.broadcast_to(subln_g.astype(F32)[:, None], (hd, LANES))
    return pl.pallas_call(
        kern, grid=(bsz, DIFF_HEADS, nq),
        in_specs=[
            pl.BlockSpec(lam.shape, lambda b, h, i: (0, 0)),
            pl.BlockSpec((hd, LANES), lambda b, h, i: (0, 0)),
            pl.BlockSpec((1, 1, hd, blk), lambda b, h, i: (h, b * nq + i, 0, 0)),
            pl.BlockSpec((seq, hd), lambda b, h, i: (b, h)),
            pl.BlockSpec((1, nq, hd, blk), lambda b, h, i: (h, b, 0, 0)),
        ],
        out_specs=pl.BlockSpec((blk, hd), lambda b, h, i: (b * nq + i, h)),
        out_shape=jax.ShapeDtypeStruct((bsz * seq, DIFF_HEADS * hd), BF16),
        scratch_shapes=[pltpu.VMEM((4, 1, blk // 2), F32), pltpu.VMEM((4, 1, blk // 2), F32),
                        pltpu.VMEM((4, hd, blk // 2), F32)],
        compiler_params=_params(3), name="diff_attn",
    )(lam, g, qt, k, vt)


def _mem_kv_kernel(m_ref, w_ref, o_ref):
    o_ref[...] = _dot(m_ref[...].astype(BF16), w_ref[...]).astype(BF16)


def _mem_kv(mem2, w_kv):
    rows, d = mem2.shape
    n = w_kv.shape[1]
    blk = min(ROW_BLOCK, rows)
    return pl.pallas_call(
        _mem_kv_kernel, grid=(rows // blk,),
        in_specs=[pl.BlockSpec((blk, d), lambda i: (i, 0)), pl.BlockSpec((d, n), lambda i: (0, 0))],
        out_specs=pl.BlockSpec((blk, n), lambda i: (i, 0)),
        out_shape=jax.ShapeDtypeStruct((rows, n), BF16),
        compiler_params=_params(1), name="mem_kv",
    )(mem2, w_kv)


def _cross_attn_block(h, wq_ref, kv_ref, wo_ref, g, b, alpha):
    d = h.shape[1]
    hd = d // MEM_HEADS
    q = (_dot(h.astype(BF16), wq_ref[...]) * hd ** -0.5).astype(BF16)
    outs = []
    for hh in range(MEM_HEADS):
        s = _dot_nt(q[:, hh * hd:(hh + 1) * hd], kv_ref[:, hh * hd:(hh + 1) * hd])
        p = jnp.exp(s - jnp.max(s, axis=1, keepdims=True))
        p = p * (1.0 / jnp.sum(p, axis=1, keepdims=True))
        outs.append(_dot(p.astype(BF16), kv_ref[:, d + hh * hd:d + (hh + 1) * hd]).astype(BF16))
    o = jnp.concatenate(outs, axis=1)
    return _layer_norm(alpha * h + _dot(o, wo_ref[...]), g, b)


def _route(h, wr_hi, wr_lo, br):
    h_hi = h.astype(BF16)
    h_lo = (h - h_hi.astype(F32)).astype(BF16)
    logits = _dot(h_hi, wr_hi) + _dot(h_hi, wr_lo) + _dot(h_lo, wr_hi) + br
    lane = lax.broadcasted_iota(I32, logits.shape, 1).astype(F32)
    gl = jnp.where(lane < N_GROUPS, logits, NEG_INF)
    gmax = jnp.max(gl, axis=1, keepdims=True)
    g_sel = jnp.min(jnp.where(gl == gmax, lane, float(LANES)), axis=1, keepdims=True)
    g_gate = 1.0 / jnp.sum(jnp.exp(gl - gmax), axis=1, keepdims=True)
    first = N_GROUPS + g_sel * EXPERTS_PER_GROUP
    el = jnp.where((lane >= first) & (lane < first + EXPERTS_PER_GROUP), logits, NEG_INF)
    v1 = jnp.max(el, axis=1, keepdims=True)
    i1 = jnp.min(jnp.where(el == v1, lane, float(LANES)), axis=1, keepdims=True)
    el2 = jnp.where(lane == i1, NEG_INF, el)
    v2 = jnp.max(el2, axis=1, keepdims=True)
    i2 = jnp.min(jnp.where(el2 == v2, lane, float(LANES)), axis=1, keepdims=True)
    r = jnp.exp(v2 - v1)
    w1 = g_gate / (1.0 + r)
    w2 = w1 * r
    first_is_low = i1 < i2
    e_lo = jnp.minimum(i1, i2) - first
    e_hi = jnp.maximum(i1, i2) - first
    return g_sel, e_lo, e_hi, jnp.where(first_is_low, w1, w2), jnp.where(first_is_low, w2, w1)


PAIRS_PER_GROUP = EXPERTS_PER_GROUP * (EXPERTS_PER_GROUP - 1) // 2
N_BINS = N_GROUPS * PAIRS_PER_GROUP
MOE_TILE = 256
W_LO_LANE, W_HI_LANE = 0, 64
META_ROWS = 8
DMA_LOOP_UNROLL = 8


def _bin_pairs():
    return [(lo, hi) for lo in range(EXPERTS_PER_GROUP) for hi in range(lo + 1, EXPERTS_PER_GROUP)]


def _route_block(h, wrh_ref, wrl_ref, br_ref, aug_ref, meta_ref, cnt_ref, run_ref):
    rows, d = h.shape
    g_sel, e_lo, e_hi, w_lo, w_hi = _route(h, wrh_ref[...], wrl_ref[...], br_ref[...])
    pid = e_lo * (7.0 - e_lo) * 0.5 + e_hi - e_lo - 1.0
    bin_id = g_sel * PAIRS_PER_GROUP + pid
    lane = lax.broadcasted_iota(I32, (rows, LANES), 1).astype(F32)
    onehot = jnp.where(lane == bin_id, 1.0, 0.0)
    r_i = lax.broadcasted_iota(I32, (rows, rows), 0)
    c_i = lax.broadcasted_iota(I32, (rows, rows), 1)
    tri = jnp.where(c_i < r_i, 1.0, 0.0).astype(BF16)
    before = _dot(tri, onehot.astype(BF16)) + run_ref[...]
    rank = jnp.sum(before * onehot, axis=1, keepdims=True)
    run_ref[...] += jnp.sum(onehot, axis=0, keepdims=True)
    cnt_ref[...] = jnp.broadcast_to(run_ref[...], cnt_ref.shape)

    aug_ref[:, 0:d] = h
    aug_ref[:, d:d + LANES] = jnp.where(lane < W_HI_LANE, w_lo, w_hi)
    meta = jnp.where(lane == 0.0, bin_id, jnp.where(lane == 1.0, rank, 0.0))
    meta_ref[0] = meta.T[0:META_ROWS, :].astype(I32)


def _layer_mid_kernel(mix_ref, wmix_ref, h_ref, ln_ref, wq_ref, kv_ref, wo_ref, wrh_ref, wrl_ref, br_ref,
                      aug_ref, meta_ref, cnt_ref, run_ref, *, alpha):
    @pl.when(pl.program_id(0) == 0)
    def _():
        run_ref[...] = jnp.zeros(run_ref.shape, F32)

    ln = ln_ref[...]
    h1 = _layer_norm(alpha * h_ref[...] + _dot(mix_ref[...], wmix_ref[...]), ln[0:1], ln[1:2])
    h2 = _cross_attn_block(h1, wq_ref, kv_ref, wo_ref, ln[2:3], ln[3:4], alpha)
    _route_block(h2, wrh_ref, wrl_ref, br_ref, aug_ref, meta_ref, cnt_ref, run_ref)


def _layer_mid(mix, w_mix, h2, ln_rows, w_q, memkv, w_out, wr_hi, wr_lo, br, alpha, seq, n_mem):
    t, d = h2.shape
    k = mix.shape[1]
    nb = t // ROW_BLOCK
    per_batch = seq // ROW_BLOCK
    row = lambda i: (i, 0)
    full = lambda i: (0, 0)
    return pl.pallas_call(
        functools.partial(_layer_mid_kernel, alpha=alpha), grid=(nb,),
        in_specs=[pl.BlockSpec((ROW_BLOCK, k), row), pl.BlockSpec((k, d), full),
                  pl.BlockSpec((ROW_BLOCK, d), row), pl.BlockSpec(ln_rows.shape, full),
                  pl.BlockSpec((d, d), full), pl.BlockSpec((n_mem, 2 * d), lambda i: (i // per_batch, 0)),
                  pl.BlockSpec((d, d), full), pl.BlockSpec((d, LANES), full),
                  pl.BlockSpec((d, LANES), full), pl.BlockSpec((1, LANES), full)],
        out_specs=[pl.BlockSpec((ROW_BLOCK, d + LANES), row),
                   pl.BlockSpec((1, META_ROWS, ROW_BLOCK), lambda i: (i, 0, 0)),
                   pl.BlockSpec((8, LANES), full)],
        out_shape=[jax.ShapeDtypeStruct((t, d + LANES), F32),
                   jax.ShapeDtypeStruct((nb, META_ROWS, ROW_BLOCK), I32),
                   jax.ShapeDtypeStruct((8, LANES), F32)],
        scratch_shapes=[pltpu.VMEM((1, LANES), F32)],
        compiler_params=_params(1), name="layer_mid",
    )(mix, w_mix, h2, ln_rows, w_q, memkv, w_out, wr_hi, wr_lo, br)


def _row_dma_loops(copy):
    def issue(r8, carry):
        for u in range(DMA_LOOP_UNROLL):
            copy(r8 * DMA_LOOP_UNROLL + u).start(priority=u % 2)
        return carry

    def drain(r8, carry):
        for u in range(DMA_LOOP_UNROLL):
            copy(r8 * DMA_LOOP_UNROLL + u).wait()
        return carry

    lax.fori_loop(0, ROW_BLOCK // DMA_LOOP_UNROLL, issue, 0)
    lax.fori_loop(0, ROW_BLOCK // DMA_LOOP_UNROLL, drain, 0)


def _moe_dispatch_kernel(off_ref, meta_ref, src_ref, init_ref, dst_ref, sem):
    del init_ref

    def copy(r):
        slot = off_ref[meta_ref[0, 0, r]] + meta_ref[0, 1, r]
        return pltpu.make_async_copy(src_ref.at[pl.ds(r, 1), :], dst_ref.at[pl.ds(slot, 1), :], sem)

    _row_dma_loops(copy)


def _moe_combine_kernel(off_ref, meta_ref, src_ref, dst_ref, sem):
    def copy(r):
        slot = off_ref[meta_ref[0, 0, r]] + meta_ref[0, 1, r]
        return pltpu.make_async_copy(src_ref.at[pl.ds(slot, 1), :], dst_ref.at[pl.ds(r, 1), :], sem)

    _row_dma_loops(copy)


def _moe_permute(off, meta, src, out_rows, init=None):
    nb = meta.shape[0]
    width = src.shape[1]
    dispatch = init is not None
    any_spec = pl.BlockSpec(memory_space=pl.ANY)
    block_spec = pl.BlockSpec((ROW_BLOCK, width), lambda i, off: (i, 0))
    meta_spec = pl.BlockSpec((1, META_ROWS, ROW_BLOCK), lambda i, off: (i, 0, 0), memory_space=pltpu.SMEM)
    grid_spec = pltpu.PrefetchScalarGridSpec(
        num_scalar_prefetch=1, grid=(nb,),
        in_specs=[meta_spec, block_spec, any_spec] if dispatch else [meta_spec, any_spec],
        out_specs=any_spec if dispatch else block_spec,
        scratch_shapes=[pltpu.SemaphoreType.DMA(())])
    return pl.pallas_call(
        _moe_dispatch_kernel if dispatch else _moe_combine_kernel, grid_spec=grid_spec,
        out_shape=jax.ShapeDtypeStruct((out_rows, width), src.dtype),
        input_output_aliases=({3: 0} if dispatch else {}),
        compiler_params=_params(1), name="moe_dispatch" if dispatch else "moe_combine",
    )(*([off, meta, src, init] if dispatch else [off, meta, src]))


def _moe_expert_kernel(e_lo_ref, e_hi_ref, nv_ref, s_ref, wg1_ref, wu1_ref, wd1_ref, wg2_ref, wu2_ref, wd2_ref,
                       g_ref, b_ref, o_ref, *, alpha):
    del e_lo_ref, e_hi_ref
    occupied = pl.program_id(0) < nv_ref[0]

    @pl.when(jnp.logical_not(occupied))
    def _():
        o_ref[...] = jnp.zeros(o_ref.shape, F32)

    @pl.when(occupied)
    def _():
        d = o_ref.shape[1]
        x = s_ref[:, 0:d]
        xb = x.astype(BF16)
        y = jnp.zeros(x.shape, F32)
        for lane0, wg_ref, wu_ref, wd_ref in ((W_LO_LANE, wg1_ref, wu1_ref, wd1_ref),
                                              (W_HI_LANE, wg2_ref, wu2_ref, wd2_ref)):
            c = s_ref[:, d + lane0:d + lane0 + 1]
            a = _dot(xb, wg_ref[0])
            u = _dot(xb, wu_ref[0])
            hid = a * (1.0 / (1.0 + jnp.exp(-a))) * u
            y = y + _dot((c * hid).astype(BF16), wd_ref[0])
        o_ref[...] = _layer_norm(alpha * x + y, g_ref[...], b_ref[...])


def _moe_experts(sorted_rows, tile_e_lo, tile_e_hi, n_valid, w_gate, w_up, w_down, g, b, alpha):
    rows, width = sorted_rows.shape
    d = width - LANES
    _, _, ff = w_gate.shape
    n_tiles = rows // MOE_TILE
    tile = lambda i, lo, hi, nv: (jnp.minimum(i, nv[0] - 1), 0)
    out_tile = lambda i, lo, hi, nv: (i, 0)
    w_lo = lambda i, lo, hi, nv: (lo[jnp.minimum(i, nv[0] - 1)], 0, 0)
    w_hi = lambda i, lo, hi, nv: (hi[jnp.minimum(i, nv[0] - 1)], 0, 0)
    full = lambda i, lo, hi, nv: (0, 0)
    grid_spec = pltpu.PrefetchScalarGridSpec(
        num_scalar_prefetch=3, grid=(n_tiles,),
        in_specs=[pl.BlockSpec((MOE_TILE, width), tile),
                  pl.BlockSpec((1, d, ff), w_lo), pl.BlockSpec((1, d, ff), w_lo), pl.BlockSpec((1, ff, d), w_lo),
                  pl.BlockSpec((1, d, ff), w_hi), pl.BlockSpec((1, d, ff), w_hi), pl.BlockSpec((1, ff, d), w_hi),
                  pl.BlockSpec((1, d), full), pl.BlockSpec((1, d), full)],
        out_specs=pl.BlockSpec((MOE_TILE, d), out_tile))
    return pl.pallas_call(
        functools.partial(_moe_expert_kernel, alpha=alpha), grid_spec=grid_spec,
        out_shape=jax.ShapeDtypeStruct((rows, d), F32),
        compiler_params=_params(1), name="moe_experts",
    )(tile_e_lo, tile_e_hi, n_valid, sorted_rows, w_gate, w_up, w_down, w_gate, w_up, w_down,
      g.reshape(1, d), b.reshape(1, d))


def _moe(aug, meta, counts, w_gate, w_up, w_down, g, b, alpha):
    t, d = aug.shape[0], aug.shape[1] - LANES
    cnt = counts[0, 0:N_BINS].astype(I32)
    padded = (cnt + MOE_TILE - 1) // MOE_TILE * MOE_TILE
    ends = jnp.cumsum(padded)
    off = (ends - padded).astype(I32)
    n_tiles = t // MOE_TILE + N_BINS
    tile_start = jnp.arange(n_tiles, dtype=I32) * MOE_TILE
    tile_bin = jnp.minimum(jnp.sum((ends[None, :] <= tile_start[:, None]).astype(I32), axis=1), N_BINS - 1)
    pairs = jnp.asarray(_bin_pairs(), I32)
    group = tile_bin // PAIRS_PER_GROUP
    tile_e_lo = group * EXPERTS_PER_GROUP + pairs[tile_bin % PAIRS_PER_GROUP, 0]
    tile_e_hi = group * EXPERTS_PER_GROUP + pairs[tile_bin % PAIRS_PER_GROUP, 1]
    n_valid = (ends[-1:] // MOE_TILE).astype(I32)
    sorted_rows = _moe_permute(off, meta, aug, n_tiles * MOE_TILE,
                               init=jnp.zeros((n_tiles * MOE_TILE, d + LANES), F32))
    out_sorted = _moe_experts(sorted_rows, tile_e_lo, tile_e_hi, n_valid, w_gate, w_up, w_down, g, b, alpha)
    return _moe_permute(off, meta, out_sorted, t)


_COL_CQ, _COL_CKV, _COL_KROPE, _COL_KIDX, _COL_WIDX, _COL_END = 0, 256, 512, 640, 768, 896


def _dsa_proj_kernel(h_ref, win_ref, gq_ref, gkv_ref, wuq_ref, wqi_ref, wuk_ref,
                     cr_ref, sur_ref, sdr_ref, ci_ref, sui_ref, sdi_ref,
                     qft_ref, kv_ref, kvt_ref, qit_ref, ki_ref, wit_ref, *, q_scale, w_scale):
    y = _dot(h_ref[...].astype(BF16), win_ref[...])
    cr, sur, sdr = cr_ref[...], sur_ref[...], sdr_ref[...]
    ci, sui, sdi = ci_ref[...], sui_ref[...], sdi_ref[...]
    half_r, half_i = MLA_ROPE_DIM // 2, IDX_ROPE_DIM // 2
    qb = QUERY_BLOCK
    n_qb = y.shape[0] // qb
    c_q = _rms_norm(y[:, _COL_CQ:_COL_CKV], gq_ref[...]).astype(BF16)
    c_kv = _rms_norm(y[:, _COL_CKV:_COL_KROPE], gkv_ref[...])
    kv_ref[:, 0:MLA_KV_RANK] = c_kv.astype(BF16)
    kvt_ref[0] = c_kv.T.astype(BF16)
    kv_ref[:, MLA_KV_RANK:MLA_FEAT] = _rope(y[:, _COL_KROPE:_COL_KIDX], cr, sur, sdr, half_r).astype(BF16)
    ki_ref[...] = _rope(y[:, _COL_KIDX:_COL_WIDX], ci, sui, sdi, half_i).astype(BF16)
    wit_ref[...] = (y[:, _COL_WIDX:_COL_END] * w_scale).T[0:IDX_HEADS, :]
    q = _dot(c_q, wuq_ref[...])
    q_bf = q.astype(BF16)
    hd = MLA_ROPE_DIM + MLA_NOPE_DIM
    for hh in range(MLA_HEADS):
        lat_t = (_dot(q_bf[:, hh * hd:(hh + 1) * hd], wuk_ref[hh]) * q_scale).T.astype(BF16)
        rope_t = (_rope(q[:, hh * hd:(hh + 1) * hd], cr, sur, sdr, half_r) * q_scale).T.astype(BF16)
        for bl in range(n_qb):
            qft_ref[bl, 0:MLA_KV_RANK, hh * qb:(hh + 1) * qb] = lat_t[:, bl * qb:(bl + 1) * qb]
            qft_ref[bl, MLA_KV_RANK:MLA_FEAT, hh * qb:(hh + 1) * qb] = rope_t[:, bl * qb:(bl + 1) * qb]
    qi = _dot(c_q, wqi_ref[...])
    for p in range(IDX_HEADS // 2):
        pair_t = _rope(qi[:, p * LANES:(p + 1) * LANES], ci, sui, sdi, half_i).T.astype(BF16)
        for bl in range(n_qb):
            qit_ref[bl, p] = pair_t[:, bl * qb:(bl + 1) * qb]


def _dsa_proj(h2, w_in_p, gq, gkv, w_uq, w_qidx, w_ukt, tabs_r, tabs_i):
    t, d = h2.shape
    row = lambda i: (i, 0)
    full = lambda i: (0, 0)
    full3 = lambda i: (0, 0, 0)
    n_pairs = IDX_HEADS // 2
    qb = QUERY_BLOCK
    n_qb = ROW_BLOCK // qb
    kern = functools.partial(_dsa_proj_kernel, q_scale=(MLA_ROPE_DIM + MLA_NOPE_DIM) ** -0.5 * LOG2E,
                             w_scale=(IDX_HEADS * IDX_DIM) ** -0.5)
    tab = pl.BlockSpec((ROW_BLOCK, LANES), row)
    return pl.pallas_call(
        kern, grid=(t // ROW_BLOCK,),
        in_specs=[pl.BlockSpec((ROW_BLOCK, d), row), pl.BlockSpec(w_in_p.shape, full),
                  pl.BlockSpec((1, MLA_Q_RANK), full), pl.BlockSpec((1, MLA_KV_RANK), full),
                  pl.BlockSpec(w_uq.shape, full), pl.BlockSpec(w_qidx.shape, full),
                  pl.BlockSpec(w_ukt.shape, full3)] + [tab] * 6,
        out_specs=[pl.BlockSpec((n_qb, MLA_FEAT, MLA_HEADS * qb), lambda i: (i, 0, 0)),
                   pl.BlockSpec((ROW_BLOCK, MLA_FEAT), row),
                   pl.BlockSpec((1, MLA_KV_RANK, ROW_BLOCK), lambda i: (i, 0, 0)),
                   pl.BlockSpec((n_qb, n_pairs, LANES, qb), lambda i: (i, 0, 0, 0)),
                   pl.BlockSpec((ROW_BLOCK, LANES), row),
                   pl.BlockSpec((IDX_HEADS, ROW_BLOCK), lambda i: (0, i))],
        out_shape=[jax.ShapeDtypeStruct((t // qb, MLA_FEAT, MLA_HEADS * qb), BF16),
                   jax.ShapeDtypeStruct((t, MLA_FEAT), BF16),
                   jax.ShapeDtypeStruct((t // ROW_BLOCK, MLA_KV_RANK, ROW_BLOCK), BF16),
                   jax.ShapeDtypeStruct((t // qb, n_pairs, LANES, qb), BF16),
                   jax.ShapeDtypeStruct((t, LANES), BF16),
                   jax.ShapeDtypeStruct((IDX_HEADS, t), F32)],
        compiler_params=_params(1), name="dsa_proj",
    )(h2, w_in_p, gq.reshape(1, -1), gkv.reshape(1, -1), w_uq, w_qidx, w_ukt, *tabs_r, *tabs_i)


def _sortable_key(x):
    bits = lax.bitcast_convert_type(x, I32)
    return bits ^ ((bits >> 31) & 0x7FFFFFFF)


def _dsa_attn_kernel(qit_ref, wit_ref, qft_ref, ki_ref, kv_ref, kvt_ref, wuvt_ref, o_ref,
                     key_ref, thr_ref, m_ref, l_ref, acc_ref, *, kt, top_k):
    g = pl.program_id(1)
    qb = QUERY_BLOCK
    n_tiles = ((g + 1) * qb + kt - 1) // kt
    n_pairs = IDX_HEADS // 2
    lane_q = lax.broadcasted_iota(I32, (1, qb), 1)
    n_allowed = (g * (qb // CHUNK) + 1 + lane_q // CHUNK) * CHUNK

    feat = lax.broadcasted_iota(I32, (LANES, qb), 0)
    pair_w = []
    for p in range(n_pairs):
        slab = qit_ref[0, p]
        zero = jnp.zeros_like(slab)
        pair_w.append(jnp.concatenate([jnp.where(feat < IDX_DIM, slab, zero),
                                       jnp.where(feat >= IDX_DIM, slab, zero)], axis=1))
    wt = wit_ref[...]

    def score_tile(t, carry):
        lo, hi = carry
        k = ki_ref[pl.ds(pl.multiple_of(t * kt, kt), kt), :]
        sc = jnp.zeros((kt, qb), F32)
        for p in range(n_pairs):
            lg = _dot(k, pair_w[p])
            sc = (sc + jnp.maximum(lg[:, 0:qb], 0.0) * wt[2 * p:2 * p + 1, :]
                  + jnp.maximum(lg[:, qb:2 * qb], 0.0) * wt[2 * p + 1:2 * p + 2, :])
        kk = t * kt + lax.broadcasted_iota(I32, sc.shape, 0)
        valid = kk < n_allowed
        key_ref[t] = _sortable_key(jnp.where(valid, sc, NEG_INF))
        lo = jnp.minimum(lo, jnp.min(jnp.where(valid, sc, float("inf")), axis=0, keepdims=True))
        hi = jnp.maximum(hi, jnp.max(jnp.where(valid, sc, NEG_INF), axis=0, keepdims=True))
        return lo, hi

    lo_f, hi_f = lax.fori_loop(0, n_tiles, score_tile,
                               (jnp.full((1, qb), float("inf"), F32), jnp.full((1, qb), NEG_INF, F32)))

    keep_all = jnp.full((1, qb), NEG_INF_KEY + 1, I32)
    thr_ref[...] = keep_all

    @pl.when((g + 1) * qb > top_k)
    def _():
        def count_ge(mid):
            def body(t, cnt):
                hit = jnp.where(key_ref[t] >= mid, 1, 0)
                return cnt + jnp.sum(hit.reshape(kt // 8, 8, qb), axis=0)
            cnt = lax.fori_loop(0, n_tiles, body, jnp.zeros((8, qb), I32))
            return jnp.sum(cnt.astype(F32), axis=0, keepdims=True)

        def probe(lo, hi, mid):
            cnt = count_ge(mid)
            ge = cnt >= float(top_k)
            lo_next = jnp.where(ge, mid, lo)
            hi_next = jnp.where(cnt == float(top_k), mid, jnp.where(ge, hi, mid - 1))
            return lo_next, hi_next

        def bisect(st):
            lo, hi = st
            for _ in range(BISECT_STEPS_PER_CHECK):
                mid = (lo >> 1) + (hi >> 1) + ((lo | hi) & 1)
                lo, hi = probe(lo, hi, mid)
            return lo, hi

        def unresolved(st):
            lo, hi = st
            return jnp.max(jnp.where(hi > lo, 1.0, 0.0)) > 0.0

        lo, hi = _sortable_key(lo_f), _sortable_key(hi_f)
        for frac in BISECT_OPENING_FRACTIONS:
            split = jnp.minimum(jnp.maximum(_sortable_key(hi_f * frac), lo + 1), hi)
            lo, hi = probe(lo, hi, jnp.where(hi > lo, split, lo))
        lo, _ = lax.while_loop(unresolved, bisect, (lo, hi))
        thr_ref[...] = jnp.where(n_allowed > top_k, lo, keep_all)

    thr = thr_ref[...]
    m_ref[...] = jnp.full(m_ref.shape, NEG_INF, F32)
    l_ref[...] = jnp.zeros(l_ref.shape, F32)
    acc_ref[...] = jnp.zeros(acc_ref.shape, F32)
    hpg = HEADS_PER_GROUP
    group = hpg * qb

    def tile_bias(t):
        bias = jnp.where(key_ref[t] >= thr, 0.0, NEG_INF)
        return jnp.concatenate([bias] * hpg, axis=1)

    def scores(t, gi, bias_g):
        kv_rows = kv_ref[pl.ds(pl.multiple_of(t * kt, kt), kt), :]
        return _dot(kv_rows, qft_ref[0, :, gi * group:(gi + 1) * group]) + bias_g

    def softmax_pv(t, gi, s):
        m_old = m_ref[gi]
        m_new = jnp.maximum(m_old, jnp.max(s, axis=0, keepdims=True))
        m_safe = jnp.where(m_new == NEG_INF, 0.0, m_new)
        p = jnp.exp2(s - m_safe)
        alpha = jnp.exp2(m_old - m_safe)
        l_ref[gi] = alpha * l_ref[gi] + jnp.sum(p, axis=0, keepdims=True)
        acc_ref[gi] = alpha * acc_ref[gi] + _dot(kvt_ref[t], p.astype(BF16))
        m_ref[gi] = m_new

    def run(tiles):
        items = [(ti, gi) for ti in range(len(tiles)) for gi in range(MLA_HEADS // hpg)]
        ahead = 4
        bias, pending = {}, {}

        def issue(n):
            ti, gi = items[n]
            if gi == 0:
                bias[ti] = tile_bias(tiles[ti])
            pending[n] = scores(tiles[ti], gi, bias[ti])

        for n in range(min(ahead, len(items))):
            issue(n)
        for n, (ti, gi) in enumerate(items):
            if n + ahead < len(items):
                issue(n + ahead)
            softmax_pv(tiles[ti], gi, pending.pop(n))

    def tile_pair(tt, carry):
        run([2 * tt, 2 * tt + 1])
        return carry

    lax.fori_loop(0, n_tiles // 2, tile_pair, 0)

    @pl.when(n_tiles % 2 == 1)
    def _():
        run([n_tiles - 1])

    for gi in range(MLA_HEADS // hpg):
        o_lat_t = (acc_ref[gi] * (1.0 / l_ref[gi])).astype(BF16)
        for hl in range(hpg):
            hh = gi * hpg + hl
            o_t = _dot(wuvt_ref[hh], o_lat_t[:, hl * qb:(hl + 1) * qb])
            o_ref[:, hh * MLA_V_DIM:(hh + 1) * MLA_V_DIM] = o_t.T.astype(BF16)


def _dsa_attn(qft, kv, kvt, qit, ki, wit, w_uvt, bsz, seq):
    kt = KEY_TILE
    qb = QUERY_BLOCK
    ng = seq // qb
    n_pairs = IDX_HEADS // 2
    top_k = min(IDX_TOPK_MAX, seq // 4)
    kern = functools.partial(_dsa_attn_kernel, kt=kt, top_k=top_k)
    batch2 = lambda b, g: (b, 0)
    n_groups, group = MLA_HEADS // HEADS_PER_GROUP, HEADS_PER_GROUP * qb
    return pl.pallas_call(
        kern, grid=(bsz, ng),
        in_specs=[pl.BlockSpec((1, n_pairs, LANES, qb), lambda b, g: (b * ng + g, 0, 0, 0)),
                  pl.BlockSpec((IDX_HEADS, qb), lambda b, g: (0, b * ng + g)),
                  pl.BlockSpec((1, MLA_FEAT, MLA_HEADS * qb), lambda b, g: (b * ng + g, 0, 0)),
                  pl.BlockSpec((seq, LANES), batch2),
                  pl.BlockSpec((seq, MLA_FEAT), batch2),
                  pl.BlockSpec((seq // kt, MLA_KV_RANK, kt), lambda b, g: (b, 0, 0)),
                  pl.BlockSpec(w_uvt.shape, lambda b, g: (0, 0, 0))],
        out_specs=pl.BlockSpec((qb, MLA_HEADS * MLA_V_DIM), lambda b, g: (b * ng + g, 0)),
        out_shape=jax.ShapeDtypeStruct((bsz * seq, MLA_HEADS * MLA_V_DIM), BF16),
        scratch_shapes=[pltpu.VMEM((seq // kt, kt, qb), I32), pltpu.VMEM((1, qb), I32),
                        pltpu.VMEM((n_groups, 1, group), F32), pltpu.VMEM((n_groups, 1, group), F32),
                        pltpu.VMEM((n_groups, MLA_KV_RANK, group), F32)],
        compiler_params=_params(2), name="dsa_attn",
    )(qit, wit, qft, ki, kv, kvt, w_uvt)


def _diff_lambda_init(layer_idx):
    import math
    return 0.8 - 0.6 * math.exp(-0.3 * layer_idx)


def _pad_dsa_w_in(w_in):
    d = w_in.shape[0]
    o_cq, o_ckv = 0, MLA_Q_RANK
    o_kr = o_ckv + MLA_KV_RANK
    o_ki = o_kr + MLA_ROPE_DIM
    o_wi = o_ki + IDX_DIM
    out = jnp.zeros((d, _COL_END), w_in.dtype)
    out = out.at[:, _COL_CQ:_COL_CQ + MLA_Q_RANK].set(w_in[:, o_cq:o_ckv])
    out = out.at[:, _COL_CKV:_COL_CKV + MLA_KV_RANK].set(w_in[:, o_ckv:o_kr])
    out = out.at[:, _COL_KROPE:_COL_KROPE + MLA_ROPE_DIM].set(w_in[:, o_kr:o_ki])
    out = out.at[:, _COL_KIDX:_COL_KIDX + IDX_DIM].set(w_in[:, o_ki:o_wi])
    out = out.at[:, _COL_KIDX + IDX_DIM:_COL_KIDX + 2 * IDX_DIM].set(w_in[:, o_ki:o_wi])
    out = out.at[:, _COL_WIDX:_COL_WIDX + IDX_HEADS].set(w_in[:, o_wi:o_wi + IDX_HEADS])
    return out.astype(BF16)


def _router_weights(w_group, b_group, w_expert, b_expert):
    d = w_group.shape[0]
    w = jnp.zeros((d, LANES), F32)
    w = w.at[:, 0:N_GROUPS].set(w_group).at[:, N_GROUPS:N_GROUPS + N_EXPERTS].set(w_expert)
    br = jnp.zeros((1, LANES), F32)
    br = br.at[0, 0:N_GROUPS].set(b_group).at[0, N_GROUPS:N_GROUPS + N_EXPERTS].set(b_expert)
    w_hi = w.astype(BF16)
    w_lo = (w - w_hi.astype(F32)).astype(BF16)
    return w_hi, w_lo, br


def kernel(x, mem, positions, a_w_in, a_lambda, a_subln_g, a_w_out, b_w_in, b_q_norm_g, b_kv_norm_g,
           b_w_uq, b_w_qidx, b_w_uk, b_w_uv, b_w_out, mem_w_kv, xa_w_q, xa_w_out,
           moe_w_group, moe_b_group, moe_w_expert, moe_b_expert, moe_w_gate, moe_w_up, moe_w_down,
           ln_g, ln_b):
    bsz, seq, d = x.shape
    n_mem = mem.shape[1]
    depth = ln_g.shape[0]
    alpha = (2.0 * depth) ** 0.25
    assert seq % ROW_BLOCK == 0 and seq % KEY_TILE == 0
    assert ROW_BLOCK == ATTN_BLOCK and ROW_BLOCK == KEY_TILE and QUERY_BLOCK == 2 * CHUNK

    tabs_d = _rope_lane_tables(positions, DIFF_ROPE_DIM, DIFF_HEAD_DIM, True)
    tabs_r = _rope_lane_tables(positions, MLA_ROPE_DIM, LANES, False)
    tabs_i = _rope_lane_tables(positions, IDX_ROPE_DIM, IDX_DIM, True)

    memkv = _mem_kv(mem.reshape(bsz * n_mem, d), mem_w_kv.astype(BF16))

    h = x.reshape(bsz * seq, d)
    for i in range(depth):
        j = i // N_MIXERS
        if i % N_MIXERS == 0:
            qt, k, vt = _diff_qkv(h, a_w_in[j].astype(BF16), tabs_d)
            mix = _diff_attn(qt, k, vt, a_lambda[j], a_subln_g[j], bsz, seq, _diff_lambda_init(i))
            w_mix_out = a_w_out[j]
        else:
            w_ukt = jnp.pad(jnp.swapaxes(b_w_uk[j], 1, 2), ((0, 0), (MLA_ROPE_DIM, 0), (0, 0))).astype(BF16)
            qft, kv, kvt, qit, ki, wit = _dsa_proj(h, _pad_dsa_w_in(b_w_in[j]), b_q_norm_g[j], b_kv_norm_g[j],
                                                   b_w_uq[j].astype(BF16), b_w_qidx[j].astype(BF16), w_ukt,
                                                   tabs_r, tabs_i)
            w_uvt = jnp.swapaxes(b_w_uv[j], 1, 2).astype(BF16)
            mix = _dsa_attn(qft, kv, kvt, qit, ki, wit, w_uvt, bsz, seq)
            w_mix_out = b_w_out[j]
        wr_hi, wr_lo, br = _router_weights(moe_w_group[i], moe_b_group[i], moe_w_expert[i], moe_b_expert[i])
        ln_rows = jnp.stack([ln_g[i, 0], ln_b[i, 0], ln_g[i, 1], ln_b[i, 1]])
        aug, meta, counts = _layer_mid(mix, w_mix_out.astype(BF16), h, ln_rows, xa_w_q[i].astype(BF16), memkv,
                                       xa_w_out[i].astype(BF16), wr_hi, wr_lo, br, alpha, seq, n_mem)
        h = _moe(aug, meta, counts, moe_w_gate[i].astype(BF16), moe_w_up[i].astype(BF16),
                 moe_w_down[i].astype(BF16), ln_g[i, 2], ln_b[i, 2], alpha)
    return h.reshape(bsz, seq, d)
```

```python
import functools

import jax
import jax.numpy as jnp
from jax import lax
from jax.experimental import pallas as pl
from jax.experimental.pallas import tpu as pltpu

F32 = jnp.float32
BF16 = jnp.bfloat16
I32 = jnp.int32

CHUNK = 64
ROPE_THETA = 500000.0
LN_EPS = 1e-5
N_MIXERS = 2
DIFF_HEADS = 8
DIFF_HEAD_DIM = 64
DIFF_ROPE_DIM = DIFF_HEAD_DIM // 4
MLA_HEADS = 8
MLA_Q_RANK = 256
MLA_KV_RANK = 256
MLA_ROPE_DIM = 32
MLA_NOPE_DIM = 96
MLA_V_DIM = 128
IDX_HEADS = 16
IDX_DIM = 64
IDX_ROPE_DIM = IDX_DIM // 4
IDX_TOPK_MAX = 256
MEM_HEADS = 4
N_GROUPS = 4
EXPERTS_PER_GROUP = 4
N_EXPERTS = N_GROUPS * EXPERTS_PER_GROUP

LANES = 128
VMEM_LIMIT_BYTES = 56 * 1024 * 1024

ROW_BLOCK = 512
ATTN_BLOCK = 512
DIFF_BLOCKS_PER_TRIP = 3
KEY_TILE = 512
QUERY_BLOCK = 128
HEADS_PER_GROUP = 2
BISECT_STEPS_PER_CHECK = 4
BISECT_OPENING_FRACTIONS = (0.125, 0.5)
MLA_FEAT = MLA_KV_RANK + LANES

NEG_INF = float("-inf")
NEG_INF_KEY = -2139095041

_NT = (((1,), (1,)), ((), ()))


def _params(n_axes):
    return pltpu.CompilerParams(dimension_semantics=("arbitrary",) * n_axes,
                                vmem_limit_bytes=VMEM_LIMIT_BYTES)


def _dot(a, b):
    return jnp.dot(a, b, preferred_element_type=F32)


def _dot_nt(a, b):
    return lax.dot_general(a, b, _NT, preferred_element_type=F32)


def _layer_norm(z, g, b):
    mu = jnp.mean(z, axis=-1, keepdims=True)
    zc = z - mu
    var = jnp.mean(zc * zc, axis=-1, keepdims=True)
    return zc * lax.rsqrt(var + LN_EPS) * g + b


def _rms_norm(x, g):
    return x * lax.rsqrt(jnp.mean(x * x, axis=-1, keepdims=True) + LN_EPS) * g


def _rope(y, c, s_up, s_dn, half):
    return (y * c + pltpu.roll(y, LANES - half, 1) * s_up + pltpu.roll(y, half, 1) * s_dn)


def _rope_lane_tables(positions, rot_dim, period, keep_rest):
    half = rot_dim // 2
    inv_freq = ROPE_THETA ** (-jnp.arange(0, rot_dim, 2, dtype=F32) / rot_dim)
    ang = positions.astype(F32)[..., None] * inv_freq
    cos, sin = jnp.cos(ang), jnp.sin(ang)
    lead = positions.shape
    rest = jnp.full(lead + (period - rot_dim,), 1.0 if keep_rest else 0.0, F32)
    zrest = jnp.zeros(lead + (period - rot_dim,), F32)
    zhalf = jnp.zeros(lead + (half,), F32)
    reps = LANES // period
    out = []
    for parts in ((cos, cos, rest), (-sin, zhalf, zrest), (zhalf, sin, zrest)):
        t = jnp.concatenate(parts, axis=-1)
        out.append(jnp.tile(t, (1,) * len(lead) + (reps,)).reshape(-1, LANES))
    return out


LOG2E = 1.4426950408889634


def _diff_qkv_kernel(h_ref, w_ref, c_ref, su_ref, sd_ref, qt_ref, k_ref, vt_ref, *, q_scale, half):
    x = h_ref[...].astype(BF16)
    c, su, sd = c_ref[...], su_ref[...], sd_ref[...]
    n_heads = qt_ref.shape[0]
    for j2 in range(3 * n_heads // 2):
        y2 = _dot(x, w_ref[:, j2 * 2 * LANES:(j2 + 1) * 2 * LANES])
        for s in range(2):
            j = 2 * j2 + s
            y = y2[:, s * LANES:(s + 1) * LANES]
            if j < n_heads:
                qt_ref[j, 0] = (_rope(y, c, su, sd, half) * q_scale).T.astype(BF16)
            elif j < 2 * n_heads:
                k_ref[:, (j - n_heads) * LANES:(j - n_heads + 1) * LANES] = _rope(y, c, su, sd, half).astype(BF16)
            else:
                vt_ref[j - 2 * n_heads, 0] = y.T.astype(BF16)


def _diff_qkv(h2, w_in, tabs):
    t, d = h2.shape
    hd = 2 * DIFF_HEAD_DIM
    nb = t // ROW_BLOCK
    kern = functools.partial(_diff_qkv_kernel, q_scale=DIFF_HEAD_DIM ** -0.5 * LOG2E, half=DIFF_ROPE_DIM // 2)
    row = lambda i: (i, 0)
    full = lambda i: (0, 0)
    t_shape = jax.ShapeDtypeStruct((DIFF_HEADS, nb, hd, ROW_BLOCK), BF16)
    t_spec = pl.BlockSpec((DIFF_HEADS, 1, hd, ROW_BLOCK), lambda i: (0, i, 0, 0))
    return pl.pallas_call(
        kern, grid=(nb,),
        in_specs=[pl.BlockSpec((ROW_BLOCK, d), row), pl.BlockSpec(w_in.shape, full)]
        + [pl.BlockSpec((ROW_BLOCK, LANES), row)] * 3,
        out_specs=[t_spec, pl.BlockSpec((ROW_BLOCK, DIFF_HEADS * hd), row), t_spec],
        out_shape=[t_shape, jax.ShapeDtypeStruct((t, DIFF_HEADS * hd), BF16), t_shape],
        compiler_params=_params(1), name="diff_qkv",
    )(h2, w_in, *tabs)


def _diff_attn_kernel(lam_ref, g_ref, qt_ref, k_ref, vt_ref, o_ref, m_ref, l_ref, acc_ref, *,
                      blk, lambda_init):
    i = pl.program_id(2)
    lam = lam_ref[...]
    lam_full = (jnp.exp(jnp.sum(lam[0:1] * lam[1:2], axis=1, keepdims=True))
                - jnp.exp(jnp.sum(lam[2:3] * lam[3:4], axis=1, keepdims=True)) + lambda_init)
    qt = qt_ref[0, 0]
    feat = lax.broadcasted_iota(I32, qt.shape, 0)
    zero = jnp.zeros_like(qt)
    q_maps = (jnp.where(feat < DIFF_HEAD_DIM, qt, zero), jnp.where(feat >= DIFF_HEAD_DIM, qt, zero))
    m_ref[...] = jnp.full(m_ref.shape, NEG_INF, F32)
    l_ref[...] = jnp.zeros(l_ref.shape, F32)
    acc_ref[...] = jnp.zeros(acc_ref.shape, F32)

    half_w = blk // 2

    def scores(j, c, hf, masked):
        k = k_ref[pl.ds(pl.multiple_of(j * blk, blk), blk), :]
        s = _dot(k, q_maps[c][:, hf * half_w:(hf + 1) * half_w])
        if masked:
            kc = lax.broadcasted_iota(I32, s.shape, 0) // CHUNK
            qc = (hf * half_w + lax.broadcasted_iota(I32, s.shape, 1)) // CHUNK
            s = jnp.where(kc <= qc, s, NEG_INF)
        return s

    def softmax_pv(j, c, hf, s):
        idx = 2 * c + hf
        m_old = m_ref[idx]
        m_new = jnp.maximum(m_old, jnp.max(s, axis=0, keepdims=True))
        p = jnp.exp2(s - m_new)
        alpha = jnp.exp2(m_old - m_new)
        l_ref[idx] = alpha * l_ref[idx] + jnp.sum(p, axis=0, keepdims=True)
        acc_ref[idx] = alpha * acc_ref[idx] + _dot(vt_ref[0, j], p.astype(BF16))
        m_ref[idx] = m_new

    def run(blocks):
        items = [(j, c, hf, masked) for (j, masked) in blocks for c in range(2) for hf in range(2)]
        ahead = 4
        pending = {n: scores(*items[n]) for n in range(min(ahead, len(items)))}
        for n, (j, c, hf, _) in enumerate(items):
            if n + ahead < len(items):
                pending[n + ahead] = scores(*items[n + ahead])
            softmax_pv(j, c, hf, pending.pop(n))

    per_trip = DIFF_BLOCKS_PER_TRIP

    def full_group(jj, carry):
        run([(per_trip * jj + u, False) for u in range(per_trip)])
        return carry

    n_trips = i // per_trip
    lax.fori_loop(0, n_trips, full_group, 0)

    left = i - n_trips * per_trip
    for extra in range(per_trip):
        @pl.when(left == extra)
        def _(extra=extra):
            run([(i - extra + u, False) for u in range(extra)] + [(i, True)])

    norm = [acc_ref[idx] * (1.0 / l_ref[idx]) for idx in range(4)]
    ot = (jnp.concatenate(norm[0:2], axis=1) - lam_full * jnp.concatenate(norm[2:4], axis=1))
    ot = ot * lax.rsqrt(jnp.mean(ot * ot, axis=0, keepdims=True) + LN_EPS) * (1.0 - lambda_init)
    g = g_ref[...]
    for s in range(blk // LANES):
        o_ref[s * LANES:(s + 1) * LANES, :] = (ot[:, s * LANES:(s + 1) * LANES] * g).T.astype(BF16)


def _diff_attn(qt, k, vt, lam, subln_g, bsz, seq, lambda_init):
    blk = ATTN_BLOCK
    nq = seq // blk
    hd = 2 * DIFF_HEAD_DIM
    kern = functools.partial(_diff_attn_kernel, blk=blk, lambda_init=lambda_init)
    g = jnp.broadcast_to(subln_g.astype(F32)[:, None], (hd, LANES))
    return pl.pallas_call(
        kern, grid=(bsz, DIFF_HEADS, nq),
        in_specs=[
            pl.BlockSpec(lam.shape, lambda b, h, i: (0, 0)),
            pl.BlockSpec((hd, LANES), lambda b, h, i: (0, 0)),
            pl.BlockSpec((1, 1, hd, blk), lambda b, h, i: (h, b * nq + i, 0, 0)),
            pl.BlockSpec((seq, hd), lambda b, h, i: (b, h)),
            pl.BlockSpec((1, nq, hd, blk), lambda b, h, i: (h, b, 0, 0)),
        ],
        out_specs=pl.BlockSpec((blk, hd), lambda b, h, i: (b * nq + i, h)),
        out_shape=jax.ShapeDtypeStruct((bsz * seq, DIFF_HEADS * hd), BF16),
        scratch_shapes=[pltpu.VMEM((4, 1, blk // 2), F32), pltpu.VMEM((4, 1, blk // 2), F32),
                        pltpu.VMEM((4, hd, blk // 2), F32)],
        compiler_params=_params(3), name="diff_attn",
    )(lam, g, qt, k, vt)


def _mem_kv_kernel(m_ref, w_ref, o_ref):
    o_ref[...] = _dot(m_ref[...].astype(BF16), w_ref[...]).astype(BF16)


def _mem_kv(mem2, w_kv):
    rows, d = mem2.shape
    n = w_kv.shape[1]
    blk = min(ROW_BLOCK, rows)
    return pl.pallas_call(
        _mem_kv_kernel, grid=(rows // blk,),
        in_specs=[pl.BlockSpec((blk, d), lambda i: (i, 0)), pl.BlockSpec((d, n), lambda i: (0, 0))],
        out_specs=pl.BlockSpec((blk, n), lambda i: (i, 0)),
        out_shape=jax.ShapeDtypeStruct((rows, n), BF16),
        compiler_params=_params(1), name="mem_kv",
    )(mem2, w_kv)


def _cross_attn_block(h, wq_ref, kv_ref, wo_ref, g, b, alpha):
    d = h.shape[1]
    hd = d // MEM_HEADS
    q = (_dot(h.astype(BF16), wq_ref[...]) * hd ** -0.5).astype(BF16)
    outs = []
    for hh in range(MEM_HEADS):
        s = _dot_nt(q[:, hh * hd:(hh + 1) * hd], kv_ref[:, hh * hd:(hh + 1) * hd])
        p = jnp.exp(s - jnp.max(s, axis=1, keepdims=True))
        p = p * (1.0 / jnp.sum(p, axis=1, keepdims=True))
        outs.append(_dot(p.astype(BF16), kv_ref[:, d + hh * hd:d + (hh + 1) * hd]).astype(BF16))
    o = jnp.concatenate(outs, axis=1)
    return _layer_norm(alpha * h + _dot(o, wo_ref[...]), g, b)


def _route(h, wr_hi, wr_lo, br):
    h_hi = h.astype(BF16)
    h_lo = (h - h_hi.astype(F32)).astype(BF16)
    logits = _dot(h_hi, wr_hi) + _dot(h_hi, wr_lo) + _dot(h_lo, wr_hi) + br
    lane = lax.broadcasted_iota(I32, logits.shape, 1).astype(F32)
    gl = jnp.where(lane < N_GROUPS, logits, NEG_INF)
    gmax = jnp.max(gl, axis=1, keepdims=True)
    g_sel = jnp.min(jnp.where(gl == gmax, lane, float(LANES)), axis=1, keepdims=True)
    g_gate = 1.0 / jnp.sum(jnp.exp(gl - gmax), axis=1, keepdims=True)
    first = N_GROUPS + g_sel * EXPERTS_PER_GROUP
    el = jnp.where((lane >= first) & (lane < first + EXPERTS_PER_GROUP), logits, NEG_INF)
    v1 = jnp.max(el, axis=1, keepdims=True)
    i1 = jnp.min(jnp.where(el == v1, lane, float(LANES)), axis=1, keepdims=True)
    el2 = jnp.where(lane == i1, NEG_INF, el)
    v2 = jnp.max(el2, axis=1, keepdims=True)
    i2 = jnp.min(jnp.where(el2 == v2, lane, float(LANES)), axis=1, keepdims=True)
    r = jnp.exp(v2 - v1)
    w1 = g_gate / (1.0 + r)
    w2 = w1 * r
    first_is_low = i1 < i2
    e_lo = jnp.minimum(i1, i2) - first
    e_hi = jnp.maximum(i1, i2) - first
    return g_sel, e_lo, e_hi, jnp.where(first_is_low, w1, w2), jnp.where(first_is_low, w2, w1)


PAIRS_PER_GROUP = EXPERTS_PER_GROUP * (EXPERTS_PER_GROUP - 1) // 2
N_BINS = N_GROUPS * PAIRS_PER_GROUP
MOE_TILE = 256
W_LO_LANE, W_HI_LANE = 0, 64
META_ROWS = 8
DMA_LOOP_UNROLL = 8


def _bin_pairs():
    return [(lo, hi) for lo in range(EXPERTS_PER_GROUP) for hi in range(lo + 1, EXPERTS_PER_GROUP)]


def _route_block(h, wrh_ref, wrl_ref, br_ref, aug_ref, meta_ref, cnt_ref, run_ref):
    rows, d = h.shape
    g_sel, e_lo, e_hi, w_lo, w_hi = _route(h, wrh_ref[...], wrl_ref[...], br_ref[...])
    pid = e_lo * (7.0 - e_lo) * 0.5 + e_hi - e_lo - 1.0
    bin_id = g_sel * PAIRS_PER_GROUP + pid
    lane = lax.broadcasted_iota(I32, (rows, LANES), 1).astype(F32)
    onehot = jnp.where(lane == bin_id, 1.0, 0.0)
    r_i = lax.broadcasted_iota(I32, (rows, rows), 0)
    c_i = lax.broadcasted_iota(I32, (rows, rows), 1)
    tri = jnp.where(c_i < r_i, 1.0, 0.0).astype(BF16)
    before = _dot(tri, onehot.astype(BF16)) + run_ref[...]
    rank = jnp.sum(before * onehot, axis=1, keepdims=True)
    run_ref[...] += jnp.sum(onehot, axis=0, keepdims=True)
    cnt_ref[...] = jnp.broadcast_to(run_ref[...], cnt_ref.shape)

    aug_ref[:, 0:d] = h
    aug_ref[:, d:d + LANES] = jnp.where(lane < W_HI_LANE, w_lo, w_hi)
    meta = jnp.where(lane == 0.0, bin_id, jnp.where(lane == 1.0, rank, 0.0))
    meta_ref[0] = meta.T[0:META_ROWS, :].astype(I32)


def _layer_mid_kernel(mix_ref, wmix_ref, h_ref, ln_ref, wq_ref, kv_ref, wo_ref, wrh_ref, wrl_ref, br_ref,
                      aug_ref, meta_ref, cnt_ref, run_ref, *, alpha):
    @pl.when(pl.program_id(0) == 0)
    def _():
        run_ref[...] = jnp.zeros(run_ref.shape, F32)

    ln = ln_ref[...]
    h1 = _layer_norm(alpha * h_ref[...] + _dot(mix_ref[...], wmix_ref[...]), ln[0:1], ln[1:2])
    h2 = _cross_attn_block(h1, wq_ref, kv_ref, wo_ref, ln[2:3], ln[3:4], alpha)
    _route_block(h2, wrh_ref, wrl_ref, br_ref, aug_ref, meta_ref, cnt_ref, run_ref)


def _layer_mid(mix, w_mix, h2, ln_rows, w_q, memkv, w_out, wr_hi, wr_lo, br, alpha, seq, n_mem):
    t, d = h2.shape
    k = mix.shape[1]
    nb = t // ROW_BLOCK
    per_batch = seq // ROW_BLOCK
    row = lambda i: (i, 0)
    full = lambda i: (0, 0)
    return pl.pallas_call(
        functools.partial(_layer_mid_kernel, alpha=alpha), grid=(nb,),
        in_specs=[pl.BlockSpec((ROW_BLOCK, k), row), pl.BlockSpec((k, d), full),
                  pl.BlockSpec((ROW_BLOCK, d), row), pl.BlockSpec(ln_rows.shape, full),
                  pl.BlockSpec((d, d), full), pl.BlockSpec((n_mem, 2 * d), lambda i: (i // per_batch, 0)),
                  pl.BlockSpec((d, d), full), pl.BlockSpec((d, LANES), full),
                  pl.BlockSpec((d, LANES), full), pl.BlockSpec((1, LANES), full)],
        out_specs=[pl.BlockSpec((ROW_BLOCK, d + LANES), row),
                   pl.BlockSpec((1, META_ROWS, ROW_BLOCK), lambda i: (i, 0, 0)),
                   pl.BlockSpec((8, LANES), full)],
        out_shape=[jax.ShapeDtypeStruct((t, d + LANES), F32),
                   jax.ShapeDtypeStruct((nb, META_ROWS, ROW_BLOCK), I32),
                   jax.ShapeDtypeStruct((8, LANES), F32)],
        scratch_shapes=[pltpu.VMEM((1, LANES), F32)],
        compiler_params=_params(1), name="layer_mid",
    )(mix, w_mix, h2, ln_rows, w_q, memkv, w_out, wr_hi, wr_lo, br)


def _row_dma_loops(copy):
    def issue(r8, carry):
        for u in range(DMA_LOOP_UNROLL):
            copy(r8 * DMA_LOOP_UNROLL + u).start(priority=u % 2)
        return carry

    def drain(r8, carry):
        for u in range(DMA_LOOP_UNROLL):
            copy(r8 * DMA_LOOP_UNROLL + u).wait()
        return carry

    lax.fori_loop(0, ROW_BLOCK // DMA_LOOP_UNROLL, issue, 0)
    lax.fori_loop(0, ROW_BLOCK // DMA_LOOP_UNROLL, drain, 0)


def _moe_dispatch_kernel(off_ref, meta_ref, src_ref, init_ref, dst_ref, sem):
    del init_ref

    def copy(r):
        slot = off_ref[meta_ref[0, 0, r]] + meta_ref[0, 1, r]
        return pltpu.make_async_copy(src_ref.at[pl.ds(r, 1), :], dst_ref.at[pl.ds(slot, 1), :], sem)

    _row_dma_loops(copy)


def _moe_combine_kernel(off_ref, meta_ref, src_ref, dst_ref, sem):
    def copy(r):
        slot = off_ref[meta_ref[0, 0, r]] + meta_ref[0, 1, r]
        return pltpu.make_async_copy(src_ref.at[pl.ds(slot, 1), :], dst_ref.at[pl.ds(r, 1), :], sem)

    _row_dma_loops(copy)


def _moe_permute(off, meta, src, out_rows, init=None):
    nb = meta.shape[0]
    width = src.shape[1]
    dispatch = init is not None
    any_spec = pl.BlockSpec(memory_space=pl.ANY)
    block_spec = pl.BlockSpec((ROW_BLOCK, width), lambda i, off: (i, 0))
    meta_spec = pl.BlockSpec((1, META_ROWS, ROW_BLOCK), lambda i, off: (i, 0, 0), memory_space=pltpu.SMEM)
    grid_spec = pltpu.PrefetchScalarGridSpec(
        num_scalar_prefetch=1, grid=(nb,),
        in_specs=[meta_spec, block_spec, any_spec] if dispatch else [meta_spec, any_spec],
        out_specs=any_spec if dispatch else block_spec,
        scratch_shapes=[pltpu.SemaphoreType.DMA(())])
    return pl.pallas_call(
        _moe_dispatch_kernel if dispatch else _moe_combine_kernel, grid_spec=grid_spec,
        out_shape=jax.ShapeDtypeStruct((out_rows, width), src.dtype),
        input_output_aliases=({3: 0} if dispatch else {}),
        compiler_params=_params(1), name="moe_dispatch" if dispatch else "moe_combine",
    )(*([off, meta, src, init] if dispatch else [off, meta, src]))


def _moe_expert_kernel(e_lo_ref, e_hi_ref, nv_ref, s_ref, wg1_ref, wu1_ref, wd1_ref, wg2_ref, wu2_ref, wd2_ref,
                       g_ref, b_ref, o_ref, *, alpha):
    del e_lo_ref, e_hi_ref
    occupied = pl.program_id(0) < nv_ref[0]

    @pl.when(jnp.logical_not(occupied))
    def _():
        o_ref[...] = jnp.zeros(o_ref.shape, F32)

    @pl.when(occupied)
    def _():
        d = o_ref.shape[1]
        x = s_ref[:, 0:d]
        xb = x.astype(BF16)
        y = jnp.zeros(x.shape, F32)
        for lane0, wg_ref, wu_ref, wd_ref in ((W_LO_LANE, wg1_ref, wu1_ref, wd1_ref),
                                              (W_HI_LANE, wg2_ref, wu2_ref, wd2_ref)):
            c = s_ref[:, d + lane0:d + lane0 + 1]
            a = _dot(xb, wg_ref[0])
            u = _dot(xb, wu_ref[0])
            hid = a * (1.0 / (1.0 + jnp.exp(-a))) * u
            y = y + _dot((c * hid).astype(BF16), wd_ref[0])
        o_ref[...] = _layer_norm(alpha * x + y, g_ref[...], b_ref[...])


def _moe_experts(sorted_rows, tile_e_lo, tile_e_hi, n_valid, w_gate, w_up, w_down, g, b, alpha):
    rows, width = sorted_rows.shape
    d = width - LANES
    _, _, ff = w_gate.shape
    n_tiles = rows // MOE_TILE
    tile = lambda i, lo, hi, nv: (jnp.minimum(i, nv[0] - 1), 0)
    out_tile = lambda i, lo, hi, nv: (i, 0)
    w_lo = lambda i, lo, hi, nv: (lo[jnp.minimum(i, nv[0] - 1)], 0, 0)
    w_hi = lambda i, lo, hi, nv: (hi[jnp.minimum(i, nv[0] - 1)], 0, 0)
    full = lambda i, lo, hi, nv: (0, 0)
    grid_spec = pltpu.PrefetchScalarGridSpec(
        num_scalar_prefetch=3, grid=(n_tiles,),
        in_specs=[pl.BlockSpec((MOE_TILE, width), tile),
                  pl.BlockSpec((1, d, ff), w_lo), pl.BlockSpec((1, d, ff), w_lo), pl.BlockSpec((1, ff, d), w_lo),
                  pl.BlockSpec((1, d, ff), w_hi), pl.BlockSpec((1, d, ff), w_hi), pl.BlockSpec((1, ff, d), w_hi),
                  pl.BlockSpec((1, d), full), pl.BlockSpec((1, d), full)],
        out_specs=pl.BlockSpec((MOE_TILE, d), out_tile))
    return pl.pallas_call(
        functools.partial(_moe_expert_kernel, alpha=alpha), grid_spec=grid_spec,
        out_shape=jax.ShapeDtypeStruct((rows, d), F32),
        compiler_params=_params(1), name="moe_experts",
    )(tile_e_lo, tile_e_hi, n_valid, sorted_rows, w_gate, w_up, w_down, w_gate, w_up, w_down,
      g.reshape(1, d), b.reshape(1, d))


def _moe(aug, meta, counts, w_gate, w_up, w_down, g, b, alpha):
    t, d = aug.shape[0], aug.shape[1] - LANES
    cnt = counts[0, 0:N_BINS].astype(I32)
    padded = (cnt + MOE_TILE - 1) // MOE_TILE * MOE_TILE
    ends = jnp.cumsum(padded)
    off = (ends - padded).astype(I32)
    n_tiles = t // MOE_TILE + N_BINS
    tile_start = jnp.arange(n_tiles, dtype=I32) * MOE_TILE
    tile_bin = jnp.minimum(jnp.sum((ends[None, :] <= tile_start[:, None]).astype(I32), axis=1), N_BINS - 1)
    pairs = jnp.asarray(_bin_pairs(), I32)
    group = tile_bin // PAIRS_PER_GROUP
    tile_e_lo = group * EXPERTS_PER_GROUP + pairs[tile_bin % PAIRS_PER_GROUP, 0]
    tile_e_hi = group * EXPERTS_PER_GROUP + pairs[tile_bin % PAIRS_PER_GROUP, 1]
    n_valid = (ends[-1:] // MOE_TILE).astype(I32)
    sorted_rows = _moe_permute(off, meta, aug, n_tiles * MOE_TILE,
                               init=jnp.zeros((n_tiles * MOE_TILE, d + LANES), F32))
    out_sorted = _moe_experts(sorted_rows, tile_e_lo, tile_e_hi, n_valid, w_gate, w_up, w_down, g, b, alpha)
    return _moe_permute(off, meta, out_sorted, t)


_COL_CQ, _COL_CKV, _COL_KROPE, _COL_KIDX, _COL_WIDX, _COL_END = 0, 256, 512, 640, 768, 896


def _dsa_proj_kernel(h_ref, win_ref, gq_ref, gkv_ref, wuq_ref, wqi_ref, wuk_ref,
                     cr_ref, sur_ref, sdr_ref, ci_ref, sui_ref, sdi_ref,
                     qft_ref, kv_ref, kvt_ref, qit_ref, ki_ref, wit_ref, *, q_scale, w_scale):
    y = _dot(h_ref[...].astype(BF16), win_ref[...])
    cr, sur, sdr = cr_ref[...], sur_ref[...], sdr_ref[...]
    ci, sui, sdi = ci_ref[...], sui_ref[...], sdi_ref[...]
    half_r, half_i = MLA_ROPE_DIM // 2, IDX_ROPE_DIM // 2
    qb = QUERY_BLOCK
    n_qb = y.shape[0] // qb
    c_q = _rms_norm(y[:, _COL_CQ:_COL_CKV], gq_ref[...]).astype(BF16)
    c_kv = _rms_norm(y[:, _COL_CKV:_COL_KROPE], gkv_ref[...])
    kv_ref[:, 0:MLA_KV_RANK] = c_kv.astype(BF16)
    kvt_ref[0] = c_kv.T.astype(BF16)
    kv_ref[:, MLA_KV_RANK:MLA_FEAT] = _rope(y[:, _COL_KROPE:_COL_KIDX], cr, sur, sdr, half_r).astype(BF16)
    ki_ref[...] = _rope(y[:, _COL_KIDX:_COL_WIDX], ci, sui, sdi, half_i).astype(BF16)
    wit_ref[...] = (y[:, _COL_WIDX:_COL_END] * w_scale).T[0:IDX_HEADS, :]
    q = _dot(c_q, wuq_ref[...])
    q_bf = q.astype(BF16)
    hd = MLA_ROPE_DIM + MLA_NOPE_DIM
    for hh in range(MLA_HEADS):
        lat_t = (_dot(q_bf[:, hh * hd:(hh + 1) * hd], wuk_ref[hh]) * q_scale).T.astype(BF16)
        rope_t = (_rope(q[:, hh * hd:(hh + 1) * hd], cr, sur, sdr, half_r) * q_scale).T.astype(BF16)
        for bl in range(n_qb):
            qft_ref[bl, 0:MLA_KV_RANK, hh * qb:(hh + 1) * qb] = lat_t[:, bl * qb:(bl + 1) * qb]
            qft_ref[bl, MLA_KV_RANK:MLA_FEAT, hh * qb:(hh + 1) * qb] = rope_t[:, bl * qb:(bl + 1) * qb]
    qi = _dot(c_q, wqi_ref[...])
    for p in range(IDX_HEADS // 2):
        pair_t = _rope(qi[:, p * LANES:(p + 1) * LANES], ci, sui, sdi, half_i).T.astype(BF16)
        for bl in range(n_qb):
            qit_ref[bl, p] = pair_t[:, bl * qb:(bl + 1) * qb]


def _dsa_proj(h2, w_in_p, gq, gkv, w_uq, w_qidx, w_ukt, tabs_r, tabs_i):
    t, d = h2.shape
    row = lambda i: (i, 0)
    full = lambda i: (0, 0)
    full3 = lambda i: (0, 0, 0)
    n_pairs = IDX_HEADS // 2
    qb = QUERY_BLOCK
    n_qb = ROW_BLOCK // qb
    kern = functools.partial(_dsa_proj_kernel, q_scale=(MLA_ROPE_DIM + MLA_NOPE_DIM) ** -0.5 * LOG2E,
                             w_scale=(IDX_HEADS * IDX_DIM) ** -0.5)
    tab = pl.BlockSpec((ROW_BLOCK, LANES), row)
    return pl.pallas_call(
        kern, grid=(t // ROW_BLOCK,),
        in_specs=[pl.BlockSpec((ROW_BLOCK, d), row), pl.BlockSpec(w_in_p.shape, full),
                  pl.BlockSpec((1, MLA_Q_RANK), full), pl.BlockSpec((1, MLA_KV_RANK), full),
                  pl.BlockSpec(w_uq.shape, full), pl.BlockSpec(w_qidx.shape, full),
                  pl.BlockSpec(w_ukt.shape, full3)] + [tab] * 6,
        out_specs=[pl.BlockSpec((n_qb, MLA_FEAT, MLA_HEADS * qb), lambda i: (i, 0, 0)),
                   pl.BlockSpec((ROW_BLOCK, MLA_FEAT), row),
                   pl.BlockSpec((1, MLA_KV_RANK, ROW_BLOCK), lambda i: (i, 0, 0)),
                   pl.BlockSpec((n_qb, n_pairs, LANES, qb), lambda i: (i, 0, 0, 0)),
                   pl.BlockSpec((ROW_BLOCK, LANES), row),
                   pl.BlockSpec((IDX_HEADS, ROW_BLOCK), lambda i: (0, i))],
        out_shape=[jax.ShapeDtypeStruct((t // qb, MLA_FEAT, MLA_HEADS * qb), BF16),
                   jax.ShapeDtypeStruct((t, MLA_FEAT), BF16),
                   jax.ShapeDtypeStruct((t // ROW_BLOCK, MLA_KV_RANK, ROW_BLOCK), BF16),
                   jax.ShapeDtypeStruct((t // qb, n_pairs, LANES, qb), BF16),
                   jax.ShapeDtypeStruct((t, LANES), BF16),
                   jax.ShapeDtypeStruct((IDX_HEADS, t), F32)],
        compiler_params=_params(1), name="dsa_proj",
    )(h2, w_in_p, gq.reshape(1, -1), gkv.reshape(1, -1), w_uq, w_qidx, w_ukt, *tabs_r, *tabs_i)


def _sortable_key(x):
    bits = lax.bitcast_convert_type(x, I32)
    return bits ^ ((bits >> 31) & 0x7FFFFFFF)


def _dsa_attn_kernel(qit_ref, wit_ref, qft_ref, ki_ref, kv_ref, kvt_ref, wuvt_ref, o_ref,
                     key_ref, thr_ref, m_ref, l_ref, acc_ref, *, kt, top_k):
    g = pl.program_id(1)
    qb = QUERY_BLOCK
    n_tiles = ((g + 1) * qb + kt - 1) // kt
    n_pairs = IDX_HEADS // 2
    lane_q = lax.broadcasted_iota(I32, (1, qb), 1)
    n_allowed = (g * (qb // CHUNK) + 1 + lane_q // CHUNK) * CHUNK

    feat = lax.broadcasted_iota(I32, (LANES, qb), 0)
    pair_w = []
    for p in range(n_pairs):
        slab = qit_ref[0, p]
        zero = jnp.zeros_like(slab)
        pair_w.append(jnp.concatenate([jnp.where(feat < IDX_DIM, slab, zero),
                                       jnp.where(feat >= IDX_DIM, slab, zero)], axis=1))
    wt = wit_ref[...]

    def score_tile(t, carry):
        lo, hi = carry
        k = ki_ref[pl.ds(pl.multiple_of(t * kt, kt), kt), :]
        sc = jnp.zeros((kt, qb), F32)
        for p in range(n_pairs):
            lg = _dot(k, pair_w[p])
            sc = (sc + jnp.maximum(lg[:, 0:qb], 0.0) * wt[2 * p:2 * p + 1, :]
                  + jnp.maximum(lg[:, qb:2 * qb], 0.0) * wt[2 * p + 1:2 * p + 2, :])
        kk = t * kt + lax.broadcasted_iota(I32, sc.shape, 0)
        valid = kk < n_allowed
        key_ref[t] = _sortable_key(jnp.where(valid, sc, NEG_INF))
        lo = jnp.minimum(lo, jnp.min(jnp.where(valid, sc, float("inf")), axis=0, keepdims=True))
        hi = jnp.maximum(hi, jnp.max(jnp.where(valid, sc, NEG_INF), axis=0, keepdims=True))
        return lo, hi

    lo_f, hi_f = lax.fori_loop(0, n_tiles, score_tile,
                               (jnp.full((1, qb), float("inf"), F32), jnp.full((1, qb), NEG_INF, F32)))

    keep_all = jnp.full((1, qb), NEG_INF_KEY + 1, I32)
    thr_ref[...] = keep_all

    @pl.when((g + 1) * qb > top_k)
    def _():
        def count_ge(mid):
            def body(t, cnt):
                hit = jnp.where(key_ref[t] >= mid, 1, 0)
                return cnt + jnp.sum(hit.reshape(kt // 8, 8, qb), axis=0)
            cnt = lax.fori_loop(0, n_tiles, body, jnp.zeros((8, qb), I32))
            return jnp.sum(cnt.astype(F32), axis=0, keepdims=True)

        def probe(lo, hi, mid):
            cnt = count_ge(mid)
            ge = cnt >= float(top_k)
            lo_next = jnp.where(ge, mid, lo)
            hi_next = jnp.where(cnt == float(top_k), mid, jnp.where(ge, hi, mid - 1))
            return lo_next, hi_next

        def bisect(st):
            lo, hi = st
            for _ in range(BISECT_STEPS_PER_CHECK):
                mid = (lo >> 1) + (hi >> 1) + ((lo | hi) & 1)
                lo, hi = probe(lo, hi, mid)
            return lo, hi

        def unresolved(st):
            lo, hi = st
            return jnp.max(jnp.where(hi > lo, 1.0, 0.0)) > 0.0

        lo, hi = _sortable_key(lo_f), _sortable_key(hi_f)
        for frac in BISECT_OPENING_FRACTIONS:
            split = jnp.minimum(jnp.maximum(_sortable_key(hi_f * frac), lo + 1), hi)
            lo, hi = probe(lo, hi, jnp.where(hi > lo, split, lo))
        lo, _ = lax.while_loop(unresolved, bisect, (lo, hi))
        thr_ref[...] = jnp.where(n_allowed > top_k, lo, keep_all)

    thr = thr_ref[...]
    m_ref[...] = jnp.full(m_ref.shape, NEG_INF, F32)
    l_ref[...] = jnp.zeros(l_ref.shape, F32)
    acc_ref[...] = jnp.zeros(acc_ref.shape, F32)
    hpg = HEADS_PER_GROUP
    group = hpg * qb

    def tile_bias(t):
        bias = jnp.where(key_ref[t] >= thr, 0.0, NEG_INF)
        return jnp.concatenate([bias] * hpg, axis=1)

    def scores(t, gi, bias_g):
        kv_rows = kv_ref[pl.ds(pl.multiple_of(t * kt, kt), kt), :]
        return _dot(kv_rows, qft_ref[0, :, gi * group:(gi + 1) * group]) + bias_g

    def softmax_pv(t, gi, s):
        m_old = m_ref[gi]
        m_new = jnp.maximum(m_old, jnp.max(s, axis=0, keepdims=True))
        m_safe = jnp.where(m_new == NEG_INF, 0.0, m_new)
        p = jnp.exp2(s - m_safe)
        alpha = jnp.exp2(m_old - m_safe)
        l_ref[gi] = alpha * l_ref[gi] + jnp.sum(p, axis=0, keepdims=True)
        acc_ref[gi] = alpha * acc_ref[gi] + _dot(kvt_ref[t], p.astype(BF16))
        m_ref[gi] = m_new

    def run(tiles):
        items = [(ti, gi) for ti in range(len(tiles)) for gi in range(MLA_HEADS // hpg)]
        ahead = 4
        bias, pending = {}, {}

        def issue(n):
            ti, gi = items[n]
            if gi == 0:
                bias[ti] = tile_bias(tiles[ti])
            pending[n] = scores(tiles[ti], gi, bias[ti])

        for n in range(min(ahead, len(items))):
            issue(n)
        for n, (ti, gi) in enumerate(items):
            if n + ahead < len(items):
                issue(n + ahead)
            softmax_pv(tiles[ti], gi, pending.pop(n))

    def tile_pair(tt, carry):
        run([2 * tt, 2 * tt + 1])
        return carry

    lax.fori_loop(0, n_tiles // 2, tile_pair, 0)

    @pl.when(n_tiles % 2 == 1)
    def _():
        run([n_tiles - 1])

    for gi in range(MLA_HEADS // hpg):
        o_lat_t = (acc_ref[gi] * (1.0 / l_ref[gi])).astype(BF16)
        for hl in range(hpg):
            hh = gi * hpg + hl
            o_t = _dot(wuvt_ref[hh], o_lat_t[:, hl * qb:(hl + 1) * qb])
            o_ref[:, hh * MLA_V_DIM:(hh + 1) * MLA_V_DIM] = o_t.T.astype(BF16)


def _dsa_attn(qft, kv, kvt, qit, ki, wit, w_uvt, bsz, seq):
    kt = KEY_TILE
    qb = QUERY_BLOCK
    ng = seq // qb
    n_pairs = IDX_HEADS // 2
    top_k = min(IDX_TOPK_MAX, seq // 4)
    kern = functools.partial(_dsa_attn_kernel, kt=kt, top_k=top_k)
    batch2 = lambda b, g: (b, 0)
    n_groups, group = MLA_HEADS // HEADS_PER_GROUP, HEADS_PER_GROUP * qb
    return pl.pallas_call(
        kern, grid=(bsz, ng),
        in_specs=[pl.BlockSpec((1, n_pairs, LANES, qb), lambda b, g: (b * ng + g, 0, 0, 0)),
                  pl.BlockSpec((IDX_HEADS, qb), lambda b, g: (0, b * ng + g)),
                  pl.BlockSpec((1, MLA_FEAT, MLA_HEADS * qb), lambda b, g: (b * ng + g, 0, 0)),
                  pl.BlockSpec((seq, LANES), batch2),
                  pl.BlockSpec((seq, MLA_FEAT), batch2),
                  pl.BlockSpec((seq // kt, MLA_KV_RANK, kt), lambda b, g: (b, 0, 0)),
                  pl.BlockSpec(w_uvt.shape, lambda b, g: (0, 0, 0))],
        out_specs=pl.BlockSpec((qb, MLA_HEADS * MLA_V_DIM), lambda b, g: (b * ng + g, 0)),
        out_shape=jax.ShapeDtypeStruct((bsz * seq, MLA_HEADS * MLA_V_DIM), BF16),
        scratch_shapes=[pltpu.VMEM((seq // kt, kt, qb), I32), pltpu.VMEM((1, qb), I32),
                        pltpu.VMEM((n_groups, 1, group), F32), pltpu.VMEM((n_groups, 1, group), F32),
                        pltpu.VMEM((n_groups, MLA_KV_RANK, group), F32)],
        compiler_params=_params(2), name="dsa_attn",
    )(qit, wit, qft, ki, kv, kvt, w_uvt)


def _diff_lambda_init(layer_idx):
    import math
    return 0.8 - 0.6 * math.exp(-0.3 * layer_idx)


def _pad_dsa_w_in(w_in):
    d = w_in.shape[0]
    o_cq, o_ckv = 0, MLA_Q_RANK
    o_kr = o_ckv + MLA_KV_RANK
    o_ki = o_kr + MLA_ROPE_DIM
    o_wi = o_ki + IDX_DIM
    out = jnp.zeros((d, _COL_END), w_in.dtype)
    out = out.at[:, _COL_CQ:_COL_CQ + MLA_Q_RANK].set(w_in[:, o_cq:o_ckv])
    out = out.at[:, _COL_CKV:_COL_CKV + MLA_KV_RANK].set(w_in[:, o_ckv:o_kr])
    out = out.at[:, _COL_KROPE:_COL_KROPE + MLA_ROPE_DIM].set(w_in[:, o_kr:o_ki])
    out = out.at[:, _COL_KIDX:_COL_KIDX + IDX_DIM].set(w_in[:, o_ki:o_wi])
    out = out.at[:, _COL_KIDX + IDX_DIM:_COL_KIDX + 2 * IDX_DIM].set(w_in[:, o_ki:o_wi])
    out = out.at[:, _COL_WIDX:_COL_WIDX + IDX_HEADS].set(w_in[:, o_wi:o_wi + IDX_HEADS])
    return out.astype(BF16)


def _router_weights(w_group, b_group, w_expert, b_expert):
    d = w_group.shape[0]
    w = jnp.zeros((d, LANES), F32)
    w = w.at[:, 0:N_GROUPS].set(w_group).at[:, N_GROUPS:N_GROUPS + N_EXPERTS].set(w_expert)
    br = jnp.zeros((1, LANES), F32)
    br = br.at[0, 0:N_GROUPS].set(b_group).at[0, N_GROUPS:N_GROUPS + N_EXPERTS].set(b_expert)
    w_hi = w.astype(BF16)
    w_lo = (w - w_hi.astype(F32)).astype(BF16)
    return w_hi, w_lo, br


def kernel(x, mem, positions, a_w_in, a_lambda, a_subln_g, a_w_out, b_w_in, b_q_norm_g, b_kv_norm_g,
           b_w_uq, b_w_qidx, b_w_uk, b_w_uv, b_w_out, mem_w_kv, xa_w_q, xa_w_out,
           moe_w_group, moe_b_group, moe_w_expert, moe_b_expert, moe_w_gate, moe_w_up, moe_w_down,
           ln_g, ln_b):
    bsz, seq, d = x.shape
    n_mem = mem.shape[1]
    depth = ln_g.shape[0]
    alpha = (2.0 * depth) ** 0.25
    assert seq % ROW_BLOCK == 0 and seq % KEY_TILE == 0
    assert ROW_BLOCK == ATTN_BLOCK and ROW_BLOCK == KEY_TILE and QUERY_BLOCK == 2 * CHUNK

    tabs_d = _rope_lane_tables(positions, DIFF_ROPE_DIM, DIFF_HEAD_DIM, True)
    tabs_r = _rope_lane_tables(positions, MLA_ROPE_DIM, LANES, False)
    tabs_i = _rope_lane_tables(positions, IDX_ROPE_DIM, IDX_DIM, True)

    memkv = _mem_kv(mem.reshape(bsz * n_mem, d), mem_w_kv.astype(BF16))

    h = x.reshape(bsz * seq, d)
    for i in range(depth):
        j = i // N_MIXERS
        if i % N_MIXERS == 0:
            qt, k, vt = _diff_qkv(h, a_w_in[j].astype(BF16), tabs_d)
            mix = _diff_attn(qt, k, vt, a_lambda[j], a_subln_g[j], bsz, seq, _diff_lambda_init(i))
            w_mix_out = a_w_out[j]
        else:
            w_ukt = jnp.pad(jnp.swapaxes(b_w_uk[j], 1, 2), ((0, 0), (MLA_ROPE_DIM, 0), (0, 0))).astype(BF16)
            qft, kv, kvt, qit, ki, wit = _dsa_proj(h, _pad_dsa_w_in(b_w_in[j]), b_q_norm_g[j], b_kv_norm_g[j],
                                                   b_w_uq[j].astype(BF16), b_w_qidx[j].astype(BF16), w_ukt,
                                                   tabs_r, tabs_i)
            w_uvt = jnp.swapaxes(b_w_uv[j], 1, 2).astype(BF16)
            mix = _dsa_attn(qft, kv, kvt, qit, ki, wit, w_uvt, bsz, seq)
            w_mix_out = b_w_out[j]
        wr_hi, wr_lo, br = _router_weights(moe_w_group[i], moe_b_group[i], moe_w_expert[i], moe_b_expert[i])
        ln_rows = jnp.stack([ln_g[i, 0], ln_b[i, 0], ln_g[i, 1], ln_b[i, 1]])
        aug, meta, counts = _layer_mid(mix, w_mix_out.astype(BF16), h, ln_rows, xa_w_q[i].astype(BF16), memkv,
                                       xa_w_out[i].astype(BF16), wr_hi, wr_lo, br, alpha, seq, n_mem)
        h = _moe(aug, meta, counts, moe_w_gate[i].astype(BF16), moe_w_up[i].astype(BF16),
                 moe_w_down[i].astype(BF16), ln_g[i, 2], ln_b[i, 2], alpha)
    return h.reshape(bsz, seq, d)
```

```python
import functools

import jax
import jax.numpy as jnp
from jax import lax
from jax.experimental import pallas as pl
from jax.experimental.pallas import tpu as pltpu

F32 = jnp.float32
BF16 = jnp.bfloat16
I32 = jnp.int32

CHUNK = 64
ROPE_THETA = 500000.0
LN_EPS = 1e-5
N_MIXERS = 2
DIFF_HEADS = 8
DIFF_HEAD_DIM = 64
DIFF_ROPE_DIM = DIFF_HEAD_DIM // 4
MLA_HEADS = 8
MLA_Q_RANK = 256
MLA_KV_RANK = 256
MLA_ROPE_DIM = 32
MLA_NOPE_DIM = 96
MLA_V_DIM = 128
IDX_HEADS = 16
IDX_DIM = 64
IDX_ROPE_DIM = IDX_DIM // 4
IDX_TOPK_MAX = 256
MEM_HEADS = 4
N_GROUPS = 4
EXPERTS_PER_GROUP = 4
N_EXPERTS = N_GROUPS * EXPERTS_PER_GROUP

LANES = 128
VMEM_LIMIT_BYTES = 56 * 1024 * 1024

ROW_BLOCK = 512
ATTN_BLOCK = 512
DIFF_BLOCKS_PER_TRIP = 4
KEY_TILE = 512
QUERY_BLOCK = 128
HEADS_PER_GROUP = 2
BISECT_STEPS_PER_CHECK = 4
BISECT_OPENING_FRACTIONS = (0.125, 0.5)
MLA_FEAT = MLA_KV_RANK + LANES

NEG_INF = float("-inf")
NEG_INF_KEY = -2139095041

_NT = (((1,), (1,)), ((), ()))


def _params(n_axes):
    return pltpu.CompilerParams(dimension_semantics=("arbitrary",) * n_axes,
                                vmem_limit_bytes=VMEM_LIMIT_BYTES)


def _dot(a, b):
    return jnp.dot(a, b, preferred_element_type=F32)


def _dot_nt(a, b):
    return lax.dot_general(a, b, _NT, preferred_element_type=F32)


def _layer_norm(z, g, b):
    mu = jnp.mean(z, axis=-1, keepdims=True)
    zc = z - mu
    var = jnp.mean(zc * zc, axis=-1, keepdims=True)
    return zc * lax.rsqrt(var + LN_EPS) * g + b


def _rms_norm(x, g):
    return x * lax.rsqrt(jnp.mean(x * x, axis=-1, keepdims=True) + LN_EPS) * g


def _rope(y, c, s_up, s_dn, half):
    return (y * c + pltpu.roll(y, LANES - half, 1) * s_up + pltpu.roll(y, half, 1) * s_dn)


def _rope_lane_tables(positions, rot_dim, period, keep_rest):
    half = rot_dim // 2
    inv_freq = ROPE_THETA ** (-jnp.arange(0, rot_dim, 2, dtype=F32) / rot_dim)
    ang = positions.astype(F32)[..., None] * inv_freq
    cos, sin = jnp.cos(ang), jnp.sin(ang)
    lead = positions.shape
    rest = jnp.full(lead + (period - rot_dim,), 1.0 if keep_rest else 0.0, F32)
    zrest = jnp.zeros(lead + (period - rot_dim,), F32)
    zhalf = jnp.zeros(lead + (half,), F32)
    reps = LANES // period
    out = []
    for parts in ((cos, cos, rest), (-sin, zhalf, zrest), (zhalf, sin, zrest)):
        t = jnp.concatenate(parts, axis=-1)
        out.append(jnp.tile(t, (1,) * len(lead) + (reps,)).reshape(-1, LANES))
    return out


LOG2E = 1.4426950408889634


def _diff_qkv_kernel(h_ref, w_ref, c_ref, su_ref, sd_ref, qt_ref, k_ref, vt_ref, *, q_scale, half):
    x = h_ref[...].astype(BF16)
    c, su, sd = c_ref[...], su_ref[...], sd_ref[...]
    n_heads = qt_ref.shape[0]
    for j2 in range(3 * n_heads // 2):
        y2 = _dot(x, w_ref[:, j2 * 2 * LANES:(j2 + 1) * 2 * LANES])
        for s in range(2):
            j = 2 * j2 + s
            y = y2[:, s * LANES:(s + 1) * LANES]
            if j < n_heads:
                qt_ref[j, 0] = (_rope(y, c, su, sd, half) * q_scale).T.astype(BF16)
            elif j < 2 * n_heads:
                k_ref[:, (j - n_heads) * LANES:(j - n_heads + 1) * LANES] = _rope(y, c, su, sd, half).astype(BF16)
            else:
                vt_ref[j - 2 * n_heads, 0] = y.T.astype(BF16)


def _diff_qkv(h2, w_in, tabs):
    t, d = h2.shape
    hd = 2 * DIFF_HEAD_DIM
    nb = t // ROW_BLOCK
    kern = functools.partial(_diff_qkv_kernel, q_scale=DIFF_HEAD_DIM ** -0.5 * LOG2E, half=DIFF_ROPE_DIM // 2)
    row = lambda i: (i, 0)
    full = lambda i: (0, 0)
    t_shape = jax.ShapeDtypeStruct((DIFF_HEADS, nb, hd, ROW_BLOCK), BF16)
    t_spec = pl.BlockSpec((DIFF_HEADS, 1, hd, ROW_BLOCK), lambda i: (0, i, 0, 0))
    return pl.pallas_call(
        kern, grid=(nb,),
        in_specs=[pl.BlockSpec((ROW_BLOCK, d), row), pl.BlockSpec(w_in.shape, full)]
        + [pl.BlockSpec((ROW_BLOCK, LANES), row)] * 3,
        out_specs=[t_spec, pl.BlockSpec((ROW_BLOCK, DIFF_HEADS * hd), row), t_spec],
        out_shape=[t_shape, jax.ShapeDtypeStruct((t, DIFF_HEADS * hd), BF16), t_shape],
        compiler_params=_params(1), name="diff_qkv",
    )(h2, w_in, *tabs)


def _diff_attn_kernel(lam_ref, g_ref, qt_ref, k_ref, vt_ref, o_ref, m_ref, l_ref, acc_ref, *,
                      blk, lambda_init):
    i = pl.program_id(2)
    lam = lam_ref[...]
    lam_full = (jnp.exp(jnp.sum(lam[0:1] * lam[1:2], axis=1, keepdims=True))
                - jnp.exp(jnp.sum(lam[2:3] * lam[3:4], axis=1, keepdims=True)) + lambda_init)
    qt = qt_ref[0, 0]
    feat = lax.broadcasted_iota(I32, qt.shape, 0)
    zero = jnp.zeros_like(qt)
    q_maps = (jnp.where(feat < DIFF_HEAD_DIM, qt, zero), jnp.where(feat >= DIFF_HEAD_DIM, qt, zero))
    m_ref[...] = jnp.full(m_ref.shape, NEG_INF, F32)
    l_ref[...] = jnp.zeros(l_ref.shape, F32)
    acc_ref[...] = jnp.zeros(acc_ref.shape, F32)

    half_w = blk // 2

    def scores(j, c, hf, masked):
        k = k_ref[pl.ds(pl.multiple_of(j * blk, blk), blk), :]
        s = _dot(k, q_maps[c][:, hf * half_w:(hf + 1) * half_w])
        if masked:
            kc = lax.broadcasted_iota(I32, s.shape, 0) // CHUNK
            qc = (hf * half_w + lax.broadcasted_iota(I32, s.shape, 1)) // CHUNK
            s = jnp.where(kc <= qc, s, NEG_INF)
        return s

    def softmax_pv(j, c, hf, s):
        idx = 2 * c + hf
        m_old = m_ref[idx]
        m_new = jnp.maximum(m_old, jnp.max(s, axis=0, keepdims=True))
        p = jnp.exp2(s - m_new)
        alpha = jnp.exp2(m_old - m_new)
        l_ref[idx] = alpha * l_ref[idx] + jnp.sum(p, axis=0, keepdims=True)
        acc_ref[idx] = alpha * acc_ref[idx] + _dot(vt_ref[0, j], p.astype(BF16))
        m_ref[idx] = m_new

    def run(blocks):
        items = [(j, c, hf, masked) for (j, masked) in blocks for c in range(2) for hf in range(2)]
        ahead = 4
        pending = {n: scores(*items[n]) for n in range(min(ahead, len(items)))}
        for n, (j, c, hf, _) in enumerate(items):
            if n + ahead < len(items):
                pending[n + ahead] = scores(*items[n + ahead])
            softmax_pv(j, c, hf, pending.pop(n))

    per_trip = DIFF_BLOCKS_PER_TRIP

    def full_group(jj, carry):
        run([(per_trip * jj + u, False) for u in range(per_trip)])
        return carry

    n_trips = i // per_trip
    lax.fori_loop(0, n_trips, full_group, 0)

    left = i - n_trips * per_trip
    for extra in range(per_trip):
        @pl.when(left == extra)
        def _(extra=extra):
            run([(i - extra + u, False) for u in range(extra)] + [(i, True)])

    norm = [acc_ref[idx] * (1.0 / l_ref[idx]) for idx in range(4)]
    ot = (jnp.concatenate(norm[0:2], axis=1) - lam_full * jnp.concatenate(norm[2:4], axis=1))
    ot = ot * lax.rsqrt(jnp.mean(ot * ot, axis=0, keepdims=True) + LN_EPS) * (1.0 - lambda_init)
    g = g_ref[...]
    for s in range(blk // LANES):
        o_ref[s * LANES:(s + 1) * LANES, :] = (ot[:, s * LANES:(s + 1) * LANES] * g).T.astype(BF16)


def _diff_attn(qt, k, vt, lam, subln_g, bsz, seq, lambda_init):
    blk = ATTN_BLOCK
    nq = seq // blk
    hd = 2 * DIFF_HEAD_DIM
    kern = functools.partial(_diff_attn_kernel, blk=blk, lambda_init=lambda_init)
    g = jnp.broadcast_to(subln_g.astype(F32)[:, None], (hd, LANES))
    return pl.pallas_call(
        kern, grid=(bsz, DIFF_HEADS, nq),
        in_specs=[
            pl.BlockSpec(lam.shape, lambda b, h, i: (0, 0)),
            pl.BlockSpec((hd, LANES), lambda b, h, i: (0, 0)),
            pl.BlockSpec((1, 1, hd, blk), lambda b, h, i: (h, b * nq + i, 0, 0)),
            pl.BlockSpec((seq, hd), lambda b, h, i: (b, h)),
            pl.BlockSpec((1, nq, hd, blk), lambda b, h, i: (h, b, 0, 0)),
        ],
        out_specs=pl.BlockSpec((blk, hd), lambda b, h, i: (b * nq + i, h)),
        out_shape=jax.ShapeDtypeStruct((bsz * seq, DIFF_HEADS * hd), BF16),
        scratch_shapes=[pltpu.VMEM((4, 1, blk // 2), F32), pltpu.VMEM((4, 1, blk // 2), F32),
                        pltpu.VMEM((4, hd, blk // 2), F32)],
        compiler_params=_params(3), name="diff_attn",
    )(lam, g, qt, k, vt)


def _mem_kv_kernel(m_ref, w_ref, o_ref):
    o_ref[...] = _dot(m_ref[...].astype(BF16), w_ref[...]).astype(BF16)


def _mem_kv(mem2, w_kv):
    rows, d = mem2.shape
    n = w_kv.shape[1]
    blk = min(ROW_BLOCK, rows)
    return pl.pallas_call(
        _mem_kv_kernel, grid=(rows // blk,),
        in_specs=[pl.BlockSpec((blk, d), lambda i: (i, 0)), pl.BlockSpec((d, n), lambda i: (0, 0))],
        out_specs=pl.BlockSpec((blk, n), lambda i: (i, 0)),
        out_shape=jax.ShapeDtypeStruct((rows, n), BF16),
        compiler_params=_params(1), name="mem_kv",
    )(mem2, w_kv)


def _cross_attn_block(h, wq_ref, kv_ref, wo_ref, g, b, alpha):
    d = h.shape[1]
    hd = d // MEM_HEADS
    q = (_dot(h.astype(BF16), wq_ref[...]) * hd ** -0.5).astype(BF16)
    outs = []
    for hh in range(MEM_HEADS):
        s = _dot_nt(q[:, hh * hd:(hh + 1) * hd], kv_ref[:, hh * hd:(hh + 1) * hd])
        p = jnp.exp(s - jnp.max(s, axis=1, keepdims=True))
        p = p * (1.0 / jnp.sum(p, axis=1, keepdims=True))
        outs.append(_dot(p.astype(BF16), kv_ref[:, d + hh * hd:d + (hh + 1) * hd]).astype(BF16))
    o = jnp.concatenate(outs, axis=1)
    return _layer_norm(alpha * h + _dot(o, wo_ref[...]), g, b)


def _route(h, wr_hi, wr_lo, br):
    h_hi = h.astype(BF16)
    h_lo = (h - h_hi.astype(F32)).astype(BF16)
    logits = _dot(h_hi, wr_hi) + _dot(h_hi, wr_lo) + _dot(h_lo, wr_hi) + br
    lane = lax.broadcasted_iota(I32, logits.shape, 1).astype(F32)
    gl = jnp.where(lane < N_GROUPS, logits, NEG_INF)
    gmax = jnp.max(gl, axis=1, keepdims=True)
    g_sel = jnp.min(jnp.where(gl == gmax, lane, float(LANES)), axis=1, keepdims=True)
    g_gate = 1.0 / jnp.sum(jnp.exp(gl - gmax), axis=1, keepdims=True)
    first = N_GROUPS + g_sel * EXPERTS_PER_GROUP
    el = jnp.where((lane >= first) & (lane < first + EXPERTS_PER_GROUP), logits, NEG_INF)
    v1 = jnp.max(el, axis=1, keepdims=True)
    i1 = jnp.min(jnp.where(el == v1, lane, float(LANES)), axis=1, keepdims=True)
    el2 = jnp.where(lane == i1, NEG_INF, el)
    v2 = jnp.max(el2, axis=1, keepdims=True)
    i2 = jnp.min(jnp.where(el2 == v2, lane, float(LANES)), axis=1, keepdims=True)
    r = jnp.exp(v2 - v1)
    w1 = g_gate / (1.0 + r)
    w2 = w1 * r
    first_is_low = i1 < i2
    e_lo = jnp.minimum(i1, i2) - first
    e_hi = jnp.maximum(i1, i2) - first
    return g_sel, e_lo, e_hi, jnp.where(first_is_low, w1, w2), jnp.where(first_is_low, w2, w1)


PAIRS_PER_GROUP = EXPERTS_PER_GROUP * (EXPERTS_PER_GROUP - 1) // 2
N_BINS = N_GROUPS * PAIRS_PER_GROUP
MOE_TILE = 256
W_LO_LANE, W_HI_LANE = 0, 64
META_ROWS = 8
DMA_LOOP_UNROLL = 8


def _bin_pairs():
    return [(lo, hi) for lo in range(EXPERTS_PER_GROUP) for hi in range(lo + 1, EXPERTS_PER_GROUP)]


def _route_block(h, wrh_ref, wrl_ref, br_ref, aug_ref, meta_ref, cnt_ref, run_ref):
    rows, d = h.shape
    g_sel, e_lo, e_hi, w_lo, w_hi = _route(h, wrh_ref[...], wrl_ref[...], br_ref[...])
    pid = e_lo * (7.0 - e_lo) * 0.5 + e_hi - e_lo - 1.0
    bin_id = g_sel * PAIRS_PER_GROUP + pid
    lane = lax.broadcasted_iota(I32, (rows, LANES), 1).astype(F32)
    onehot = jnp.where(lane == bin_id, 1.0, 0.0)
    r_i = lax.broadcasted_iota(I32, (rows, rows), 0)
    c_i = lax.broadcasted_iota(I32, (rows, rows), 1)
    tri = jnp.where(c_i < r_i, 1.0, 0.0).astype(BF16)
    before = _dot(tri, onehot.astype(BF16)) + run_ref[...]
    rank = jnp.sum(before * onehot, axis=1, keepdims=True)
    run_ref[...] += jnp.sum(onehot, axis=0, keepdims=True)
    cnt_ref[...] = jnp.broadcast_to(run_ref[...], cnt_ref.shape)

    aug_ref[:, 0:d] = h
    aug_ref[:, d:d + LANES] = jnp.where(lane < W_HI_LANE, w_lo, w_hi)
    meta = jnp.where(lane == 0.0, bin_id, jnp.where(lane == 1.0, rank, 0.0))
    meta_ref[0] = meta.T[0:META_ROWS, :].astype(I32)


def _layer_mid_kernel(mix_ref, wmix_ref, h_ref, ln_ref, wq_ref, kv_ref, wo_ref, wrh_ref, wrl_ref, br_ref,
                      aug_ref, meta_ref, cnt_ref, run_ref, *, alpha):
    @pl.when(pl.program_id(0) == 0)
    def _():
        run_ref[...] = jnp.zeros(run_ref.shape, F32)

    ln = ln_ref[...]
    h1 = _layer_norm(alpha * h_ref[...] + _dot(mix_ref[...], wmix_ref[...]), ln[0:1], ln[1:2])
    h2 = _cross_attn_block(h1, wq_ref, kv_ref, wo_ref, ln[2:3], ln[3:4], alpha)
    _route_block(h2, wrh_ref, wrl_ref, br_ref, aug_ref, meta_ref, cnt_ref, run_ref)


def _layer_mid(mix, w_mix, h2, ln_rows, w_q, memkv, w_out, wr_hi, wr_lo, br, alpha, seq, n_mem):
    t, d = h2.shape
    k = mix.shape[1]
    nb = t // ROW_BLOCK
    per_batch = seq // ROW_BLOCK
    row = lambda i: (i, 0)
    full = lambda i: (0, 0)
    return pl.pallas_call(
        functools.partial(_layer_mid_kernel, alpha=alpha), grid=(nb,),
        in_specs=[pl.BlockSpec((ROW_BLOCK, k), row), pl.BlockSpec((k, d), full),
                  pl.BlockSpec((ROW_BLOCK, d), row), pl.BlockSpec(ln_rows.shape, full),
                  pl.BlockSpec((d, d), full), pl.BlockSpec((n_mem, 2 * d), lambda i: (i // per_batch, 0)),
                  pl.BlockSpec((d, d), full), pl.BlockSpec((d, LANES), full),
                  pl.BlockSpec((d, LANES), full), pl.BlockSpec((1, LANES), full)],
        out_specs=[pl.BlockSpec((ROW_BLOCK, d + LANES), row),
                   pl.BlockSpec((1, META_ROWS, ROW_BLOCK), lambda i: (i, 0, 0)),
                   pl.BlockSpec((8, LANES), full)],
        out_shape=[jax.ShapeDtypeStruct((t, d + LANES), F32),
                   jax.ShapeDtypeStruct((nb, META_ROWS, ROW_BLOCK), I32),
                   jax.ShapeDtypeStruct((8, LANES), F32)],
        scratch_shapes=[pltpu.VMEM((1, LANES), F32)],
        compiler_params=_params(1), name="layer_mid",
    )(mix, w_mix, h2, ln_rows, w_q, memkv, w_out, wr_hi, wr_lo, br)


def _row_dma_loops(copy):
    def issue(r8, carry):
        for u in range(DMA_LOOP_UNROLL):
            copy(r8 * DMA_LOOP_UNROLL + u).start(priority=u % 2)
        return carry

    def drain(r8, carry):
        for u in range(DMA_LOOP_UNROLL):
            copy(r8 * DMA_LOOP_UNROLL + u).wait()
        return carry

    lax.fori_loop(0, ROW_BLOCK // DMA_LOOP_UNROLL, issue, 0)
    lax.fori_loop(0, ROW_BLOCK // DMA_LOOP_UNROLL, drain, 0)


def _moe_dispatch_kernel(off_ref, meta_ref, src_ref, init_ref, dst_ref, sem):
    del init_ref

    def copy(r):
        slot = off_ref[meta_ref[0, 0, r]] + meta_ref[0, 1, r]
        return pltpu.make_async_copy(src_ref.at[pl.ds(r, 1), :], dst_ref.at[pl.ds(slot, 1), :], sem)

    _row_dma_loops(copy)


def _moe_combine_kernel(off_ref, meta_ref, src_ref, dst_ref, sem):
    def copy(r):
        slot = off_ref[meta_ref[0, 0, r]] + meta_ref[0, 1, r]
        return pltpu.make_async_copy(src_ref.at[pl.ds(slot, 1), :], dst_ref.at[pl.ds(r, 1), :], sem)

    _row_dma_loops(copy)


def _moe_permute(off, meta, src, out_rows, init=None):
    nb = meta.shape[0]
    width = src.shape[1]
    dispatch = init is not None
    any_spec = pl.BlockSpec(memory_space=pl.ANY)
    block_spec = pl.BlockSpec((ROW_BLOCK, width), lambda i, off: (i, 0))
    meta_spec = pl.BlockSpec((1, META_ROWS, ROW_BLOCK), lambda i, off: (i, 0, 0), memory_space=pltpu.SMEM)
    grid_spec = pltpu.PrefetchScalarGridSpec(
        num_scalar_prefetch=1, grid=(nb,),
        in_specs=[meta_spec, block_spec, any_spec] if dispatch else [meta_spec, any_spec],
        out_specs=any_spec if dispatch else block_spec,
        scratch_shapes=[pltpu.SemaphoreType.DMA(())])
    return pl.pallas_call(
        _moe_dispatch_kernel if dispatch else _moe_combine_kernel, grid_spec=grid_spec,
        out_shape=jax.ShapeDtypeStruct((out_rows, width), src.dtype),
        input_output_aliases=({3: 0} if dispatch else {}),
        compiler_params=_params(1), name="moe_dispatch" if dispatch else "moe_combine",
    )(*([off, meta, src, init] if dispatch else [off, meta, src]))


def _moe_expert_kernel(e_lo_ref, e_hi_ref, nv_ref, s_ref, wg1_ref, wu1_ref, wd1_ref, wg2_ref, wu2_ref, wd2_ref,
                       g_ref, b_ref, o_ref, *, alpha):
    del e_lo_ref, e_hi_ref
    occupied = pl.program_id(0) < nv_ref[0]

    @pl.when(jnp.logical_not(occupied))
    def _():
        o_ref[...] = jnp.zeros(o_ref.shape, F32)

    @pl.when(occupied)
    def _():
        d = o_ref.shape[1]
        x = s_ref[:, 0:d]
        xb = x.astype(BF16)
        y = jnp.zeros(x.shape, F32)
        for lane0, wg_ref, wu_ref, wd_ref in ((W_LO_LANE, wg1_ref, wu1_ref, wd1_ref),
                                              (W_HI_LANE, wg2_ref, wu2_ref, wd2_ref)):
            c = s_ref[:, d + lane0:d + lane0 + 1]
            a = _dot(xb, wg_ref[0])
            u = _dot(xb, wu_ref[0])
            hid = a * (1.0 / (1.0 + jnp.exp(-a))) * u
            y = y + _dot((c * hid).astype(BF16), wd_ref[0])
        o_ref[...] = _layer_norm(alpha * x + y, g_ref[...], b_ref[...])


def _moe_experts(sorted_rows, tile_e_lo, tile_e_hi, n_valid, w_gate, w_up, w_down, g, b, alpha):
    rows, width = sorted_rows.shape
    d = width - LANES
    _, _, ff = w_gate.shape
    n_tiles = rows // MOE_TILE
    tile = lambda i, lo, hi, nv: (jnp.minimum(i, nv[0] - 1), 0)
    out_tile = lambda i, lo, hi, nv: (i, 0)
    w_lo = lambda i, lo, hi, nv: (lo[jnp.minimum(i, nv[0] - 1)], 0, 0)
    w_hi = lambda i, lo, hi, nv: (hi[jnp.minimum(i, nv[0] - 1)], 0, 0)
    full = lambda i, lo, hi, nv: (0, 0)
    grid_spec = pltpu.PrefetchScalarGridSpec(
        num_scalar_prefetch=3, grid=(n_tiles,),
        in_specs=[pl.BlockSpec((MOE_TILE, width), tile),
                  pl.BlockSpec((1, d, ff), w_lo), pl.BlockSpec((1, d, ff), w_lo), pl.BlockSpec((1, ff, d), w_lo),
                  pl.BlockSpec((1, d, ff), w_hi), pl.BlockSpec((1, d, ff), w_hi), pl.BlockSpec((1, ff, d), w_hi),
                  pl.BlockSpec((1, d), full), pl.BlockSpec((1, d), full)],
        out_specs=pl.BlockSpec((MOE_TILE, d), out_tile))
    return pl.pallas_call(
        functools.partial(_moe_expert_kernel, alpha=alpha), grid_spec=grid_spec,
        out_shape=jax.ShapeDtypeStruct((rows, d), F32),
        compiler_params=_params(1), name="moe_experts",
    )(tile_e_lo, tile_e_hi, n_valid, sorted_rows, w_gate, w_up, w_down, w_gate, w_up, w_down,
      g.reshape(1, d), b.reshape(1, d))


def _moe(aug, meta, counts, w_gate, w_up, w_down, g, b, alpha):
    t, d = aug.shape[0], aug.shape[1] - LANES
    cnt = counts[0, 0:N_BINS].astype(I32)
    padded = (cnt + MOE_TILE - 1) // MOE_TILE * MOE_TILE
    ends = jnp.cumsum(padded)
    off = (ends - padded).astype(I32)
    n_tiles = t // MOE_TILE + N_BINS
    tile_start = jnp.arange(n_tiles, dtype=I32) * MOE_TILE
    tile_bin = jnp.minimum(jnp.sum((ends[None, :] <= tile_start[:, None]).astype(I32), axis=1), N_BINS - 1)
    pairs = jnp.asarray(_bin_pairs(), I32)
    group = tile_bin // PAIRS_PER_GROUP
    tile_e_lo = group * EXPERTS_PER_GROUP + pairs[tile_bin % PAIRS_PER_GROUP, 0]
    tile_e_hi = group * EXPERTS_PER_GROUP + pairs[tile_bin % PAIRS_PER_GROUP, 1]
    n_valid = (ends[-1:] // MOE_TILE).astype(I32)
    sorted_rows = _moe_permute(off, meta, aug, n_tiles * MOE_TILE,
                               init=jnp.zeros((n_tiles * MOE_TILE, d + LANES), F32))
    out_sorted = _moe_experts(sorted_rows, tile_e_lo, tile_e_hi, n_valid, w_gate, w_up, w_down, g, b, alpha)
    return _moe_permute(off, meta, out_sorted, t)


_COL_CQ, _COL_CKV, _COL_KROPE, _COL_KIDX, _COL_WIDX, _COL_END = 0, 256, 512, 640, 768, 896


def _dsa_proj_kernel(h_ref, win_ref, gq_ref, gkv_ref, wuq_ref, wqi_ref, wuk_ref,
                     cr_ref, sur_ref, sdr_ref, ci_ref, sui_ref, sdi_ref,
                     qft_ref, kv_ref, kvt_ref, qit_ref, ki_ref, wit_ref, *, q_scale, w_scale):
    y = _dot(h_ref[...].astype(BF16), win_ref[...])
    cr, sur, sdr = cr_ref[...], sur_ref[...], sdr_ref[...]
    ci, sui, sdi = ci_ref[...], sui_ref[...], sdi_ref[...]
    half_r, half_i = MLA_ROPE_DIM // 2, IDX_ROPE_DIM // 2
    qb = QUERY_BLOCK
    n_qb = y.shape[0] // qb
    c_q = _rms_norm(y[:, _COL_CQ:_COL_CKV], gq_ref[...]).astype(BF16)
    c_kv = _rms_norm(y[:, _COL_CKV:_COL_KROPE], gkv_ref[...])
    kv_ref[:, 0:MLA_KV_RANK] = c_kv.astype(BF16)
    kvt_ref[0] = c_kv.T.astype(BF16)
    kv_ref[:, MLA_KV_RANK:MLA_FEAT] = _rope(y[:, _COL_KROPE:_COL_KIDX], cr, sur, sdr, half_r).astype(BF16)
    ki_ref[...] = _rope(y[:, _COL_KIDX:_COL_WIDX], ci, sui, sdi, half_i).astype(BF16)
    wit_ref[...] = (y[:, _COL_WIDX:_COL_END] * w_scale).T[0:IDX_HEADS, :]
    q = _dot(c_q, wuq_ref[...])
    q_bf = q.astype(BF16)
    hd = MLA_ROPE_DIM + MLA_NOPE_DIM
    for hh in range(MLA_HEADS):
        lat_t = (_dot(q_bf[:, hh * hd:(hh + 1) * hd], wuk_ref[hh]) * q_scale).T.astype(BF16)
        rope_t = (_rope(q[:, hh * hd:(hh + 1) * hd], cr, sur, sdr, half_r) * q_scale).T.astype(BF16)
        for bl in range(n_qb):
            qft_ref[bl, 0:MLA_KV_RANK, hh * qb:(hh + 1) * qb] = lat_t[:, bl * qb:(bl + 1) * qb]
            qft_ref[bl, MLA_KV_RANK:MLA_FEAT, hh * qb:(hh + 1) * qb] = rope_t[:, bl * qb:(bl + 1) * qb]
    qi = _dot(c_q, wqi_ref[...])
    for p in range(IDX_HEADS // 2):
        pair_t = _rope(qi[:, p * LANES:(p + 1) * LANES], ci, sui, sdi, half_i).T.astype(BF16)
        for bl in range(n_qb):
            qit_ref[bl, p] = pair_t[:, bl * qb:(bl + 1) * qb]


def _dsa_proj(h2, w_in_p, gq, gkv, w_uq, w_qidx, w_ukt, tabs_r, tabs_i):
    t, d = h2.shape
    row = lambda i: (i, 0)
    full = lambda i: (0, 0)
    full3 = lambda i: (0, 0, 0)
    n_pairs = IDX_HEADS // 2
    qb = QUERY_BLOCK
    n_qb = ROW_BLOCK // qb
    kern = functools.partial(_dsa_proj_kernel, q_scale=(MLA_ROPE_DIM + MLA_NOPE_DIM) ** -0.5 * LOG2E,
                             w_scale=(IDX_HEADS * IDX_DIM) ** -0.5)
    tab = pl.BlockSpec((ROW_BLOCK, LANES), row)
    return pl.pallas_call(
        kern, grid=(t // ROW_BLOCK,),
        in_specs=[pl.BlockSpec((ROW_BLOCK, d), row), pl.BlockSpec(w_in_p.shape, full),
                  pl.BlockSpec((1, MLA_Q_RANK), full), pl.BlockSpec((1, MLA_KV_RANK), full),
                  pl.BlockSpec(w_uq.shape, full), pl.BlockSpec(w_qidx.shape, full),
                  pl.BlockSpec(w_ukt.shape, full3)] + [tab] * 6,
        out_specs=[pl.BlockSpec((n_qb, MLA_FEAT, MLA_HEADS * qb), lambda i: (i, 0, 0)),
                   pl.BlockSpec((ROW_BLOCK, MLA_FEAT), row),
                   pl.BlockSpec((1, MLA_KV_RANK, ROW_BLOCK), lambda i: (i, 0, 0)),
                   pl.BlockSpec((n_qb, n_pairs, LANES, qb), lambda i: (i, 0, 0, 0)),
                   pl.BlockSpec((ROW_BLOCK, LANES), row),
                   pl.BlockSpec((IDX_HEADS, ROW_BLOCK), lambda i: (0, i))],
        out_shape=[jax.ShapeDtypeStruct((t // qb, MLA_FEAT, MLA_HEADS * qb), BF16),
                   jax.ShapeDtypeStruct((t, MLA_FEAT), BF16),
                   jax.ShapeDtypeStruct((t // ROW_BLOCK, MLA_KV_RANK, ROW_BLOCK), BF16),
                   jax.ShapeDtypeStruct((t // qb, n_pairs, LANES, qb), BF16),
                   jax.ShapeDtypeStruct((t, LANES), BF16),
                   jax.ShapeDtypeStruct((IDX_HEADS, t), F32)],
        compiler_params=_params(1), name="dsa_proj",
    )(h2, w_in_p, gq.reshape(1, -1), gkv.reshape(1, -1), w_uq, w_qidx, w_ukt, *tabs_r, *tabs_i)


def _sortable_key(x):
    bits = lax.bitcast_convert_type(x, I32)
    return bits ^ ((bits >> 31) & 0x7FFFFFFF)


def _dsa_attn_kernel(qit_ref, wit_ref, qft_ref, ki_ref, kv_ref, kvt_ref, wuvt_ref, o_ref,
                     key_ref, thr_ref, m_ref, l_ref, acc_ref, *, kt, top_k):
    g = pl.program_id(1)
    qb = QUERY_BLOCK
    n_tiles = ((g + 1) * qb + kt - 1) // kt
    n_pairs = IDX_HEADS // 2
    lane_q = lax.broadcasted_iota(I32, (1, qb), 1)
    n_allowed = (g * (qb // CHUNK) + 1 + lane_q // CHUNK) * CHUNK

    feat = lax.broadcasted_iota(I32, (LANES, qb), 0)
    pair_w = []
    for p in range(n_pairs):
        slab = qit_ref[0, p]
        zero = jnp.zeros_like(slab)
        pair_w.append(jnp.concatenate([jnp.where(feat < IDX_DIM, slab, zero),
                                       jnp.where(feat >= IDX_DIM, slab, zero)], axis=1))
    wt = wit_ref[...]

    def score_tile(t, carry):
        lo, hi = carry
        k = ki_ref[pl.ds(pl.multiple_of(t * kt, kt), kt), :]
        sc = jnp.zeros((kt, qb), F32)
        for p in range(n_pairs):
            lg = _dot(k, pair_w[p])
            sc = (sc + jnp.maximum(lg[:, 0:qb], 0.0) * wt[2 * p:2 * p + 1, :]
                  + jnp.maximum(lg[:, qb:2 * qb], 0.0) * wt[2 * p + 1:2 * p + 2, :])
        kk = t * kt + lax.broadcasted_iota(I32, sc.shape, 0)
        valid = kk < n_allowed
        key_ref[t] = _sortable_key(jnp.where(valid, sc, NEG_INF))
        lo = jnp.minimum(lo, jnp.min(jnp.where(valid, sc, float("inf")), axis=0, keepdims=True))
        hi = jnp.maximum(hi, jnp.max(jnp.where(valid, sc, NEG_INF), axis=0, keepdims=True))
        return lo, hi

    lo_f, hi_f = lax.fori_loop(0, n_tiles, score_tile,
                               (jnp.full((1, qb), float("inf"), F32), jnp.full((1, qb), NEG_INF, F32)))

    keep_all = jnp.full((1, qb), NEG_INF_KEY + 1, I32)
    thr_ref[...] = keep_all

    @pl.when((g + 1) * qb > top_k)
    def _():
        def count_ge(mid):
            def body(t, cnt):
                hit = jnp.where(key_ref[t] >= mid, 1, 0)
                return cnt + jnp.sum(hit.reshape(kt // 8, 8, qb), axis=0)
            cnt = lax.fori_loop(0, n_tiles, body, jnp.zeros((8, qb), I32))
            return jnp.sum(cnt.astype(F32), axis=0, keepdims=True)

        def probe(lo, hi, mid):
            cnt = count_ge(mid)
            ge = cnt >= float(top_k)
            lo_next = jnp.where(ge, mid, lo)
            hi_next = jnp.where(cnt == float(top_k), mid, jnp.where(ge, hi, mid - 1))
            return lo_next, hi_next

        def bisect(st):
            lo, hi = st
            for _ in range(BISECT_STEPS_PER_CHECK):
                mid = (lo >> 1) + (hi >> 1) + ((lo | hi) & 1)
                lo, hi = probe(lo, hi, mid)
            return lo, hi

        def unresolved(st):
            lo, hi = st
            return jnp.max(jnp.where(hi > lo, 1.0, 0.0)) > 0.0

        lo, hi = _sortable_key(lo_f), _sortable_key(hi_f)
        for frac in BISECT_OPENING_FRACTIONS:
            split = jnp.minimum(jnp.maximum(_sortable_key(hi_f * frac), lo + 1), hi)
            lo, hi = probe(lo, hi, jnp.where(hi > lo, split, lo))
        lo, _ = lax.while_loop(unresolved, bisect, (lo, hi))
        thr_ref[...] = jnp.where(n_allowed > top_k, lo, keep_all)

    thr = thr_ref[...]
    m_ref[...] = jnp.full(m_ref.shape, NEG_INF, F32)
    l_ref[...] = jnp.zeros(l_ref.shape, F32)
    acc_ref[...] = jnp.zeros(acc_ref.shape, F32)
    hpg = HEADS_PER_GROUP
    group = hpg * qb

    def tile_bias(t):
        bias = jnp.where(key_ref[t] >= thr, 0.0, NEG_INF)
        return jnp.concatenate([bias] * hpg, axis=1)

    def scores(t, gi, bias_g):
        kv_rows = kv_ref[pl.ds(pl.multiple_of(t * kt, kt), kt), :]
        return _dot(kv_rows, qft_ref[0, :, gi * group:(gi + 1) * group]) + bias_g

    def softmax_pv(t, gi, s):
        m_old = m_ref[gi]
        m_new = jnp.maximum(m_old, jnp.max(s, axis=0, keepdims=True))
        m_safe = jnp.where(m_new == NEG_INF, 0.0, m_new)
        p = jnp.exp2(s - m_safe)
        alpha = jnp.exp2(m_old - m_safe)
        l_ref[gi] = alpha * l_ref[gi] + jnp.sum(p, axis=0, keepdims=True)
        acc_ref[gi] = alpha * acc_ref[gi] + _dot(kvt_ref[t], p.astype(BF16))
        m_ref[gi] = m_new

    def run(tiles):
        items = [(ti, gi) for ti in range(len(tiles)) for gi in range(MLA_HEADS // hpg)]
        ahead = 4
        bias, pending = {}, {}

        def issue(n):
            ti, gi = items[n]
            if gi == 0:
                bias[ti] = tile_bias(tiles[ti])
            pending[n] = scores(tiles[ti], gi, bias[ti])

        for n in range(min(ahead, len(items))):
            issue(n)
        for n, (ti, gi) in enumerate(items):
            if n + ahead < len(items):
                issue(n + ahead)
            softmax_pv(tiles[ti], gi, pending.pop(n))

    def tile_pair(tt, carry):
        run([2 * tt, 2 * tt + 1])
        return carry

    lax.fori_loop(0, n_tiles // 2, tile_pair, 0)

    @pl.when(n_tiles % 2 == 1)
    def _():
        run([n_tiles - 1])

    for gi in range(MLA_HEADS // hpg):
        o_lat_t = (acc_ref[gi] * (1.0 / l_ref[gi])).astype(BF16)
        for hl in range(hpg):
            hh = gi * hpg + hl
            o_t = _dot(wuvt_ref[hh], o_lat_t[:, hl * qb:(hl + 1) * qb])
            o_ref[:, hh * MLA_V_DIM:(hh + 1) * MLA_V_DIM] = o_t.T.astype(BF16)


def _dsa_attn(qft, kv, kvt, qit, ki, wit, w_uvt, bsz, seq):
    kt = KEY_TILE
    qb = QUERY_BLOCK
    ng = seq // qb
    n_pairs = IDX_HEADS // 2
    top_k = min(IDX_TOPK_MAX, seq // 4)
    kern = functools.partial(_dsa_attn_kernel, kt=kt, top_k=top_k)
    batch2 = lambda b, g: (b, 0)
    n_groups, group = MLA_HEADS // HEADS_PER_GROUP, HEADS_PER_GROUP * qb
    return pl.pallas_call(
        kern, grid=(bsz, ng),
        in_specs=[pl.BlockSpec((1, n_pairs, LANES, qb), lambda b, g: (b * ng + g, 0, 0, 0)),
                  pl.BlockSpec((IDX_HEADS, qb), lambda b, g: (0, b * ng + g)),
                  pl.BlockSpec((1, MLA_FEAT, MLA_HEADS * qb), lambda b, g: (b * ng + g, 0, 0)),
                  pl.BlockSpec((seq, LANES), batch2),
                  pl.BlockSpec((seq, MLA_FEAT), batch2),
                  pl.BlockSpec((seq // kt, MLA_KV_RANK, kt), lambda b, g: (b, 0, 0)),
                  pl.BlockSpec(w_uvt.shape, lambda b, g: (0, 0, 0))],
        out_specs=pl.BlockSpec((qb, MLA_HEADS * MLA_V_DIM), lambda b, g: (b * ng + g, 0)),
        out_shape=jax.ShapeDtypeStruct((bsz * seq, MLA_HEADS * MLA_V_DIM), BF16),
        scratch_shapes=[pltpu.VMEM((seq // kt, kt, qb), I32), pltpu.VMEM((1, qb), I32),
                        pltpu.VMEM((n_groups, 1, group), F32), pltpu.VMEM((n_groups, 1, group), F32),
                        pltpu.VMEM((n_groups, MLA_KV_RANK, group), F32)],
        compiler_params=_params(2), name="dsa_attn",
    )(qit, wit, qft, ki, kv, kvt, w_uvt)


def _diff_lambda_init(layer_idx):
    import math
    return 0.8 - 0.6 * math.exp(-0.3 * layer_idx)


def _pad_dsa_w_in(w_in):
    d = w_in.shape[0]
    o_cq, o_ckv = 0, MLA_Q_RANK
    o_kr = o_ckv + MLA_KV_RANK
    o_ki = o_kr + MLA_ROPE_DIM
    o_wi = o_ki + IDX_DIM
    out = jnp.zeros((d, _COL_END), w_in.dtype)
    out = out.at[:, _COL_CQ:_COL_CQ + MLA_Q_RANK].set(w_in[:, o_cq:o_ckv])
    out = out.at[:, _COL_CKV:_COL_CKV + MLA_KV_RANK].set(w_in[:, o_ckv:o_kr])
    out = out.at[:, _COL_KROPE:_COL_KROPE + MLA_ROPE_DIM].set(w_in[:, o_kr:o_ki])
    out = out.at[:, _COL_KIDX:_COL_KIDX + IDX_DIM].set(w_in[:, o_ki:o_wi])
    out = out.at[:, _COL_KIDX + IDX_DIM:_COL_KIDX + 2 * IDX_DIM].set(w_in[:, o_ki:o_wi])
    out = out.at[:, _COL_WIDX:_COL_WIDX + IDX_HEADS].set(w_in[:, o_wi:o_wi + IDX_HEADS])
    return out.astype(BF16)


def _router_weights(w_group, b_group, w_expert, b_expert):
    d = w_group.shape[0]
    w = jnp.zeros((d, LANES), F32)
    w = w.at[:, 0:N_GROUPS].set(w_group).at[:, N_GROUPS:N_GROUPS + N_EXPERTS].set(w_expert)
    br = jnp.zeros((1, LANES), F32)
    br = br.at[0, 0:N_GROUPS].set(b_group).at[0, N_GROUPS:N_GROUPS + N_EXPERTS].set(b_expert)
    w_hi = w.astype(BF16)
    w_lo = (w - w_hi.astype(F32)).astype(BF16)
    return w_hi, w_lo, br


def kernel(x, mem, positions, a_w_in, a_lambda, a_subln_g, a_w_out, b_w_in, b_q_norm_g, b_kv_norm_g,
           b_w_uq, b_w_qidx, b_w_uk, b_w_uv, b_w_out, mem_w_kv, xa_w_q, xa_w_out,
           moe_w_group, moe_b_group, moe_w_expert, moe_b_expert, moe_w_gate, moe_w_up, moe_w_down,
           ln_g, ln_b):
    bsz, seq, d = x.shape
    n_mem = mem.shape[1]
    depth = ln_g.shape[0]
    alpha = (2.0 * depth) ** 0.25
    assert seq % ROW_BLOCK == 0 and seq % KEY_TILE == 0
    assert ROW_BLOCK == ATTN_BLOCK and ROW_BLOCK == KEY_TILE and QUERY_BLOCK == 2 * CHUNK

    tabs_d = _rope_lane_tables(positions, DIFF_ROPE_DIM, DIFF_HEAD_DIM, True)
    tabs_r = _rope_lane_tables(positions, MLA_ROPE_DIM, LANES, False)
    tabs_i = _rope_lane_tables(positions, IDX_ROPE_DIM, IDX_DIM, True)

    memkv = _mem_kv(mem.reshape(bsz * n_mem, d), mem_w_kv.astype(BF16))

    h = x.reshape(bsz * seq, d)
    for i in range(depth):
        j = i // N_MIXERS
        if i % N_MIXERS == 0:
            qt, k, vt = _diff_qkv(h, a_w_in[j].astype(BF16), tabs_d)
            mix = _diff_attn(qt, k, vt, a_lambda[j], a_subln_g[j], bsz, seq, _diff_lambda_init(i))
            w_mix_out = a_w_out[j]
        else:
            w_ukt = jnp.pad(jnp.swapaxes(b_w_uk[j], 1, 2), ((0, 0), (MLA_ROPE_DIM, 0), (0, 0))).astype(BF16)
            qft, kv, kvt, qit, ki, wit = _dsa_proj(h, _pad_dsa_w_in(b_w_in[j]), b_q_norm_g[j], b_kv_norm_g[j],
                                                   b_w_uq[j].astype(BF16), b_w_qidx[j].astype(BF16), w_ukt,
                                                   tabs_r, tabs_i)
            w_uvt = jnp.swapaxes(b_w_uv[j], 1, 2).astype(BF16)
            mix = _dsa_attn(qft, kv, kvt, qit, ki, wit, w_uvt, bsz, seq)
            w_mix_out = b_w_out[j]
        wr_hi, wr_lo, br = _router_weights(moe_w_group[i], moe_b_group[i], moe_w_expert[i], moe_b_expert[i])
        ln_rows = jnp.stack([ln_g[i, 0], ln_b[i, 0], ln_g[i, 1], ln_b[i, 1]])
        aug, meta, counts = _layer_mid(mix, w_mix_out.astype(BF16), h, ln_rows, xa_w_q[i].astype(BF16), memkv,
                                       xa_w_out[i].astype(BF16), wr_hi, wr_lo, br, alpha, seq, n_mem)
        h = _moe(aug, meta, counts, moe_w_gate[i].astype(BF16), moe_w_up[i].astype(BF16),
                 moe_w_down[i].astype(BF16), ln_g[i, 2], ln_b[i, 2], alpha)
    return h.reshape(bsz, seq, d)
```
